```python
import jax, jax.numpy as jnp
from jax import lax
import numpy as np

D_MODEL = 1024
BATCH = 8
SEQ = 2048
DEPTH = 1

N_META = 16
POOL_WINDOWS = (2, 4, 8, 16)
N_POOL_GROUPS = 4
POOL_WIDTH = D_MODEL
POOL_GROUP_DIM = POOL_WIDTH // N_POOL_GROUPS
DN_HEADS = 8
DN_HEAD_DIM = 128
DN_WIDTH = DN_HEADS * DN_HEAD_DIM
CONV_WIDTH = 4
CHUNK = 64
N_EXPERT_GROUPS = 4
EXPERTS_PER_GROUP = 8
N_EXPERTS = N_EXPERT_GROUPS * EXPERTS_PER_GROUP
TOP_K_IN_GROUP = 2
D_FF_EXPERT = 512
NORM_EPS = 1e-6
PROJ_SPLITS = [POOL_WIDTH, POOL_WIDTH + 3 * DN_WIDTH, POOL_WIDTH + 4 * DN_WIDTH,
               POOL_WIDTH + 4 * DN_WIDTH + DN_HEADS, POOL_WIDTH + 4 * DN_WIDTH + 2 * DN_HEADS,
               POOL_WIDTH + 4 * DN_WIDTH + 2 * DN_HEADS + D_MODEL]
PROJ_COLS = POOL_WIDTH + 4 * DN_WIDTH + 2 * DN_HEADS + 2 * D_MODEL

kernel_name = "hybrid_pool_gdn_hiermoe_block"


def rmsnorm(x, w):
    xf = x.astype(jnp.float32)
    y = xf * lax.rsqrt(jnp.mean(xf * xf, axis=-1, keepdims=True) + NORM_EPS)
    return (y * w.astype(jnp.float32)).astype(x.dtype)


def l2norm(t):
    return t * lax.rsqrt(jnp.sum(t * t, axis=-1, keepdims=True) + NORM_EPS)


def multiscale_pool_branch(u, pool_w, pool_scale):
    Bsz, L, _ = u.shape
    uf = u.astype(jnp.float32)
    cs = jnp.concatenate([jnp.zeros((Bsz, 1, POOL_WIDTH), jnp.float32), jnp.cumsum(uf, axis=1)], axis=1)
    t = jnp.arange(L)
    outs = []
    for gi, win in enumerate(POOL_WINDOWS):
        cs_g = cs[:, :, gi * POOL_GROUP_DIM:(gi + 1) * POOL_GROUP_DIM]
        start = jnp.maximum(t + 1 - win, 0)
        window_sum = cs_g[:, 1:] - jnp.take(cs_g, start, axis=1)
        count = (t + 1 - start).astype(jnp.float32)
        outs.append(window_sum / count[None, :, None])
    pooled = jnp.concatenate(outs, axis=-1) - uf
    pooled = pooled.reshape(Bsz, L, N_POOL_GROUPS, POOL_GROUP_DIM)
    mixed = jnp.einsum('blgc,gcd->blgd', pooled, pool_w.astype(jnp.float32)).reshape(Bsz, L, POOL_WIDTH)
    return (mixed * pool_scale.astype(jnp.float32)).astype(u.dtype)


def causal_depthwise_conv(x, w):
    C = x.shape[-1]
    return lax.conv_general_dilated(x, w[:, None, :].astype(x.dtype), window_strides=(1,),
                                    padding=[(CONV_WIDTH - 1, 0)],
                                    dimension_numbers=('NWC', 'WIO', 'NWC'), feature_group_count=C)


def chunk_gated_delta_rule(q, k, v, g, beta, S0, chunk):
    Bsz, H, T, dk = q.shape
    dv = v.shape[-1]
    n = T // chunk
    q = q.reshape(Bsz, H, n, chunk, dk) * (dk ** -0.5)
    k = k.reshape(Bsz, H, n, chunk, dk)
    v = v.reshape(Bsz, H, n, chunk, dv)
    beta = beta.reshape(Bsz, H, n, chunk)
    gam = jnp.cumsum(g.reshape(Bsz, H, n, chunk), axis=-1)
    idx = jnp.arange(chunk)
    lower_incl = idx[:, None] >= idx[None, :]
    strict = idx[:, None] > idx[None, :]
    decay_mat = jnp.exp(jnp.where(lower_incl, gam[..., :, None] - gam[..., None, :], -jnp.inf))
    kk = jnp.einsum('bhncd,bhnsd->bhncs', k, k)
    A = jnp.where(strict, beta[..., :, None] * kk * decay_mat, 0.0)
    rhs = jnp.concatenate([v * beta[..., None], k * (beta * jnp.exp(gam))[..., None]], axis=-1)
    sol = lax.linalg.triangular_solve(A, rhs, left_side=True, lower=True, unit_diagonal=True)
    u_w, w_w = sol[..., :dv], sol[..., dv:]
    qk = jnp.einsum('bhncd,bhnsd->bhncs', q, k) * decay_mat
    q_dec = q * jnp.exp(gam)[..., None]
    k_dec = k * jnp.exp(gam[..., -1:] - gam)[..., None]
    chunk_decay = jnp.exp(gam[..., -1])

    def step(S, xs):
        u_c, w_c, qk_c, qd_c, kd_c, cd_c = xs
        v_new = u_c - jnp.einsum('bhck,bhkv->bhcv', w_c, S)
        o_c = jnp.einsum('bhck,bhkv->bhcv', qd_c, S) + jnp.einsum('bhcs,bhsv->bhcv', qk_c, v_new)
        S = S * cd_c[..., None, None] + jnp.einsum('bhck,bhcv->bhkv', kd_c, v_new)
        return S, o_c

    xs = tuple(jnp.moveaxis(t_, 2, 0) for t_ in (u_w, w_w, qk, q_dec, k_dec, chunk_decay))
    S_final, o = lax.scan(step, S0, xs)
    o = jnp.moveaxis(o, 0, 2).reshape(Bsz, H, T, dv)
    return o, S_final


def gated_deltanet_branch(p_qkv, p_z, p_a, p_b, conv_w, a_log, dt_bias, norm_w):
    Bsz, L, _ = p_qkv.shape
    qkv = jax.nn.silu(causal_depthwise_conv(p_qkv, conv_w))
    q, k, v = jnp.split(qkv, 3, axis=-1)

    def heads(t):
        return t.reshape(Bsz, L, DN_HEADS, DN_HEAD_DIM).transpose(0, 2, 1, 3).astype(jnp.float32)

    q = l2norm(heads(q))
    k = l2norm(heads(k))
    v = heads(v)
    beta = jax.nn.sigmoid(p_b.astype(jnp.float32)).transpose(0, 2, 1)
    g = (-jnp.exp(a_log.astype(jnp.float32))
         * jax.nn.softplus(p_a.astype(jnp.float32) + dt_bias.astype(jnp.float32))).transpose(0, 2, 1)
    S0 = jnp.zeros((Bsz, DN_HEADS, DN_HEAD_DIM, DN_HEAD_DIM), jnp.float32)
    o_meta, S_meta = chunk_gated_delta_rule(q[:, :, :N_META], k[:, :, :N_META], v[:, :, :N_META],
                                            g[:, :, :N_META], beta[:, :, :N_META], S0, N_META)
    o_real, _ = chunk_gated_delta_rule(q[:, :, N_META:], k[:, :, N_META:], v[:, :, N_META:],
                                       g[:, :, N_META:], beta[:, :, N_META:], S_meta, CHUNK)
    o = jnp.concatenate([o_meta, o_real], axis=2).transpose(0, 2, 1, 3)
    o = o * lax.rsqrt(jnp.mean(o * o, axis=-1, keepdims=True) + NORM_EPS) * norm_w.astype(jnp.float32)
    z = p_z.astype(jnp.float32).reshape(Bsz, L, DN_HEADS, DN_HEAD_DIM)
    o = o * jax.nn.silu(z)
    return o.reshape(Bsz, L, DN_WIDTH).astype(p_qkv.dtype)


def hierarchical_moe(u, w_grp, b_grp, w_rtr, b_rtr, w_gate, w_up, w_down):
    Bsz, L, D = u.shape
    xt = u.reshape(-1, D)
    grp_probs = jax.nn.softmax((xt @ w_grp + b_grp).astype(jnp.float32), axis=-1)
    p_grp, g_idx = lax.top_k(grp_probs, 1)
    exp_logits = (xt @ w_rtr + b_rtr).astype(jnp.float32).reshape(-1, N_EXPERT_GROUPS, EXPERTS_PER_GROUP)
    sel_logits = jnp.take_along_axis(exp_logits, g_idx[:, :, None], axis=1)[:, 0]
    w_e, e_idx = lax.top_k(jax.nn.softmax(sel_logits, axis=-1), TOP_K_IN_GROUP)
    w_e = w_e / jnp.sum(w_e, axis=-1, keepdims=True)
    exp_weight = jnp.sum(jax.nn.one_hot(e_idx, EXPERTS_PER_GROUP, dtype=jnp.float32) * w_e[..., None], axis=1)
    comb = (jax.nn.one_hot(g_idx[:, 0], N_EXPERT_GROUPS, dtype=jnp.float32)[:, :, None]
            * (p_grp * exp_weight)[:, None, :])
    y = jnp.zeros_like(xt)
    for gi in range(N_EXPERT_GROUPS):
        hdn = (jax.nn.silu(jnp.einsum('nd,edf->nef', xt, w_gate[gi]))
               * jnp.einsum('nd,edf->nef', xt, w_up[gi]))
        hdn = hdn * comb[:, gi, :, None].astype(hdn.dtype)
        y = y + jnp.einsum('nef,efd->nd', hdn, w_down[gi])
    return y.reshape(Bsz, L, D)


def setup_inputs(seed: int = 0) -> dict:
    key = jax.random.key(seed)
    ks = jax.random.split(key, 24)
    f32 = jnp.float32
    dt = jnp.exp(jax.random.uniform(ks[8], (DEPTH, DN_HEADS), f32, np.log(0.001), np.log(0.1)))
    return {
        'x': jax.random.normal(ks[0], (BATCH, SEQ, D_MODEL), f32),
        'meta_tokens': jax.random.normal(ks[1], (N_META, D_MODEL), f32),
        'norm_mix_w': 1.0 + 0.02 * jax.random.normal(ks[2], (DEPTH, D_MODEL), f32),
        'w_in': jax.random.normal(ks[3], (DEPTH, D_MODEL, PROJ_COLS), f32) * D_MODEL ** -0.5,
        'conv_w': jax.random.normal(ks[4], (DEPTH, CONV_WIDTH, 3 * DN_WIDTH), f32) * CONV_WIDTH ** -0.5,
        'pool_w': jax.random.normal(ks[5], (DEPTH, N_POOL_GROUPS, POOL_GROUP_DIM, POOL_GROUP_DIM), f32) * POOL_GROUP_DIM ** -0.5,
        'pool_scale': 1.0 + 0.1 * jax.random.normal(ks[6], (DEPTH, POOL_WIDTH), f32),
        'a_log': jnp.log(jax.random.uniform(ks[7], (DEPTH, DN_HEADS), f32, 1.0, 16.0)),
        'dt_bias': dt + jnp.log(-jnp.expm1(-dt)),
        'dn_norm_w': 1.0 + 0.02 * jax.random.normal(ks[9], (DEPTH, DN_HEAD_DIM), f32),
        'w_out': jax.random.normal(ks[10], (DEPTH, D_MODEL, D_MODEL), f32) * D_MODEL ** -0.5,
        'norm_ffn_w': 1.0 + 0.02 * jax.random.normal(ks[11], (DEPTH, D_MODEL), f32),
        'router_group_w': jax.random.normal(ks[12], (DEPTH, D_MODEL, N_EXPERT_GROUPS), f32) * D_MODEL ** -0.5,
        'router_group_b': 0.01 * jax.random.normal(ks[13], (DEPTH, N_EXPERT_GROUPS), f32),
        'router_expert_w': jax.random.normal(ks[14], (DEPTH, D_MODEL, N_EXPERTS), f32) * D_MODEL ** -0.5,
        'router_expert_b': 0.01 * jax.random.normal(ks[15], (DEPTH, N_EXPERTS), f32),
        'expert_w_gate': jax.random.normal(ks[16], (DEPTH, N_EXPERT_GROUPS, EXPERTS_PER_GROUP, D_MODEL, D_FF_EXPERT), f32) * D_MODEL ** -0.5,
        'expert_w_up': jax.random.normal(ks[17], (DEPTH, N_EXPERT_GROUPS, EXPERTS_PER_GROUP, D_MODEL, D_FF_EXPERT), f32) * D_MODEL ** -0.5,
        'expert_w_down': jax.random.normal(ks[18], (DEPTH, N_EXPERT_GROUPS, EXPERTS_PER_GROUP, D_FF_EXPERT, D_MODEL), f32) * D_FF_EXPERT ** -0.5,
        'norm_final_w': 1.0 + 0.02 * jax.random.normal(ks[19], (D_MODEL,), f32),
    }


def reference(x, meta_tokens, norm_mix_w, w_in, conv_w, pool_w, pool_scale, a_log, dt_bias, dn_norm_w,
              w_out, norm_ffn_w, router_group_w, router_group_b, router_expert_w, router_expert_b,
              expert_w_gate, expert_w_up, expert_w_down, norm_final_w):
    Bsz = x.shape[0]
    meta = jnp.broadcast_to(meta_tokens[None].astype(x.dtype), (Bsz, N_META, D_MODEL))
    h = jnp.concatenate([meta, x], axis=1)
    for layer in range(DEPTH):
        u = rmsnorm(h, norm_mix_w[layer])
        p = u @ w_in[layer]
        p_pool, p_qkv, p_z, p_a, p_b, g_pool, g_dn = jnp.split(p, PROJ_SPLITS, axis=-1)
        y_pool = multiscale_pool_branch(p_pool, pool_w[layer], pool_scale[layer])
        y_dn = gated_deltanet_branch(p_qkv, p_z, p_a, p_b, conv_w[layer], a_log[layer], dt_bias[layer],
                                     dn_norm_w[layer])
        merged = jax.nn.sigmoid(g_pool) * y_pool + jax.nn.sigmoid(g_dn) * y_dn
        h = h + merged @ w_out[layer]
        h = h + hierarchical_moe(rmsnorm(h, norm_ffn_w[layer]), router_group_w[layer], router_group_b[layer],
                                 router_expert_w[layer], router_expert_b[layer], expert_w_gate[layer],
                                 expert_w_up[layer], expert_w_down[layer])
    out = rmsnorm(h, norm_final_w)
    return out[:, N_META:]
```

```python
import functools
import math

import jax
import jax.numpy as jnp
from jax import lax
from jax.experimental import pallas as pl
from jax.experimental.pallas import tpu as pltpu

F32 = jnp.float32
BF16 = jnp.bfloat16

D_MODEL = 1024
N_META = 16
POOL_WINDOWS = (2, 4, 8, 16)
POOL_GROUP_DIM = 256
DN_HEADS = 8
DN_HEAD_DIM = 128
CONV_WIDTH = 4
CHUNK = 64
N_EXPERT_GROUPS = 4
EXPERTS_PER_GROUP = 8
N_EXPERTS = 32
D_FF_EXPERT = 512
NORM_EPS = 1e-6

LANES = 128
SUBLANES = 8
P_MAIN_COLS = 7 * D_MODEL
POOL_HALO = 16
CONV_HALO = 8
VMEM_LIMIT = 56 * 1024 * 1024


def _dot(a, b):
    return jnp.dot(a, b, preferred_element_type=F32)


def _dot_nt(a, b):
    return lax.dot_general(a, b, (((1,), (1,)), ((), ())), preferred_element_type=F32)


def _dot_tn(a, b):
    return lax.dot_general(a, b, (((0,), (0,)), ((), ())), preferred_element_type=F32)


def _sigmoid(x):
    return 1.0 / (1.0 + jnp.exp(-x))


def _inproj_kernel(x_ref, nw_ref, w_ref, wab_ref, wabt_ref, p_ref, pab_ref, pabt_ref, u_scr):
    j = pl.program_id(1)

    @pl.when(j == 0)
    def _():
        x = x_ref[...]
        ms = jnp.mean(x * x, axis=-1, keepdims=True)
        u = (x * lax.rsqrt(ms + NORM_EPS) * nw_ref[...]).astype(BF16)
        u_scr[...] = u
        pab_ref[...] = _dot(u, wab_ref[...])
        pabt_ref[...] = _dot_nt(wabt_ref[...], u)

    p_ref[...] = _dot(u_scr[...], w_ref[...]).astype(p_ref.dtype)


def _inproj(x2d, norm_w, w_main, wab, wabt, *, tm, tn):
    n = x2d.shape[0]
    grid = (n // tm, P_MAIN_COLS // tn)
    return pl.pallas_call(
        _inproj_kernel,
        grid=grid,
        in_specs=[
            pl.BlockSpec((tm, D_MODEL), lambda i, j: (i, 0)),
            pl.BlockSpec((1, D_MODEL), lambda i, j: (0, 0)),
            pl.BlockSpec((D_MODEL, tn), lambda i, j: (0, j)),
            pl.BlockSpec((D_MODEL, LANES), lambda i, j: (0, 0)),
            pl.BlockSpec((LANES, D_MODEL), lambda i, j: (0, 0)),
        ],
        out_specs=[
            pl.BlockSpec((tm, tn), lambda i, j: (i, j)),
            pl.BlockSpec((tm, LANES), lambda i, j: (i, 0)),
            pl.BlockSpec((LANES, tm), lambda i, j: (0, i)),
        ],
        out_shape=[
            jax.ShapeDtypeStruct((n, P_MAIN_COLS), BF16),
            jax.ShapeDtypeStruct((n, LANES), F32),
            jax.ShapeDtypeStruct((LANES, n), F32),
        ],
        scratch_shapes=[pltpu.VMEM((tm, D_MODEL), BF16)],
        compiler_params=pltpu.CompilerParams(
            dimension_semantics=("arbitrary", "arbitrary"), vmem_limit_bytes=VMEM_LIMIT),
        name="inproj",
    )(x2d, norm_w, w_main, wab, wabt)


def _unit_lower_inverse(a):
    c = a.shape[0]
    ii = lax.broadcasted_iota(jnp.int32, (c, c), 0)
    jj = lax.broadcasted_iota(jnp.int32, (c, c), 1)
    p = jnp.where(ii == jj, 1.0, 0.0).astype(F32) - a
    apow = a
    for _ in range(int(math.log2(c)) - 1):
        ab = apow.astype(BF16)
        apow = _dot(ab, ab)
        p = p + _dot(p.astype(BF16), apow.astype(BF16))
    return p


def _chunk_head(q, k, v, bcol, gcol, grow, s):
    c = q.shape[0]
    ii = lax.broadcasted_iota(jnp.int32, (c, c), 0)
    jj = lax.broadcasted_iota(jnp.int32, (c, c), 1)
    dec = jnp.exp(jnp.where(ii >= jj, gcol - grow, -jnp.inf))
    kb = k.astype(BF16)
    qb = q.astype(BF16)
    kk = _dot_nt(kb, kb)
    qk = _dot_nt(qb, kb)
    a = jnp.where(ii > jj, bcol * kk * dec, 0.0)
    tinv = _unit_lower_inverse(a)
    egc = jnp.exp(gcol)
    rhs = jnp.concatenate([v * bcol, k * (bcol * egc)], axis=1)
    sol = _dot(tinv.astype(BF16), rhs.astype(BF16))
    u_w = sol[:, :DN_HEAD_DIM]
    w_w = sol[:, DN_HEAD_DIM:]
    glast = gcol[c - 1:c, :]
    kd = k * jnp.exp(glast - gcol)
    qd = q * egc
    sb = s.astype(BF16)
    wq = _dot(jnp.concatenate([w_w, qd], axis=0).astype(BF16), sb)
    v_new = u_w - wq[:c]
    vb = v_new.astype(BF16)
    o = wq[c:] + _dot((qk * dec).astype(BF16), vb)
    s_new = s * jnp.exp(glast) + _dot_tn(kd.astype(BF16), vb)
    return o, s_new


def _mixer_kernel(pad_rows, t_rows,
                  p_ref, pab_ref, pabt_ref, x_ref, s0_ref, ph0_ref, ch0_ref,
                  convw_ref, poolw_ref, pscale_ref, alog_ref, dtb_ref, alogt_ref, dtbt_ref,
                  dnw_ref, wout_ref, nffn_ref, wrt_ref, brt_ref,
                  h_ref, xt_ref, rf_ref, ri_ref, s_out_ref, ph_out_ref, ch_out_ref,
                  s_scr, pool_buf, conv_buf, qkv_scr, o_scr, beta_scr, gcol_scr, grow_scr):
    t = t_rows
    j = pl.program_id(1)
    n_chunks = t // CHUNK

    @pl.when(j == 0)
    def _():
        s_scr[...] = s0_ref[...]
        pool_buf[0:POOL_HALO, :] = ph0_ref[...]
        conv_buf[0:CONV_HALO, :] = ch0_ref[...]

    pool_buf[POOL_HALO:POOL_HALO + t, :] = p_ref[:, 0:D_MODEL].astype(F32)
    conv_buf[CONV_HALO:CONV_HALO + t, :] = p_ref[:, D_MODEL:4 * D_MODEL].astype(F32)

    p_cur = pool_buf[POOL_HALO:POOL_HALO + t, :]
    mixed = []
    for gi, win in enumerate(POOL_WINDOWS):
        cs = slice(gi * POOL_GROUP_DIM, (gi + 1) * POOL_GROUP_DIM)
        acc = p_cur[:, cs]
        for i in range(1, win):
            acc = acc + pool_buf[POOL_HALO - i:POOL_HALO - i + t, cs]
        pooled = acc * (1.0 / win) - p_cur[:, cs]
        mixed.append(_dot(pooled.astype(BF16), poolw_ref[gi]))
    y_pool = jnp.concatenate(mixed, axis=1) * pscale_ref[...]

    acc = convw_ref[CONV_WIDTH - 1:CONV_WIDTH, :] * conv_buf[CONV_HALO:CONV_HALO + t, :]
    for kk in range(CONV_WIDTH - 1):
        off = CONV_HALO - (CONV_WIDTH - 1) + kk
        acc = acc + convw_ref[kk:kk + 1, :] * conv_buf[off:off + t, :]
    qkv = acc * _sigmoid(acc)
    q_scale = DN_HEAD_DIM ** -0.5
    for hh in range(2 * DN_HEADS):
        ls = slice(hh * DN_HEAD_DIM, (hh + 1) * DN_HEAD_DIM)
        blk = qkv[:, ls]
        nrm = lax.rsqrt(jnp.sum(blk * blk, axis=-1, keepdims=True) + NORM_EPS)
        if hh < DN_HEADS:
            nrm = nrm * q_scale
        qkv_scr[:, ls] = blk * nrm
    qkv_scr[:, 2 * D_MODEL:3 * D_MODEL] = qkv[:, 2 * D_MODEL:3 * D_MODEL]

    pab = pab_ref[...]
    gcol_all = -jnp.exp(alog_ref[...]) * _softplus(pab + dtb_ref[...])
    beta_all = _sigmoid(pab)
    pabt = pabt_ref[...]
    grow_all = -jnp.exp(alogt_ref[...])[:, 0:1] * _softplus(pabt + dtbt_ref[...][:, 0:1])
    if pad_rows:
        rid = lax.broadcasted_iota(jnp.int32, (t, LANES), 0)
        gcol_all = jnp.where(rid >= pad_rows, gcol_all, 0.0)
        beta_all = jnp.where(rid >= pad_rows, beta_all, 0.0)
        cid = lax.broadcasted_iota(jnp.int32, (LANES, t), 1)
        grow_all = jnp.where(cid >= pad_rows, grow_all, 0.0)
    beta_scr[...] = beta_all
    ci = lax.broadcasted_iota(jnp.int32, (CHUNK, CHUNK), 0)
    cj = lax.broadcasted_iota(jnp.int32, (CHUNK, CHUNK), 1)
    tri_l = jnp.where(ci >= cj, 1.0, 0.0).astype(F32)
    tri_u = jnp.where(ci <= cj, 1.0, 0.0).astype(F32)
    for c in range(n_chunks):
        rs = slice(c * CHUNK, (c + 1) * CHUNK)
        gcol_scr[rs, :] = jnp.dot(tri_l, gcol_all[rs, :], preferred_element_type=F32,
                                  precision=lax.Precision.HIGHEST)
        grow_scr[c] = jnp.dot(grow_all[:, rs], tri_u, preferred_element_type=F32,
                              precision=lax.Precision.HIGHEST)

    def chunk_step(c, carry):
        r0 = pl.multiple_of(c * CHUNK, CHUNK)
        gcols = gcol_scr[pl.ds(r0, CHUNK), :]
        betas = beta_scr[pl.ds(r0, CHUNK), :]
        grows = grow_scr[c]
        for hh in range(DN_HEADS):
            ls = slice(hh * DN_HEAD_DIM, (hh + 1) * DN_HEAD_DIM)
            q = qkv_scr[pl.ds(r0, CHUNK), hh * DN_HEAD_DIM:(hh + 1) * DN_HEAD_DIM]
            k = qkv_scr[pl.ds(r0, CHUNK), D_MODEL + hh * DN_HEAD_DIM:D_MODEL + (hh + 1) * DN_HEAD_DIM]
            v = qkv_scr[pl.ds(r0, CHUNK), 2 * D_MODEL + hh * DN_HEAD_DIM:2 * D_MODEL + (hh + 1) * DN_HEAD_DIM]
            o, s_new = _chunk_head(q, k, v,
                                   betas[:, DN_HEADS + hh:DN_HEADS + hh + 1],
                                   gcols[:, hh:hh + 1], grows[hh:hh + 1, :], s_scr[hh])
            s_scr[hh] = s_new
            o_scr[pl.ds(r0, CHUNK), ls] = o
        return carry

    lax.fori_loop(0, n_chunks, chunk_step, 0)

    z = p_ref[:, 4 * D_MODEL:5 * D_MODEL].astype(F32)
    y_dn_parts = []
    for hh in range(DN_HEADS):
        ls = slice(hh * DN_HEAD_DIM, (hh + 1) * DN_HEAD_DIM)
        o = o_scr[:, ls]
        o = o * lax.rsqrt(jnp.mean(o * o, axis=-1, keepdims=True) + NORM_EPS) * dnw_ref[...]
        zz = z[:, ls]
        y_dn_parts.append(o * (zz * _sigmoid(zz)))
    y_dn = jnp.concatenate(y_dn_parts, axis=1)

    g_pool = p_ref[:, 5 * D_MODEL:6 * D_MODEL].astype(F32)
    g_dn = p_ref[:, 6 * D_MODEL:7 * D_MODEL].astype(F32)
    merged = _sigmoid(g_pool) * y_pool + _sigmoid(g_dn) * y_dn
    h = x_ref[...] + _dot(merged.astype(BF16), wout_ref[...])
    h_ref[...] = h

    xt = h * lax.rsqrt(jnp.mean(h * h, axis=-1, keepdims=True) + NORM_EPS) * nffn_ref[...]
    xt_ref[...] = xt.astype(xt_ref.dtype)
    logits = lax.dot_general(wrt_ref[...], xt, (((1,), (1,)), ((), ())),
                             preferred_element_type=F32, precision=lax.Precision.HIGHEST)
    logits = logits + brt_ref[...][:, 0:1]
    rid8 = lax.broadcasted_iota(jnp.int32, (SUBLANES, t), 0)
    lg = jnp.where(rid8 < N_EXPERT_GROUPS, logits[0:SUBLANES, :], -jnp.inf)
    gmax = jnp.max(lg, axis=0, keepdims=True)
    g_idx = jnp.min(jnp.where(lg == gmax, rid8, SUBLANES), axis=0, keepdims=True)
    p_grp = 1.0 / jnp.sum(jnp.exp(lg - gmax), axis=0, keepdims=True)
    sel = jnp.zeros((EXPERTS_PER_GROUP, t), F32)
    for gi in range(N_EXPERT_GROUPS):
        r0 = SUBLANES + gi * EXPERTS_PER_GROUP
        sel = jnp.where(g_idx == gi, logits[r0:r0 + EXPERTS_PER_GROUP, :], sel)
    m1 = jnp.max(sel, axis=0, keepdims=True)
    i1 = jnp.min(jnp.where(sel == m1, rid8, SUBLANES), axis=0, keepdims=True)
    sel2 = jnp.where(rid8 == i1, -jnp.inf, sel)
    m2 = jnp.max(sel2, axis=0, keepdims=True)
    i2 = jnp.min(jnp.where(sel2 == m2, rid8, SUBLANES), axis=0, keepdims=True)
    e21 = jnp.exp(m2 - m1)
    w1 = 1.0 / (1.0 + e21)
    c1 = p_grp * w1
    c2 = p_grp * (e21 * w1)
    id1 = g_idx * EXPERTS_PER_GROUP + i1
    id2 = g_idx * EXPERTS_PER_GROUP + i2
    rf_ref[...] = jnp.where(rid8 == 0, c1, jnp.where(rid8 == 1, c2, 0.0))
    ri_ref[...] = jnp.where(rid8 == 0, id1, jnp.where(rid8 == 1, id2, 0))

    pool_buf[0:POOL_HALO, :] = pool_buf[t:t + POOL_HALO, :]
    conv_buf[0:CONV_HALO, :] = conv_buf[t:t + CONV_HALO, :]
    s_out_ref[...] = s_scr[...]
    ph_out_ref[...] = pool_buf[0:POOL_HALO, :]
    ch_out_ref[...] = conv_buf[0:CONV_HALO, :]


def _softplus(x):
    return jnp.maximum(x, 0.0) + jnp.log1p(jnp.exp(-jnp.abs(x)))


def _mixer(p_main, pab, pabt, x2d, s0, ph0, ch0, params, *, batch, t_rows, pad_rows):
    n = x2d.shape[0]
    n_t = n // batch // t_rows
    t = t_rows
    row_blk = lambda b, j: (b * n_t + j, 0)
    col_blk = lambda b, j: (0, b * n_t + j)
    const2 = lambda b, j: (0, 0)
    const3 = lambda b, j: (0, 0, 0)
    (convw, poolw, pscale, alog, dtb, alogt, dtbt, dnw, wout, nffn, wrt, brt) = params
    in_specs = [
        pl.BlockSpec((t, P_MAIN_COLS), row_blk),
        pl.BlockSpec((t, LANES), row_blk),
        pl.BlockSpec((LANES, t), col_blk),
        pl.BlockSpec((t, D_MODEL), row_blk),
        pl.BlockSpec((DN_HEADS, DN_HEAD_DIM, DN_HEAD_DIM), const3),
        pl.BlockSpec((POOL_HALO, D_MODEL), const2),
        pl.BlockSpec((CONV_HALO, 3 * D_MODEL), const2),
        pl.BlockSpec(convw.shape, const2),
        pl.BlockSpec(poolw.shape, const3),
        pl.BlockSpec(pscale.shape, const2),
        pl.BlockSpec(alog.shape, const2),
        pl.BlockSpec(dtb.shape, const2),
        pl.BlockSpec(alogt.shape, const2),
        pl.BlockSpec(dtbt.shape, const2),
        pl.BlockSpec(dnw.shape, const2),
        pl.BlockSpec(wout.shape, const2),
        pl.BlockSpec(nffn.shape, const2),
        pl.BlockSpec(wrt.shape, const2),
        pl.BlockSpec(brt.shape, const2),
    ]
    out_specs = [
        pl.BlockSpec((t, D_MODEL), row_blk),
        pl.BlockSpec((t, D_MODEL), row_blk),
        pl.BlockSpec((SUBLANES, t), col_blk),
        pl.BlockSpec((SUBLANES, t), col_blk),
        pl.BlockSpec((DN_HEADS, DN_HEAD_DIM, DN_HEAD_DIM), const3),
        pl.BlockSpec((POOL_HALO, D_MODEL), const2),
        pl.BlockSpec((CONV_HALO, 3 * D_MODEL), const2),
    ]
    out_shape = [
        jax.ShapeDtypeStruct((n, D_MODEL), F32),
        jax.ShapeDtypeStruct((n, D_MODEL), BF16),
        jax.ShapeDtypeStruct((SUBLANES, n), F32),
        jax.ShapeDtypeStruct((SUBLANES, n), jnp.int32),
        jax.ShapeDtypeStruct((DN_HEADS, DN_HEAD_DIM, DN_HEAD_DIM), F32),
        jax.ShapeDtypeStruct((POOL_HALO, D_MODEL), F32),
        jax.ShapeDtypeStruct((CONV_HALO, 3 * D_MODEL), F32),
    ]
    scratch = [
        pltpu.VMEM((DN_HEADS, DN_HEAD_DIM, DN_HEAD_DIM), F32),
        pltpu.VMEM((t + POOL_HALO, D_MODEL), F32),
        pltpu.VMEM((t + CONV_HALO, 3 * D_MODEL), F32),
        pltpu.VMEM((t, 3 * D_MODEL), F32),
        pltpu.VMEM((t, D_MODEL), F32),
        pltpu.VMEM((t, LANES), F32),
        pltpu.VMEM((t, LANES), F32),
        pltpu.VMEM((t // CHUNK, LANES, CHUNK), F32),
    ]
    return pl.pallas_call(
        functools.partial(_mixer_kernel, pad_rows, t_rows),
        grid=(batch, n_t),
        in_specs=in_specs,
        out_specs=out_specs,
        out_shape=out_shape,
        scratch_shapes=scratch,
        compiler_params=pltpu.CompilerParams(
            dimension_semantics=("arbitrary", "arbitrary"), vmem_limit_bytes=VMEM_LIMIT),
        name="mixer_meta" if pad_rows else "mixer",
    )(p_main, pab, pabt, x2d, s0, ph0, ch0, *params)


def _moe_kernel(xt_ref, h_ref, rf_ref, ri_ref, wg_ref, wu_ref, wd_ref, nfin_ref, out_ref, acc_scr):
    e = pl.program_id(1)
    tm = xt_ref.shape[0]

    @pl.when(e == 0)
    def _():
        acc_scr[...] = jnp.zeros_like(acc_scr)

    rf = rf_ref[...]
    ri = ri_ref[...]
    comb_row = (jnp.where(ri[0:1, :] == e, rf[0:1, :], 0.0)
                + jnp.where(ri[1:2, :] == e, rf[1:2, :], 0.0))
    comb_col = jnp.transpose(jnp.broadcast_to(comb_row, (LANES, tm)))[:, 0:1]
    x = xt_ref[...]
    gate = _dot(x, wg_ref[0].astype(BF16))
    up = _dot(x, wu_ref[0].astype(BF16))
    hdn = gate * _sigmoid(gate) * up * comb_col
    acc_scr[...] += _dot(hdn.astype(BF16), wd_ref[0].astype(BF16))

    @pl.when(e == N_EXPERTS - 1)
    def _():
        hh = h_ref[...] + acc_scr[...]
        out_ref[...] = hh * lax.rsqrt(jnp.mean(hh * hh, axis=-1, keepdims=True) + NORM_EPS) * nfin_ref[...]


def _moe(xt, h, rf, ri, wg, wu, wd, nfin, *, tm):
    n = xt.shape[0]
    return pl.pallas_call(
        _moe_kernel,
        grid=(n // tm, N_EXPERTS),
        in_specs=[
            pl.BlockSpec((tm, D_MODEL), lambda i, e: (i, 0)),
            pl.BlockSpec((tm, D_MODEL), lambda i, e: (i, 0)),
            pl.BlockSpec((SUBLANES, tm), lambda i, e: (0, i)),
            pl.BlockSpec((SUBLANES, tm), lambda i, e: (0, i)),
            pl.BlockSpec((1, D_MODEL, D_FF_EXPERT), lambda i, e: (e, 0, 0)),
            pl.BlockSpec((1, D_MODEL, D_FF_EXPERT), lambda i, e: (e, 0, 0)),
            pl.BlockSpec((1, D_FF_EXPERT, D_MODEL), lambda i, e: (e, 0, 0)),
            pl.BlockSpec((1, D_MODEL), lambda i, e: (0, 0)),
        ],
        out_specs=pl.BlockSpec((tm, D_MODEL), lambda i, e: (i, 0)),
        out_shape=jax.ShapeDtypeStruct((n, D_MODEL), F32),
        scratch_shapes=[pltpu.VMEM((tm, D_MODEL), F32)],
        compiler_params=pltpu.CompilerParams(
            dimension_semantics=("arbitrary", "arbitrary"), vmem_limit_bytes=VMEM_LIMIT),
        name="moe",
    )(xt, h, rf, ri, wg, wu, wd, nfin)


def _pad_lanes_row(v):
    return jnp.pad(v.astype(F32), (0, LANES - v.shape[0]))[None, :]


def _block_forward(x, meta_tokens, norm_mix_w, w_in, conv_w, pool_w, pool_scale, a_log, dt_bias,
                   dn_norm_w, w_out, norm_ffn_w, router_group_w, router_group_b, router_expert_w,
                   router_expert_b, expert_w_gate, expert_w_up, expert_w_down, norm_final_w,
                   *, mixer_rows, inproj_rows, inproj_cols, moe_rows):
    bsz, seq, _ = x.shape
    n = bsz * seq
    x2d = x.reshape(n, D_MODEL)

    ab0 = 5 * D_MODEL
    w_main = jnp.concatenate([w_in[:, :ab0], w_in[:, ab0 + 2 * DN_HEADS:]], axis=1).astype(BF16)
    wab = jnp.pad(w_in[:, ab0:ab0 + 2 * DN_HEADS], ((0, 0), (0, LANES - 2 * DN_HEADS))).astype(BF16)
    wabt = wab.T
    nmix = norm_mix_w[None, :]
    alog = _pad_lanes_row(a_log)
    dtb = _pad_lanes_row(dt_bias)
    alogt = jnp.broadcast_to(alog.T, (LANES, LANES))
    dtbt = jnp.broadcast_to(dtb.T, (LANES, LANES))
    wr = jnp.zeros((D_MODEL, LANES), F32)
    wr = wr.at[:, 0:N_EXPERT_GROUPS].set(router_group_w)
    wr = wr.at[:, SUBLANES:SUBLANES + N_EXPERTS].set(router_expert_w)
    br = jnp.zeros((LANES,), F32)
    br = br.at[0:N_EXPERT_GROUPS].set(router_group_b)
    br = br.at[SUBLANES:SUBLANES + N_EXPERTS].set(router_expert_b)
    params = (conv_w, pool_w.astype(BF16), pool_scale[None, :], alog, dtb, alogt, dtbt,
              dn_norm_w[None, :], w_out.astype(BF16), norm_ffn_w[None, :],
              wr.T, jnp.broadcast_to(br[:, None], (LANES, LANES)))

    pad_rows = CHUNK - N_META
    xm = jnp.concatenate([jnp.zeros((pad_rows, D_MODEL), F32), meta_tokens], axis=0)
    pm, pabm, pabtm = _inproj(xm, nmix, w_main, wab, wabt, tm=CHUNK, tn=inproj_cols)
    zeros_s = jnp.zeros((DN_HEADS, DN_HEAD_DIM, DN_HEAD_DIM), F32)
    zeros_ph = jnp.zeros((POOL_HALO, D_MODEL), F32)
    zeros_ch = jnp.zeros((CONV_HALO, 3 * D_MODEL), F32)
    meta_out = _mixer(pm, pabm, pabtm, xm, zeros_s, zeros_ph, zeros_ch, params,
                      batch=1, t_rows=CHUNK, pad_rows=pad_rows)
    s_meta, ph_meta, ch_meta = meta_out[4], meta_out[5], meta_out[6]

    p_main, pab, pabt = _inproj(x2d, nmix, w_main, wab, wabt, tm=inproj_rows, tn=inproj_cols)
    h, xt, rf, ri, _, _, _ = _mixer(p_main, pab, pabt, x2d, s_meta, ph_meta, ch_meta, params,
                                    batch=bsz, t_rows=mixer_rows, pad_rows=0)
    wg = expert_w_gate.reshape(N_EXPERTS, D_MODEL, D_FF_EXPERT)
    wu = expert_w_up.reshape(N_EXPERTS, D_MODEL, D_FF_EXPERT)
    wd = expert_w_down.reshape(N_EXPERTS, D_FF_EXPERT, D_MODEL)
    out = _moe(xt, h, rf, ri, wg, wu, wd, norm_final_w[None, :], tm=moe_rows)
    return out.reshape(bsz, seq, D_MODEL)


def kernel(x, meta_tokens, norm_mix_w, w_in, conv_w, pool_w, pool_scale, a_log, dt_bias, dn_norm_w, w_out, norm_ffn_w, router_group_w, router_group_b, router_expert_w, router_expert_b, expert_w_gate, expert_w_up, expert_w_down, norm_final_w):
    assert norm_mix_w.shape[0] == 1, "single-layer block"
    seq = x.shape[1]
    return _block_forward(
        x, meta_tokens, norm_mix_w[0], w_in[0], conv_w[0], pool_w[0], pool_scale[0], a_log[0],
        dt_bias[0], dn_norm_w[0], w_out[0], norm_ffn_w[0], router_group_w[0], router_group_b[0],
        router_expert_w[0], router_expert_b[0], expert_w_gate[0], expert_w_up[0], expert_w_down[0],
        norm_final_w,
        mixer_rows=min(256, seq), inproj_rows=min(1024, x.shape[0] * seq), inproj_cols=1024,
        moe_rows=min(1024, x.shape[0] * seq))
```

```python
import functools
import math

import jax
import jax.numpy as jnp
from jax import lax
from jax.experimental import pallas as pl
from jax.experimental.pallas import tpu as pltpu

F32 = jnp.float32
BF16 = jnp.bfloat16

D_MODEL = 1024
N_META = 16
POOL_WINDOWS = (2, 4, 8, 16)
POOL_GROUP_DIM = 256
DN_HEADS = 8
DN_HEAD_DIM = 128
CONV_WIDTH = 4
CHUNK = 64
N_EXPERT_GROUPS = 4
EXPERTS_PER_GROUP = 8
N_EXPERTS = 32
D_FF_EXPERT = 512
NORM_EPS = 1e-6

LANES = 128
SUBLANES = 8
P_MAIN_COLS = 7 * D_MODEL
POOL_HALO = 16
CONV_HALO = 8
VMEM_LIMIT = 56 * 1024 * 1024
MOE_TILE = 256
ROUTE_BLOCK = 512
ROW_BLOCK = 256


def _dot(a, b):
    return jnp.dot(a, b, preferred_element_type=F32)


def _dot_nt(a, b):
    return lax.dot_general(a, b, (((1,), (1,)), ((), ())), preferred_element_type=F32)


def _dot_tn(a, b):
    return lax.dot_general(a, b, (((0,), (0,)), ((), ())), preferred_element_type=F32)


def _sigmoid(x):
    return 1.0 / (1.0 + jnp.exp(-x))


def _inproj_kernel(x_ref, nw_ref, w_ref, wab_ref, wabt_ref, p_ref, pab_ref, pabt_ref, u_scr):
    j = pl.program_id(1)

    @pl.when(j == 0)
    def _():
        x = x_ref[...]
        ms = jnp.mean(x * x, axis=-1, keepdims=True)
        u = (x * lax.rsqrt(ms + NORM_EPS) * nw_ref[...]).astype(BF16)
        u_scr[...] = u
        pab_ref[...] = _dot(u, wab_ref[...])
        pabt_ref[...] = _dot_nt(wabt_ref[...], u)

    p_ref[...] = _dot(u_scr[...], w_ref[...]).astype(p_ref.dtype)


def _inproj(x2d, norm_w, w_main, wab, wabt, *, tm, tn):
    n = x2d.shape[0]
    grid = (n // tm, P_MAIN_COLS // tn)
    return pl.pallas_call(
        _inproj_kernel,
        grid=grid,
        in_specs=[
            pl.BlockSpec((tm, D_MODEL), lambda i, j: (i, 0)),
            pl.BlockSpec((1, D_MODEL), lambda i, j: (0, 0)),
            pl.BlockSpec((D_MODEL, tn), lambda i, j: (0, j)),
            pl.BlockSpec((D_MODEL, LANES), lambda i, j: (0, 0)),
            pl.BlockSpec((LANES, D_MODEL), lambda i, j: (0, 0)),
        ],
        out_specs=[
            pl.BlockSpec((tm, tn), lambda i, j: (i, j)),
            pl.BlockSpec((tm, LANES), lambda i, j: (i, 0)),
            pl.BlockSpec((LANES, tm), lambda i, j: (0, i)),
        ],
        out_shape=[
            jax.ShapeDtypeStruct((n, P_MAIN_COLS), BF16),
            jax.ShapeDtypeStruct((n, LANES), F32),
            jax.ShapeDtypeStruct((LANES, n), F32),
        ],
        scratch_shapes=[pltpu.VMEM((tm, D_MODEL), BF16)],
        compiler_params=pltpu.CompilerParams(
            dimension_semantics=("arbitrary", "arbitrary"), vmem_limit_bytes=VMEM_LIMIT),
        name="inproj",
    )(x2d, norm_w, w_main, wab, wabt)


def _intra_chunk(qs, ks, vs, bcols, gcols, grows):
    c = qs[0].shape[0]
    hs = range(len(qs))
    ii = lax.broadcasted_iota(jnp.int32, (c, c), 0)
    jj = lax.broadcasted_iota(jnp.int32, (c, c), 1)
    dec =[jnp.exp(jnp.where(ii >= jj, gcols[h] - grows[h], -jnp.inf)) for h in hs]
    kb = [ks[h].astype(BF16) for h in hs]
    qkb = [jnp.concatenate([qs[h].astype(BF16), kb[h]], axis=0) for h in hs]
    qkk = [_dot_nt(qkb[h], kb[h]) for h in hs]
    egc = [jnp.exp(gcols[h]) for h in hs]
    pw = [jnp.where(ii > jj, -(bcols[h] * qkk[h][c:] * dec[h]), 0.0) for h in hs]
    sol = [jnp.concatenate([vs[h] * bcols[h], ks[h] * (bcols[h] * egc[h])], axis=1) for h in hs]
    width = 2 * DN_HEAD_DIM
    levels = int(math.log2(c))
    for lvl in range(levels):
        pb = [pw[h].astype(BF16) for h in hs]
        if lvl < levels - 1:
            r = [_dot(pb[h], jnp.concatenate([sol[h].astype(BF16), pb[h]], axis=1)) for h in hs]
            sol = [sol[h] + r[h][:, :width] for h in hs]
            pw = [r[h][:, width:] for h in hs]
        else:
            sol = [sol[h] + _dot(pb[h], sol[h].astype(BF16)) for h in hs]
    qd = [qs[h] * egc[h] for h in hs]
    glast = [gcols[h][c - 1:c, :] for h in hs]
    kd = [(ks[h] * jnp.exp(glast[h] - gcols[h])).astype(BF16) for h in hs]
    u = [sol[h][:, :DN_HEAD_DIM] for h in hs]
    wq = [jnp.concatenate([sol[h][:, DN_HEAD_DIM:], qd[h]], axis=0).astype(BF16) for h in hs]
    qkd = [(qkk[h][:c] * dec[h]).astype(BF16) for h in hs]
    cd = [jnp.broadcast_to(jnp.exp(glast[h]), (1, DN_HEAD_DIM)) for h in hs]
    return u, wq, qkd, kd, cd


def _scan_chunk(u, wq, qkd, kd, cd, s):
    c = u[0].shape[0]
    hs = range(len(u))
    sb = [s[h].astype(BF16) for h in hs]
    ws = [_dot(wq[h], sb[h]) for h in hs]
    vb = [(u[h] - ws[h][:c]).astype(BF16) for h in hs]
    o = [ws[h][c:] + _dot(qkd[h], vb[h]) for h in hs]
    s_new = [s[h] * cd[h] + _dot_tn(kd[h], vb[h]) for h in hs]
    return o, s_new


def _mixer_kernel(pad_rows, t_rows,
                  p_ref, pab_ref, pabt_ref, x_ref, s0_ref, ph0_ref, ch0_ref,
                  convw_ref, poolw_ref, pscale_ref, alog_ref, dtb_ref, alogt_ref, dtbt_ref,
                  dnw_ref, wout_ref, nffn_ref, wrt_ref, brt_ref,
                  h_ref, rf_ref, ri_ref, s_out_ref, ph_out_ref, ch_out_ref,
                  s_scr, pool_buf, conv_buf, qkv_scr, o_scr, beta_scr, gcol_scr, grow_scr,
                  u_scr, wq_scr, qkd_scr, kd_scr, cd_scr):
    t = t_rows
    j = pl.program_id(1)
    n_chunks = t // CHUNK

    @pl.when(j == 0)
    def _():
        s_scr[...] = s0_ref[...]
        pool_buf[0:POOL_HALO, :] = ph0_ref[...]
        conv_buf[0:CONV_HALO, :] = ch0_ref[...]

    pool_buf[POOL_HALO:POOL_HALO + t, :] = p_ref[:, 0:D_MODEL].astype(F32)
    conv_buf[CONV_HALO:CONV_HALO + t, :] = p_ref[:, D_MODEL:4 * D_MODEL].astype(F32)

    p_cur = pool_buf[POOL_HALO:POOL_HALO + t, :]
    mixed = []
    for gi, win in enumerate(POOL_WINDOWS):
        cs = slice(gi * POOL_GROUP_DIM, (gi + 1) * POOL_GROUP_DIM)
        acc = pool_buf[:, cs]
        shift = 1
        while shift < win:
            acc = acc + pltpu.roll(acc, shift, axis=0)
            shift *= 2
        pooled = acc[POOL_HALO:, :] * (1.0 / win) - p_cur[:, cs]
        mixed.append(_dot(pooled.astype(BF16), poolw_ref[gi]))
    y_pool = jnp.concatenate(mixed, axis=1) * pscale_ref[...]

    acc = convw_ref[CONV_WIDTH - 1:CONV_WIDTH, :] * conv_buf[CONV_HALO:CONV_HALO + t, :]
    for kk in range(CONV_WIDTH - 1):
        off = CONV_HALO - (CONV_WIDTH - 1) + kk
        acc = acc + convw_ref[kk:kk + 1, :] * conv_buf[off:off + t, :]
    qkv = acc * _sigmoid(acc)
    q_scale = DN_HEAD_DIM ** -0.5
    for hh in range(2 * DN_HEADS):
        ls = slice(hh * DN_HEAD_DIM, (hh + 1) * DN_HEAD_DIM)
        blk = qkv[:, ls]
        nrm = lax.rsqrt(jnp.sum(blk * blk, axis=-1, keepdims=True) + NORM_EPS)
        if hh < DN_HEADS:
            nrm = nrm * q_scale
        qkv_scr[:, ls] = blk * nrm
    qkv_scr[:, 2 * D_MODEL:3 * D_MODEL] = qkv[:, 2 * D_MODEL:3 * D_MODEL]

    pab = pab_ref[...]
    gcol_all = -jnp.exp(alog_ref[...]) * _softplus(pab + dtb_ref[...])
    beta_all = _sigmoid(pab)
    pabt = pabt_ref[...]
    grow_all = -jnp.exp(alogt_ref[...])[:, 0:1] * _softplus(pabt + dtbt_ref[...][:, 0:1])
    if pad_rows:
        rid = lax.broadcasted_iota(jnp.int32, (t, LANES), 0)
        gcol_all = jnp.where(rid >= pad_rows, gcol_all, 0.0)
        beta_all = jnp.where(rid >= pad_rows, beta_all, 0.0)
        cid = lax.broadcasted_iota(jnp.int32, (LANES, t), 1)
        grow_all = jnp.where(cid >= pad_rows, grow_all, 0.0)
    beta_scr[...] = beta_all
    ci = lax.broadcasted_iota(jnp.int32, (CHUNK, CHUNK), 0)
    cj = lax.broadcasted_iota(jnp.int32, (CHUNK, CHUNK), 1)
    tri_l = jnp.where(ci >= cj, 1.0, 0.0).astype(F32)
    tri_u = jnp.where(ci <= cj, 1.0, 0.0).astype(F32)
    for c in range(n_chunks):
        rs = slice(c * CHUNK, (c + 1) * CHUNK)
        gcol_scr[rs, :] = jnp.dot(tri_l, gcol_all[rs, :], preferred_element_type=F32,
                                  precision=lax.Precision.HIGHEST)
        grow_scr[c] = jnp.dot(grow_all[:, rs], tri_u, preferred_element_type=F32,
                              precision=lax.Precision.HIGHEST)

    heads = range(DN_HEADS)

    def head_cols(base, hh):
        return slice(base + hh * DN_HEAD_DIM, base + (hh + 1) * DN_HEAD_DIM)

    per_step = 2 if n_chunks % 2 == 0 else 1

    def intra_step(step, carry):
        qs, ks, vs, bcols, gcols, grows = [], [], [], [], [], []
        for ci in range(per_step):
            c = step * per_step + ci
            r0 = pl.multiple_of(c * CHUNK, CHUNK)
            gcol_c = gcol_scr[pl.ds(r0, CHUNK), :]
            beta_c = beta_scr[pl.ds(r0, CHUNK), :]
            grow_c = grow_scr[c]
            for hh in heads:
                qs.append(qkv_scr[pl.ds(r0, CHUNK), head_cols(0, hh)])
                ks.append(qkv_scr[pl.ds(r0, CHUNK), head_cols(D_MODEL, hh)])
                vs.append(qkv_scr[pl.ds(r0, CHUNK), head_cols(2 * D_MODEL, hh)])
                bcols.append(beta_c[:, DN_HEADS + hh:DN_HEADS + hh + 1])
                gcols.append(gcol_c[:, hh:hh + 1])
                grows.append(grow_c[hh:hh + 1, :])
        u, wq, qkd, kd, cd = _intra_chunk(qs, ks, vs, bcols, gcols, grows)
        for ci in range(per_step):
            c = step * per_step + ci
            for hh in heads:
                idx = ci * DN_HEADS + hh
                u_scr[c, hh] = u[idx]
                wq_scr[c, hh] = wq[idx]
                qkd_scr[c, hh] = qkd[idx]
                kd_scr[c, hh] = kd[idx]
            cd_scr[c] = jnp.concatenate(cd[ci * DN_HEADS:(ci + 1) * DN_HEADS], axis=0)
        return carry

    lax.fori_loop(0, n_chunks // per_step, intra_step, 0)

    def scan_step(c, carry):
        r0 = pl.multiple_of(c * CHUNK, CHUNK)
        cds = cd_scr[c]
        o, s_new = _scan_chunk(
            [u_scr[c, hh] for hh in heads], [wq_scr[c, hh] for hh in heads],
            [qkd_scr[c, hh] for hh in heads], [kd_scr[c, hh] for hh in heads],
            [cds[hh:hh + 1, :] for hh in heads], [s_scr[hh] for hh in heads])
        for hh in heads:
            s_scr[hh] = s_new[hh]
            o_scr[pl.ds(r0, CHUNK), head_cols(0, hh)] = o[hh]
        return carry

    lax.fori_loop(0, n_chunks, scan_step, 0)

    z = p_ref[:, 4 * D_MODEL:5 * D_MODEL].astype(F32)
    y_dn_parts = []
    for hh in range(DN_HEADS):
        ls = slice(hh * DN_HEAD_DIM, (hh + 1) * DN_HEAD_DIM)
        o = o_scr[:, ls]
        o = o * lax.rsqrt(jnp.mean(o * o, axis=-1, keepdims=True) + NORM_EPS) * dnw_ref[...]
        zz = z[:, ls]
        y_dn_parts.append(o * (zz * _sigmoid(zz)))
    y_dn = jnp.concatenate(y_dn_parts, axis=1)

    g_pool = p_ref[:, 5 * D_MODEL:6 * D_MODEL].astype(F32)
    g_dn = p_ref[:, 6 * D_MODEL:7 * D_MODEL].astype(F32)
    merged = _sigmoid(g_pool) * y_pool + _sigmoid(g_dn) * y_dn
    h = x_ref[...] + _dot(merged.astype(BF16), wout_ref[...])
    h_ref[...] = h

    xt = h * lax.rsqrt(jnp.mean(h * h, axis=-1, keepdims=True) + NORM_EPS) * nffn_ref[...]
    logits = lax.dot_general(wrt_ref[...], xt, (((1,), (1,)), ((), ())),
                             preferred_element_type=F32, precision=lax.Precision.HIGHEST)
    logits = logits + brt_ref[...][:, 0:1]
    rid8 = lax.broadcasted_iota(jnp.int32, (SUBLANES, t), 0)
    lg = jnp.where(rid8 < N_EXPERT_GROUPS, logits[0:SUBLANES, :], -jnp.inf)
    gmax = jnp.max(lg, axis=0, keepdims=True)
    g_idx = jnp.min(jnp.where(lg == gmax, rid8, SUBLANES), axis=0, keepdims=True)
    p_grp = 1.0 / jnp.sum(jnp.exp(lg - gmax), axis=0, keepdims=True)
    sel = jnp.zeros((EXPERTS_PER_GROUP, t), F32)
    for gi in range(N_EXPERT_GROUPS):
        r0 = SUBLANES + gi * EXPERTS_PER_GROUP
        sel = jnp.where(g_idx == gi, logits[r0:r0 + EXPERTS_PER_GROUP, :], sel)
    m1 = jnp.max(sel, axis=0, keepdims=True)
    i1 = jnp.min(jnp.where(sel == m1, rid8, SUBLANES), axis=0, keepdims=True)
    sel2 = jnp.where(rid8 == i1, -jnp.inf, sel)
    m2 = jnp.max(sel2, axis=0, keepdims=True)
    i2 = jnp.min(jnp.where(sel2 == m2, rid8, SUBLANES), axis=0, keepdims=True)
    e21 = jnp.exp(m2 - m1)
    w1 = 1.0 / (1.0 + e21)
    c1 = p_grp * w1
    c2 = p_grp * (e21 * w1)
    id1 = g_idx * EXPERTS_PER_GROUP + i1
    id2 = g_idx * EXPERTS_PER_GROUP + i2
    rf_ref[...] = jnp.where(rid8 == 0, c1, jnp.where(rid8 == 1, c2, 0.0))
    ri_ref[...] = jnp.where(rid8 == 0, id1, jnp.where(rid8 == 1, id2, 0))

    pool_buf[0:POOL_HALO, :] = pool_buf[t:t + POOL_HALO, :]
    conv_buf[0:CONV_HALO, :] = conv_buf[t:t + CONV_HALO, :]
    s_out_ref[...] = s_scr[...]
    ph_out_ref[...] = pool_buf[0:POOL_HALO, :]
    ch_out_ref[...] = conv_buf[0:CONV_HALO, :]


def _softplus(x):
    return jnp.maximum(x, 0.0) + jnp.log1p(jnp.exp(-jnp.abs(x)))


def _mixer(p_main, pab, pabt, x2d, s0, ph0, ch0, params, *, batch, t_rows, pad_rows):
    n = x2d.shape[0]
    n_t = n // batch // t_rows
    t = t_rows
    row_blk = lambda b, j: (b * n_t + j, 0)
    col_blk = lambda b, j: (0, b * n_t + j)
    const2 = lambda b, j: (0, 0)
    const3 = lambda b, j: (0, 0, 0)
    (convw, poolw, pscale, alog, dtb, alogt, dtbt, dnw, wout, nffn, wrt, brt) = params
    in_specs = [
        pl.BlockSpec((t, P_MAIN_COLS), row_blk),
        pl.BlockSpec((t, LANES), row_blk),
        pl.BlockSpec((LANES, t), col_blk),
        pl.BlockSpec((t, D_MODEL), row_blk),
        pl.BlockSpec((DN_HEADS, DN_HEAD_DIM, DN_HEAD_DIM), const3),
        pl.BlockSpec((POOL_HALO, D_MODEL), const2),
        pl.BlockSpec((CONV_HALO, 3 * D_MODEL), const2),
        pl.BlockSpec(convw.shape, const2),
        pl.BlockSpec(poolw.shape, const3),
        pl.BlockSpec(pscale.shape, const2),
        pl.BlockSpec(alog.shape, const2),
        pl.BlockSpec(dtb.shape, const2),
        pl.BlockSpec(alogt.shape, const2),
        pl.BlockSpec(dtbt.shape, const2),
        pl.BlockSpec(dnw.shape, const2),
        pl.BlockSpec(wout.shape, const2),
        pl.BlockSpec(nffn.shape, const2),
        pl.BlockSpec(wrt.shape, const2),
        pl.BlockSpec(brt.shape, const2),
    ]
    out_specs = [
        pl.BlockSpec((t, D_MODEL), row_blk),
        pl.BlockSpec((SUBLANES, t), col_blk),
        pl.BlockSpec((SUBLANES, t), col_blk),
        pl.BlockSpec((DN_HEADS, DN_HEAD_DIM, DN_HEAD_DIM), const3),
        pl.BlockSpec((POOL_HALO, D_MODEL), const2),
        pl.BlockSpec((CONV_HALO, 3 * D_MODEL), const2),
    ]
    out_shape = [
        jax.ShapeDtypeStruct((n, D_MODEL), F32),
        jax.ShapeDtypeStruct((SUBLANES, n), F32),
        jax.ShapeDtypeStruct((SUBLANES, n), jnp.int32),
        jax.ShapeDtypeStruct((DN_HEADS, DN_HEAD_DIM, DN_HEAD_DIM), F32),
        jax.ShapeDtypeStruct((POOL_HALO, D_MODEL), F32),
        jax.ShapeDtypeStruct((CONV_HALO, 3 * D_MODEL), F32),
    ]
    scratch = [
        pltpu.VMEM((DN_HEADS, DN_HEAD_DIM, DN_HEAD_DIM), F32),
        pltpu.VMEM((t + POOL_HALO, D_MODEL), F32),
        pltpu.VMEM((t + CONV_HALO, 3 * D_MODEL), F32),
        pltpu.VMEM((t, 3 * D_MODEL), F32),
        pltpu.VMEM((t, D_MODEL), F32),
        pltpu.VMEM((t, LANES), F32),
        pltpu.VMEM((t, LANES), F32),
        pltpu.VMEM((t // CHUNK, LANES, CHUNK), F32),
        pltpu.VMEM((t // CHUNK, DN_HEADS, CHUNK, DN_HEAD_DIM), F32),
        pltpu.VMEM((t // CHUNK, DN_HEADS, 2 * CHUNK, DN_HEAD_DIM), BF16),
        pltpu.VMEM((t // CHUNK, DN_HEADS, CHUNK, CHUNK), BF16),
        pltpu.VMEM((t // CHUNK, DN_HEADS, CHUNK, DN_HEAD_DIM), BF16),
        pltpu.VMEM((t // CHUNK, DN_HEADS, DN_HEAD_DIM), F32),
    ]
    return pl.pallas_call(
        functools.partial(_mixer_kernel, pad_rows, t_rows),
        grid=(batch, n_t),
        in_specs=in_specs,
        out_specs=out_specs,
        out_shape=out_shape,
        scratch_shapes=scratch,
        compiler_params=pltpu.CompilerParams(
            dimension_semantics=("arbitrary", "arbitrary"), vmem_limit_bytes=VMEM_LIMIT),
        name="mixer_meta" if pad_rows else "mixer",
    )(p_main, pab, pabt, x2d, s0, ph0, ch0, *params)


def _max_tiles(n_tokens):
    return 2 * n_tokens // MOE_TILE + N_EXPERTS


def _route_kernel(ri_ref, tri_ref, pos_ref, tinfo_ref, cnt_scr, base_scr):
    ph = pl.program_id(0)
    i = pl.program_id(1)
    rb = ri_ref.shape[1]
    eid = lax.broadcasted_iota(jnp.int32, (N_EXPERTS, rb), 0)
    oh1 = jnp.where(ri_ref[0:1, :] == eid, 1.0, 0.0).astype(F32)
    oh2 = jnp.where(ri_ref[1:2, :] == eid, 1.0, 0.0).astype(F32)
    ohs = oh1 + oh2

    @pl.when((ph == 0) & (i == 0))
    def _():
        cnt_scr[...] = jnp.zeros_like(cnt_scr)

    @pl.when(ph == 0)
    def _():
        cnt_scr[...] += ohs

    @pl.when((ph == 1) & (i == 0))
    def _():
        counts = jnp.sum(cnt_scr[...], axis=1, keepdims=True)
        padded = jnp.floor((counts + (MOE_TILE - 1)) * (1.0 / MOE_TILE)) * MOE_TILE
        padded_b = jnp.broadcast_to(padded, (N_EXPERTS, LANES))
        r = lax.broadcasted_iota(jnp.int32, (N_EXPERTS, N_EXPERTS), 0)
        c = lax.broadcasted_iota(jnp.int32, (N_EXPERTS, N_EXPERTS), 1)
        strict = jnp.where(r > c, 1.0, 0.0).astype(F32)
        offs = jnp.dot(strict, padded_b, preferred_element_type=F32,
                       precision=lax.Precision.HIGHEST)
        base_scr[...] = offs
        ends = offs[:, 0:1] + padded
        total = jnp.sum(padded, axis=0, keepdims=True)
        n_lanes = tinfo_ref.shape[1]
        tile_start = (lax.broadcasted_iota(jnp.int32, (N_EXPERTS, n_lanes), 1) * MOE_TILE).astype(F32)
        texp = jnp.sum(jnp.where(ends <= tile_start, 1.0, 0.0), axis=0, keepdims=True)
        last_active = jnp.sum(jnp.where(ends <= total - MOE_TILE, 1.0, 0.0), axis=0, keepdims=True)
        texp = jnp.minimum(texp, last_active)
        nact = jnp.broadcast_to(total * (1.0 / MOE_TILE), (1, n_lanes))
        rid = lax.broadcasted_iota(jnp.int32, tinfo_ref.shape, 0)
        tinfo_ref[...] = jnp.where(rid == 0, texp, jnp.where(rid == 1, nact, 0.0)).astype(jnp.int32)

    @pl.when(ph == 1)
    def _():
        incl = _dot(ohs.astype(BF16), tri_ref[...])
        slot = base_scr[:, 0:1] + incl - ohs
        pos1 = jnp.sum(oh1 * slot, axis=0, keepdims=True)
        pos2 = jnp.sum(oh2 * slot, axis=0, keepdims=True)
        rid = lax.broadcasted_iota(jnp.int32, pos_ref.shape, 0)
        pos_ref[...] = jnp.where(rid == 0, pos1, jnp.where(rid == 1, pos2, 0.0)).astype(jnp.int32)
        base_scr[...] += jnp.sum(ohs, axis=1, keepdims=True)


def _route(ri, tri, *, n_tile_lanes):
    n = ri.shape[1]
    rb = min(ROUTE_BLOCK, n)
    return pl.pallas_call(
        _route_kernel,
        grid=(2, n // rb),
        in_specs=[
            pl.BlockSpec((SUBLANES, rb), lambda p, i: (0, i)),
            pl.BlockSpec((rb, rb), lambda p, i: (0, 0)),
        ],
        out_specs=[
            pl.BlockSpec((SUBLANES, rb), lambda p, i: (0, i * p)),
            pl.BlockSpec((SUBLANES, n_tile_lanes), lambda p, i: (0, 0)),
        ],
        out_shape=[
            jax.ShapeDtypeStruct((SUBLANES, n), jnp.int32),
            jax.ShapeDtypeStruct((SUBLANES, n_tile_lanes), jnp.int32),
        ],
        scratch_shapes=[pltpu.VMEM((N_EXPERTS, rb), F32), pltpu.VMEM((N_EXPERTS, LANES), F32)],
        compiler_params=pltpu.CompilerParams(dimension_semantics=("arbitrary", "arbitrary")),
        name="route",
    )(ri, tri)


def _row_copy(src_ref, src_row, dst_ref, dst_row, sem):
    return pltpu.make_async_copy(src_ref.at[pl.ds(src_row, 1)], dst_ref.at[pl.ds(dst_row, 1)], sem)


def _scatter_kernel(pos_hbm, h_ref, xs_in_hbm, xs_hbm, idx_smem, idx_sem, row_sem):
    del xs_in_hbm
    i = pl.program_id(0)
    tb = h_ref.shape[0]
    idx_cp = pltpu.make_async_copy(pos_hbm.at[i], idx_smem, idx_sem)
    idx_cp.start()
    idx_cp.wait()

    def issue(r, carry):
        for k in range(2):
            _row_copy(h_ref, r, xs_hbm, idx_smem[k * tb + r], row_sem).start()
        return carry

    lax.fori_loop(0, tb, issue, 0, unroll=8)

    def drain(r, carry):
        for k in range(2):
            _row_copy(h_ref, 0, xs_hbm, 0, row_sem).wait()
        return carry

    lax.fori_loop(0, tb, drain, 0, unroll=8)


def _scatter(pos_blocks, h, xs_zero):
    n = h.shape[0]
    tb = pos_blocks.shape[1] // 2
    return pl.pallas_call(
        _scatter_kernel,
        grid=(n // tb,),
        in_specs=[
            pl.BlockSpec(memory_space=pl.ANY),
            pl.BlockSpec((tb, D_MODEL), lambda i: (i, 0)),
            pl.BlockSpec(memory_space=pl.ANY),
        ],
        out_specs=pl.BlockSpec(memory_space=pl.ANY),
        out_shape=jax.ShapeDtypeStruct(xs_zero.shape, xs_zero.dtype),
        scratch_shapes=[pltpu.SMEM((2 * tb,), jnp.int32), pltpu.SemaphoreType.DMA(()),
                        pltpu.SemaphoreType.DMA(())],
        input_output_aliases={2: 0},
        compiler_params=pltpu.CompilerParams(dimension_semantics=("arbitrary",)),
        name="scatter_rows",
    )(pos_blocks, h, xs_zero)


def _experts_kernel(texp_ref, nact_ref, xs_ref, nffn_ref, wg_ref, wu_ref, wd_ref, ys_ref,
                    wg_b, wu_b, wd_b):
    i = pl.program_id(0)
    e = texp_ref[i]
    e_prev = texp_ref[jnp.maximum(i - 1, 0)]

    @pl.when((i == 0) | (e != e_prev))
    def _():
        wg_b[...] = wg_ref[0].astype(BF16)
        wu_b[...] = wu_ref[0].astype(BF16)
        wd_b[...] = wd_ref[0].astype(BF16)

    @pl.when(i < nact_ref[0])
    def _():
        hrow = xs_ref[...]
        x = (hrow * lax.rsqrt(jnp.mean(hrow * hrow, axis=-1, keepdims=True) + NORM_EPS)
             * nffn_ref[...]).astype(BF16)
        gate = _dot(x, wg_b[...])
        up = _dot(x, wu_b[...])
        hdn = (gate * _sigmoid(gate) * up).astype(BF16)
        ys_ref[...] = _dot(hdn, wd_b[...])


def _experts(texp, nact, xs, nffn, wg, wu, wd):
    n_tiles = xs.shape[0] // MOE_TILE
    tile_blk = lambda i, texp, nact: (jnp.minimum(i, nact[0] - 1), 0)
    w_blk = lambda i, texp, nact: (texp[i], 0, 0)
    grid_spec = pltpu.PrefetchScalarGridSpec(
        num_scalar_prefetch=2,
        grid=(n_tiles,),
        in_specs=[
            pl.BlockSpec((MOE_TILE, D_MODEL), tile_blk),
            pl.BlockSpec((1, D_MODEL), lambda i, texp, nact: (0, 0)),
            pl.BlockSpec((1, D_MODEL, D_FF_EXPERT), w_blk),
            pl.BlockSpec((1, D_MODEL, D_FF_EXPERT), w_blk),
            pl.BlockSpec((1, D_FF_EXPERT, D_MODEL), w_blk),
        ],
        out_specs=pl.BlockSpec((MOE_TILE, D_MODEL), tile_blk),
        scratch_shapes=[pltpu.VMEM((D_MODEL, D_FF_EXPERT), BF16),
                        pltpu.VMEM((D_MODEL, D_FF_EXPERT), BF16),
                        pltpu.VMEM((D_FF_EXPERT, D_MODEL), BF16)],
    )
    return pl.pallas_call(
        _experts_kernel,
        grid_spec=grid_spec,
        out_shape=jax.ShapeDtypeStruct(xs.shape, xs.dtype),
        input_output_aliases={2: 0},
        compiler_params=pltpu.CompilerParams(
            dimension_semantics=("arbitrary",), vmem_limit_bytes=VMEM_LIMIT),
        name="experts",
    )(texp, nact, xs, nffn, wg, wu, wd)


def _combine_kernel(pos_hbm, ys_hbm, h_ref, rf_ref, nfin_ref, out_ref, idx_smem, ybuf, idx_sem, row_sem):
    i = pl.program_id(0)
    tb = h_ref.shape[0]
    idx_cp = pltpu.make_async_copy(pos_hbm.at[i], idx_smem, idx_sem)
    idx_cp.start()
    idx_cp.wait()

    def issue(r, carry):
        for k in range(2):
            _row_copy(ys_hbm, idx_smem[k * tb + r], ybuf.at[k], r, row_sem).start()
        return carry

    lax.fori_loop(0, tb, issue, 0, unroll=8)

    def drain(r, carry):
        for k in range(2):
            _row_copy(ys_hbm, 0, ybuf.at[k], 0, row_sem).wait()
        return carry

    lax.fori_loop(0, tb, drain, 0, unroll=8)

    rf = rf_ref[...]
    rf_cols = jnp.transpose(jnp.concatenate(
        [rf, jnp.zeros((LANES - SUBLANES, tb), F32)], axis=0))
    hh = h_ref[...] + rf_cols[:, 0:1] * ybuf[0] + rf_cols[:, 1:2] * ybuf[1]
    out_ref[...] = hh * lax.rsqrt(jnp.mean(hh * hh, axis=-1, keepdims=True) + NORM_EPS) * nfin_ref[...]


def _combine(pos_blocks, ys, h, rf, nfin):
    n = h.shape[0]
    tb = pos_blocks.shape[1] // 2
    return pl.pallas_call(
        _combine_kernel,
        grid=(n // tb,),
        in_specs=[
            pl.BlockSpec(memory_space=pl.ANY),
            pl.BlockSpec(memory_space=pl.ANY),
            pl.BlockSpec((tb, D_MODEL), lambda i: (i, 0)),
            pl.BlockSpec((SUBLANES, tb), lambda i: (0, i)),
            pl.BlockSpec((1, D_MODEL), lambda i: (0, 0)),
        ],
        out_specs=pl.BlockSpec((tb, D_MODEL), lambda i: (i, 0)),
        out_shape=jax.ShapeDtypeStruct((n, D_MODEL), F32),
        scratch_shapes=[pltpu.SMEM((2 * tb,), jnp.int32), pltpu.VMEM((2, tb, D_MODEL), F32),
                        pltpu.SemaphoreType.DMA(()), pltpu.SemaphoreType.DMA(())],
        compiler_params=pltpu.CompilerParams(dimension_semantics=("arbitrary",)),
        name="combine",
    )(pos_blocks, ys, h, rf, nfin)


def _moe(h, rf, ri, nffn, wg, wu, wd, nfin):
    n = h.shape[0]
    max_tiles = _max_tiles(n)
    n_tile_lanes = -(-max_tiles // LANES) * LANES
    rb = min(ROUTE_BLOCK, n)
    tri = jnp.triu(jnp.ones((rb, rb), BF16))
    pos, tinfo = _route(ri, tri, n_tile_lanes=n_tile_lanes)
    texp = tinfo[0, :max_tiles]
    nact = tinfo[1, :1]
    tb = min(ROW_BLOCK, n)
    pos_blocks = pos[0:2].reshape(2, n // tb, tb).transpose(1, 0, 2).reshape(n // tb, 2 * tb)
    xs_zero = jnp.zeros((max_tiles * MOE_TILE, D_MODEL), F32)
    xs = _scatter(pos_blocks, h, xs_zero)
    ys = _experts(texp, nact, xs, nffn, wg, wu, wd)
    return _combine(pos_blocks, ys, h, rf, nfin)


def _pad_lanes_row(v):
    return jnp.pad(v.astype(F32), (0, LANES - v.shape[0]))[None, :]


def _block_forward(x, meta_tokens, norm_mix_w, w_in, conv_w, pool_w, pool_scale, a_log, dt_bias,
                   dn_norm_w, w_out, norm_ffn_w, router_group_w, router_group_b, router_expert_w,
                   router_expert_b, expert_w_gate, expert_w_up, expert_w_down, norm_final_w,
                   *, mixer_rows, inproj_rows, inproj_cols):
    bsz, seq, _ = x.shape
    n = bsz * seq
    x2d = x.reshape(n, D_MODEL)

    ab0 = 5 * D_MODEL
    w_main = jnp.concatenate([w_in[:, :ab0], w_in[:, ab0 + 2 * DN_HEADS:]], axis=1).astype(BF16)
    wab = jnp.pad(w_in[:, ab0:ab0 + 2 * DN_HEADS], ((0, 0), (0, LANES - 2 * DN_HEADS))).astype(BF16)
    wabt = wab.T
    nmix = norm_mix_w[None, :]
    alog = _pad_lanes_row(a_log)
    dtb = _pad_lanes_row(dt_bias)
    alogt = jnp.broadcast_to(alog.T, (LANES, LANES))
    dtbt = jnp.broadcast_to(dtb.T, (LANES, LANES))
    wr = jnp.zeros((D_MODEL, LANES), F32)
    wr = wr.at[:, 0:N_EXPERT_GROUPS].set(router_group_w)
    wr = wr.at[:, SUBLANES:SUBLANES + N_EXPERTS].set(router_expert_w)
    br = jnp.zeros((LANES,), F32)
    br = br.at[0:N_EXPERT_GROUPS].set(router_group_b)
    br = br.at[SUBLANES:SUBLANES + N_EXPERTS].set(router_expert_b)
    params = (conv_w, pool_w.astype(BF16), pool_scale[None, :], alog, dtb, alogt, dtbt,
              dn_norm_w[None, :], w_out.astype(BF16), norm_ffn_w[None, :],
              wr.T, jnp.broadcast_to(br[:, None], (LANES, LANES)))

    pad_rows = CHUNK - N_META
    xm = jnp.concatenate([jnp.zeros((pad_rows, D_MODEL), F32), meta_tokens], axis=0)
    pm, pabm, pabtm = _inproj(xm, nmix, w_main, wab, wabt, tm=CHUNK, tn=inproj_cols)
    zeros_s = jnp.zeros((DN_HEADS, DN_HEAD_DIM, DN_HEAD_DIM), F32)
    zeros_ph = jnp.zeros((POOL_HALO, D_MODEL), F32)
    zeros_ch = jnp.zeros((CONV_HALO, 3 * D_MODEL), F32)
    meta_out = _mixer(pm, pabm, pabtm, xm, zeros_s, zeros_ph, zeros_ch, params,
                      batch=1, t_rows=CHUNK, pad_rows=pad_rows)
    s_meta, ph_meta, ch_meta = meta_out[3], meta_out[4], meta_out[5]

    p_main, pab, pabt = _inproj(x2d, nmix, w_main, wab, wabt, tm=inproj_rows, tn=inproj_cols)
    h, rf, ri, _, _, _ = _mixer(p_main, pab, pabt, x2d, s_meta, ph_meta, ch_meta, params,
                                    batch=bsz, t_rows=mixer_rows, pad_rows=0)
    wg = expert_w_gate.reshape(N_EXPERTS, D_MODEL, D_FF_EXPERT)
    wu = expert_w_up.reshape(N_EXPERTS, D_MODEL, D_FF_EXPERT)
    wd = expert_w_down.reshape(N_EXPERTS, D_FF_EXPERT, D_MODEL)
    out = _moe(h, rf, ri, norm_ffn_w[None, :], wg, wu, wd, norm_final_w[None, :])
    return out.reshape(bsz, seq, D_MODEL)


def kernel(x, meta_tokens, norm_mix_w, w_in, conv_w, pool_w, pool_scale, a_log, dt_bias, dn_norm_w, w_out, norm_ffn_w, router_group_w, router_group_b, router_expert_w, router_expert_b, expert_w_gate, expert_w_up, expert_w_down, norm_final_w):
    assert norm_mix_w.shape[0] == 1, "single-layer block"
    seq = x.shape[1]
    return _block_forward(
        x, meta_tokens, norm_mix_w[0], w_in[0], conv_w[0], pool_w[0], pool_scale[0], a_log[0],
        dt_bias[0], dn_norm_w[0], w_out[0], norm_ffn_w[0], router_group_w[0], router_group_b[0],
        router_expert_w[0], router_expert_b[0], expert_w_gate[0], expert_w_up[0], expert_w_down[0],
        norm_final_w,
        mixer_rows=min(256, seq), inproj_rows=min(1024, x.shape[0] * seq), inproj_cols=1024)
```

```python
import functools
import math

import jax
import jax.numpy as jnp
from jax import lax
from jax.experimental import pallas as pl
from jax.experimental.pallas import tpu as pltpu

F32 = jnp.float32
BF16 = jnp.bfloat16

D_MODEL = 1024
N_META = 16
POOL_WINDOWS = (2, 4, 8, 16)
POOL_GROUP_DIM = 256
DN_HEADS = 8
DN_HEAD_DIM = 128
CONV_WIDTH = 4
CHUNK = 64
N_EXPERT_GROUPS = 4
EXPERTS_PER_GROUP = 8
N_EXPERTS = 32
D_FF_EXPERT = 512
NORM_EPS = 1e-6

LANES = 128
SUBLANES = 8
P_MAIN_COLS = 7 * D_MODEL
POOL_HALO = 16
CONV_HALO = 8
VMEM_LIMIT = 56 * 1024 * 1024
MOE_TILE = 256
ROUTE_BLOCK = 512
ROW_BLOCK = 256


def _dot(a, b):
    return jnp.dot(a, b, preferred_element_type=F32)


def _dot_nt(a, b):
    return lax.dot_general(a, b, (((1,), (1,)), ((), ())), preferred_element_type=F32)


def _dot_tn(a, b):
    return lax.dot_general(a, b, (((0,), (0,)), ((), ())), preferred_element_type=F32)


def _sigmoid(x):
    return 1.0 / (1.0 + jnp.exp(-x))


def _intra_chunk(qs, ks, vs, bcols, gcols, grows):
    c = qs[0].shape[0]
    hs = range(len(qs))
    ii = lax.broadcasted_iota(jnp.int32, (c, c), 0)
    jj = lax.broadcasted_iota(jnp.int32, (c, c), 1)
    dec =[jnp.exp(jnp.where(ii >= jj, gcols[h] - grows[h], -jnp.inf)) for h in hs]
    kb = [ks[h].astype(BF16) for h in hs]
    qkb = [jnp.concatenate([qs[h].astype(BF16), kb[h]], axis=0) for h in hs]
    qkk = [_dot_nt(qkb[h], kb[h]) for h in hs]
    egc = [jnp.exp(gcols[h]) for h in hs]
    pw = [jnp.where(ii > jj, -(bcols[h] * qkk[h][c:] * dec[h]), 0.0) for h in hs]
    sol = [jnp.concatenate([vs[h] * bcols[h], ks[h] * (bcols[h] * egc[h])], axis=1) for h in hs]
    width = 2 * DN_HEAD_DIM
    levels = int(math.log2(c))
    for lvl in range(levels):
        pb = [pw[h].astype(BF16) for h in hs]
        if lvl < levels - 1:
            r = [_dot(pb[h], jnp.concatenate([sol[h].astype(BF16), pb[h]], axis=1)) for h in hs]
            sol = [sol[h] + r[h][:, :width] for h in hs]
            pw = [r[h][:, width:] for h in hs]
        else:
            sol = [sol[h] + _dot(pb[h], sol[h].astype(BF16)) for h in hs]
    qd = [qs[h] * egc[h] for h in hs]
    glast = [gcols[h][c - 1:c, :] for h in hs]
    kd = [(ks[h] * jnp.exp(glast[h] - gcols[h])).astype(BF16) for h in hs]
    u = [sol[h][:, :DN_HEAD_DIM] for h in hs]
    wq = [jnp.concatenate([sol[h][:, DN_HEAD_DIM:], qd[h]], axis=0).astype(BF16) for h in hs]
    qkd = [(qkk[h][:c] * dec[h]).astype(BF16) for h in hs]
    cd = [jnp.broadcast_to(jnp.exp(glast[h]), (1, DN_HEAD_DIM)) for h in hs]
    return u, wq, qkd, kd, cd


def _scan_chunk(u, wq, qkd, kd, cd, s):
    c = u[0].shape[0]
    hs = range(len(u))
    sb = [s[h].astype(BF16) for h in hs]
    ws = [_dot(wq[h], sb[h]) for h in hs]
    vb = [(u[h] - ws[h][:c]).astype(BF16) for h in hs]
    o = [ws[h][c:] + _dot(qkd[h], vb[h]) for h in hs]
    s_new = [s[h] * cd[h] + _dot_tn(kd[h], vb[h]) for h in hs]
    return o, s_new


def _mixer_kernel(pad_rows, t_rows,
                  x_ref, s0_ref, ph0_ref, ch0_ref, nmix_ref, win_ref, wab_ref, wabt_ref,
                  convw_ref, poolw_ref, pscale_ref, alog_ref, dtb_ref, alogt_ref, dtbt_ref,
                  dnw_ref, wout_ref, nffn_ref, wrt_ref, brt_ref,
                  h_ref, rf_ref, ri_ref, s_out_ref, ph_out_ref, ch_out_ref,
                  s_scr, pool_buf, conv_buf, zg_scr, qkv_scr, o_scr, beta_scr, gcol_scr, grow_scr,
                  u_scr, wq_scr, qkd_scr, kd_scr, cd_scr):
    t = t_rows
    j = pl.program_id(1)
    n_chunks = t // CHUNK

    @pl.when(j == 0)
    def _():
        s_scr[...] = s0_ref[...]
        pool_buf[0:POOL_HALO, :] = ph0_ref[...]
        conv_buf[0:CONV_HALO, :] = ch0_ref[...]

    x = x_ref[...]
    un = (x * lax.rsqrt(jnp.mean(x * x, axis=-1, keepdims=True) + NORM_EPS) * nmix_ref[...]).astype(BF16)

    def project(col_block):
        return _dot(un, win_ref[:, col_block * D_MODEL:(col_block + 1) * D_MODEL])

    def pooling_mixer():
        p_cur = pool_buf[POOL_HALO:POOL_HALO + t, :]
        mixed = []
        for gi, win in enumerate(POOL_WINDOWS):
            cs = slice(gi * POOL_GROUP_DIM, (gi + 1) * POOL_GROUP_DIM)
            acc = pool_buf[:, cs]
            shift = 1
            while shift < win:
                acc = acc + pltpu.roll(acc, shift, axis=0)
                shift *= 2
            pooled = acc[POOL_HALO:, :] * (1.0 / win) - p_cur[:, cs]
            mixed.append(_dot(pooled.astype(BF16), poolw_ref[gi]))
        return jnp.concatenate(mixed, axis=1) * pscale_ref[...]

    def conv_block(blk):
        cs = slice(blk * D_MODEL, (blk + 1) * D_MODEL)
        acc = convw_ref[CONV_WIDTH - 1:CONV_WIDTH, cs] * conv_buf[CONV_HALO:CONV_HALO + t, cs]
        for kk in range(CONV_WIDTH - 1):
            off = CONV_HALO - (CONV_WIDTH - 1) + kk
            acc = acc + convw_ref[kk:kk + 1, cs] * conv_buf[off:off + t, cs]
        act = acc * _sigmoid(acc)
        if blk == 2:
            qkv_scr[:, cs] = act
            return
        for hh in range(DN_HEADS):
            ls = slice(hh * DN_HEAD_DIM, (hh + 1) * DN_HEAD_DIM)
            part = act[:, ls]
            nrm = lax.rsqrt(jnp.sum(part * part, axis=-1, keepdims=True) + NORM_EPS)
            if blk == 0:
                nrm = nrm * (DN_HEAD_DIM ** -0.5)
            qkv_scr[:, blk * D_MODEL + hh * DN_HEAD_DIM:blk * D_MODEL + (hh + 1) * DN_HEAD_DIM] = part * nrm

    pool_buf[POOL_HALO:POOL_HALO + t, :] = project(0)
    conv_buf[CONV_HALO:CONV_HALO + t, 0:D_MODEL] = project(1)
    y_pool = pooling_mixer()
    conv_buf[CONV_HALO:CONV_HALO + t, D_MODEL:2 * D_MODEL] = project(2)
    conv_block(0)
    conv_buf[CONV_HALO:CONV_HALO + t, 2 * D_MODEL:3 * D_MODEL] = project(3)
    conv_block(1)
    zg_scr[:, 0:D_MODEL] = project(4)
    conv_block(2)
    zg_scr[:, D_MODEL:2 * D_MODEL] = project(5)
    zg_scr[:, 2 * D_MODEL:3 * D_MODEL] = project(6)
    pab = _dot(un, wab_ref[...])
    pabt = _dot_nt(wabt_ref[...], un)

    gcol_all = -jnp.exp(alog_ref[...]) * _softplus(pab + dtb_ref[...])
    beta_all = _sigmoid(pab)
    ab_rows = 2 * DN_HEADS
    grow_all = (-jnp.exp(alogt_ref[0:ab_rows, 0:1])
                * _softplus(pabt[0:ab_rows, :] + dtbt_ref[0:ab_rows, 0:1]))
    if pad_rows:
        rid = lax.broadcasted_iota(jnp.int32, (t, LANES), 0)
        gcol_all = jnp.where(rid >= pad_rows, gcol_all, 0.0)
        beta_all = jnp.where(rid >= pad_rows, beta_all, 0.0)
        cid = lax.broadcasted_iota(jnp.int32, (ab_rows, t), 1)
        grow_all = jnp.where(cid >= pad_rows, grow_all, 0.0)
    beta_scr[...] = beta_all
    ci = lax.broadcasted_iota(jnp.int32, (CHUNK, CHUNK), 0)
    cj = lax.broadcasted_iota(jnp.int32, (CHUNK, CHUNK), 1)
    tri_l = jnp.where(ci >= cj, 1.0, 0.0).astype(F32)
    tri_u = jnp.where(ci <= cj, 1.0, 0.0).astype(F32)
    for c in range(n_chunks):
        rs = slice(c * CHUNK, (c + 1) * CHUNK)
        gcol_scr[rs, :] = jnp.dot(tri_l, gcol_all[rs, :], preferred_element_type=F32,
                                  precision=lax.Precision.HIGHEST)
        grow_scr[c] = jnp.dot(grow_all[:, rs], tri_u, preferred_element_type=F32,
                              precision=lax.Precision.HIGHEST)

    heads = range(DN_HEADS)

    def head_cols(base, hh):
        return slice(base + hh * DN_HEAD_DIM, base + (hh + 1) * DN_HEAD_DIM)

    per_step = 2 if n_chunks % 2 == 0 else 1

    def intra_step(step, carry):
        qs, ks, vs, bcols, gcols, grows = [], [], [], [], [], []
        for ci in range(per_step):
            c = step * per_step + ci
            r0 = pl.multiple_of(c * CHUNK, CHUNK)
            gcol_c = gcol_scr[pl.ds(r0, CHUNK), :]
            beta_c = beta_scr[pl.ds(r0, CHUNK), :]
            grow_c = grow_scr[c]
            for hh in heads:
                qs.append(qkv_scr[pl.ds(r0, CHUNK), head_cols(0, hh)])
                ks.append(qkv_scr[pl.ds(r0, CHUNK), head_cols(D_MODEL, hh)])
                vs.append(qkv_scr[pl.ds(r0, CHUNK), head_cols(2 * D_MODEL, hh)])
                bcols.append(beta_c[:, DN_HEADS + hh:DN_HEADS + hh + 1])
                gcols.append(gcol_c[:, hh:hh + 1])
                grows.append(grow_c[hh:hh + 1, :])
        u, wq, qkd, kd, cd = _intra_chunk(qs, ks, vs, bcols, gcols, grows)
        for ci in range(per_step):
            c = step * per_step + ci
            for hh in heads:
                idx = ci * DN_HEADS + hh
                u_scr[c, hh] = u[idx]
                wq_scr[c, hh] = wq[idx]
                qkd_scr[c, hh] = qkd[idx]
                kd_scr[c, hh] = kd[idx]
            cd_scr[c] = jnp.concatenate(cd[ci * DN_HEADS:(ci + 1) * DN_HEADS], axis=0)
        return carry

    lax.fori_loop(0, n_chunks // per_step, intra_step, 0)

    def scan_step(c, carry):
        r0 = pl.multiple_of(c * CHUNK, CHUNK)
        cds = cd_scr[c]
        o, s_new = _scan_chunk(
            [u_scr[c, hh] for hh in heads], [wq_scr[c, hh] for hh in heads],
            [qkd_scr[c, hh] for hh in heads], [kd_scr[c, hh] for hh in heads],
            [cds[hh:hh + 1, :] for hh in heads], [s_scr[hh] for hh in heads])
        for hh in heads:
            s_scr[hh] = s_new[hh]
            o_scr[pl.ds(r0, CHUNK), head_cols(0, hh)] = o[hh]
        return carry

    lax.fori_loop(0, n_chunks, scan_step, 0)

    z = zg_scr[:, 0:D_MODEL]
    y_dn_parts = []
    for hh in range(DN_HEADS):
        ls = slice(hh * DN_HEAD_DIM, (hh + 1) * DN_HEAD_DIM)
        o = o_scr[:, ls]
        o = o * lax.rsqrt(jnp.mean(o * o, axis=-1, keepdims=True) + NORM_EPS) * dnw_ref[...]
        zz = z[:, ls]
        y_dn_parts.append(o * (zz * _sigmoid(zz)))
    y_dn = jnp.concatenate(y_dn_parts, axis=1)

    g_pool = zg_scr[:, D_MODEL:2 * D_MODEL]
    g_dn = zg_scr[:, 2 * D_MODEL:3 * D_MODEL]
    merged = _sigmoid(g_pool) * y_pool + _sigmoid(g_dn) * y_dn
    h = x_ref[...] + _dot(merged.astype(BF16), wout_ref[...])
    h_ref[...] = h

    xt = h * lax.rsqrt(jnp.mean(h * h, axis=-1, keepdims=True) + NORM_EPS) * nffn_ref[...]
    logits = lax.dot_general(wrt_ref[...], xt, (((1,), (1,)), ((), ())),
                             preferred_element_type=F32, precision=lax.Precision.HIGHEST)
    logits = logits + brt_ref[...][:, 0:1]
    rid8 = lax.broadcasted_iota(jnp.int32, (SUBLANES, t), 0)
    lg = jnp.where(rid8 < N_EXPERT_GROUPS, logits[0:SUBLANES, :], -jnp.inf)
    gmax = jnp.max(lg, axis=0, keepdims=True)
    g_idx = jnp.min(jnp.where(lg == gmax, rid8, SUBLANES), axis=0, keepdims=True)
    p_grp = 1.0 / jnp.sum(jnp.exp(lg - gmax), axis=0, keepdims=True)
    sel = jnp.zeros((EXPERTS_PER_GROUP, t), F32)
    for gi in range(N_EXPERT_GROUPS):
        r0 = SUBLANES + gi * EXPERTS_PER_GROUP
        sel = jnp.where(g_idx == gi, logits[r0:r0 + EXPERTS_PER_GROUP, :], sel)
    m1 = jnp.max(sel, axis=0, keepdims=True)
    i1 = jnp.min(jnp.where(sel == m1, rid8, SUBLANES), axis=0, keepdims=True)
    sel2 = jnp.where(rid8 == i1, -jnp.inf, sel)
    m2 = jnp.max(sel2, axis=0, keepdims=True)
    i2 = jnp.min(jnp.where(sel2 == m2, rid8, SUBLANES), axis=0, keepdims=True)
    e21 = jnp.exp(m2 - m1)
    w1 = 1.0 / (1.0 + e21)
    c1 = p_grp * w1
    c2 = p_grp * (e21 * w1)
    id1 = g_idx * EXPERTS_PER_GROUP + i1
    id2 = g_idx * EXPERTS_PER_GROUP + i2
    rf_ref[...] = jnp.where(rid8 == 0, c1, jnp.where(rid8 == 1, c2, 0.0))
    ri_ref[...] = jnp.where(rid8 == 0, id1, jnp.where(rid8 == 1, id2, 0))

    pool_buf[0:POOL_HALO, :] = pool_buf[t:t + POOL_HALO, :]
    conv_buf[0:CONV_HALO, :] = conv_buf[t:t + CONV_HALO, :]
    s_out_ref[...] = s_scr[...]
    ph_out_ref[...] = pool_buf[0:POOL_HALO, :]
    ch_out_ref[...] = conv_buf[0:CONV_HALO, :]


def _softplus(x):
    return jnp.maximum(x, 0.0) + jnp.log1p(jnp.exp(-jnp.abs(x)))


def _mixer(x2d, s0, ph0, ch0, params, *, batch, t_rows, pad_rows):
    n = x2d.shape[0]
    n_t = n // batch // t_rows
    t = t_rows
    row_blk = lambda b, j: (b * n_t + j, 0)
    col_blk = lambda b, j: (0, b * n_t + j)
    const2 = lambda b, j: (0, 0)
    const3 = lambda b, j: (0, 0, 0)
    (nmix, w_main, wab, wabt,
     convw, poolw, pscale, alog, dtb, alogt, dtbt, dnw, wout, nffn, wrt, brt) = params
    in_specs = [
        pl.BlockSpec((t, D_MODEL), row_blk),
        pl.BlockSpec((DN_HEADS, DN_HEAD_DIM, DN_HEAD_DIM), const3),
        pl.BlockSpec((POOL_HALO, D_MODEL), const2),
        pl.BlockSpec((CONV_HALO, 3 * D_MODEL), const2),
        pl.BlockSpec(nmix.shape, const2),
        pl.BlockSpec(w_main.shape, const2, pipeline_mode=pl.Buffered(1)),
        pl.BlockSpec(wab.shape, const2),
        pl.BlockSpec(wabt.shape, const2),
        pl.BlockSpec(convw.shape, const2),
        pl.BlockSpec(poolw.shape, const3),
        pl.BlockSpec(pscale.shape, const2),
        pl.BlockSpec(alog.shape, const2),
        pl.BlockSpec(dtb.shape, const2),
        pl.BlockSpec(alogt.shape, const2),
        pl.BlockSpec(dtbt.shape, const2),
        pl.BlockSpec(dnw.shape, const2),
        pl.BlockSpec(wout.shape, const2),
        pl.BlockSpec(nffn.shape, const2),
        pl.BlockSpec(wrt.shape, const2),
        pl.BlockSpec(brt.shape, const2),
    ]
    out_specs = [
        pl.BlockSpec((t, D_MODEL), row_blk),
        pl.BlockSpec((SUBLANES, t), col_blk),
        pl.BlockSpec((SUBLANES, t), col_blk),
        pl.BlockSpec((DN_HEADS, DN_HEAD_DIM, DN_HEAD_DIM), const3),
        pl.BlockSpec((POOL_HALO, D_MODEL), const2),
        pl.BlockSpec((CONV_HALO, 3 * D_MODEL), const2),
    ]
    out_shape = [
        jax.ShapeDtypeStruct((n, D_MODEL), F32),
        jax.ShapeDtypeStruct((SUBLANES, n), F32),
        jax.ShapeDtypeStruct((SUBLANES, n), jnp.int32),
        jax.ShapeDtypeStruct((DN_HEADS, DN_HEAD_DIM, DN_HEAD_DIM), F32),
        jax.ShapeDtypeStruct((POOL_HALO, D_MODEL), F32),
        jax.ShapeDtypeStruct((CONV_HALO, 3 * D_MODEL), F32),
    ]
    scratch = [
        pltpu.VMEM((DN_HEADS, DN_HEAD_DIM, DN_HEAD_DIM), F32),
        pltpu.VMEM((t + POOL_HALO, D_MODEL), F32),
        pltpu.VMEM((t + CONV_HALO, 3 * D_MODEL), F32),
        pltpu.VMEM((t, 3 * D_MODEL), F32),
        pltpu.VMEM((t, 3 * D_MODEL), F32),
        pltpu.VMEM((t, D_MODEL), F32),
        pltpu.VMEM((t, LANES), F32),
        pltpu.VMEM((t, LANES), F32),
        pltpu.VMEM((t // CHUNK, 2 * DN_HEADS, CHUNK), F32),
        pltpu.VMEM((t // CHUNK, DN_HEADS, CHUNK, DN_HEAD_DIM), F32),
        pltpu.VMEM((t // CHUNK, DN_HEADS, 2 * CHUNK, DN_HEAD_DIM), BF16),
        pltpu.VMEM((t // CHUNK, DN_HEADS, CHUNK, CHUNK), BF16),
        pltpu.VMEM((t // CHUNK, DN_HEADS, CHUNK, DN_HEAD_DIM), BF16),
        pltpu.VMEM((t // CHUNK, DN_HEADS, DN_HEAD_DIM), F32),
    ]
    return pl.pallas_call(
        functools.partial(_mixer_kernel, pad_rows, t_rows),
        grid=(batch, n_t),
        in_specs=in_specs,
        out_specs=out_specs,
        out_shape=out_shape,
        scratch_shapes=scratch,
        compiler_params=pltpu.CompilerParams(
            dimension_semantics=("arbitrary", "arbitrary"), vmem_limit_bytes=VMEM_LIMIT),
        name="mixer_meta" if pad_rows else "mixer",
    )(x2d, s0, ph0, ch0, *params)


def _max_tiles(n_tokens):
    return 2 * n_tokens // MOE_TILE + N_EXPERTS


def _route_kernel(ri_ref, tri_ref, pos_ref, tinfo_ref, cnt_scr, base_scr):
    ph = pl.program_id(0)
    i = pl.program_id(1)
    rb = ri_ref.shape[1]
    eid = lax.broadcasted_iota(jnp.int32, (N_EXPERTS, rb), 0)
    oh1 = jnp.where(ri_ref[0:1, :] == eid, 1.0, 0.0).astype(F32)
    oh2 = jnp.where(ri_ref[1:2, :] == eid, 1.0, 0.0).astype(F32)
    ohs = oh1 + oh2

    @pl.when((ph == 0) & (i == 0))
    def _():
        cnt_scr[...] = jnp.zeros_like(cnt_scr)

    @pl.when(ph == 0)
    def _():
        cnt_scr[...] += ohs

    @pl.when((ph == 1) & (i == 0))
    def _():
        counts = jnp.sum(cnt_scr[...], axis=1, keepdims=True)
        padded = jnp.floor((counts + (MOE_TILE - 1)) * (1.0 / MOE_TILE)) * MOE_TILE
        padded_b = jnp.broadcast_to(padded, (N_EXPERTS, LANES))
        r = lax.broadcasted_iota(jnp.int32, (N_EXPERTS, N_EXPERTS), 0)
        c = lax.broadcasted_iota(jnp.int32, (N_EXPERTS, N_EXPERTS), 1)
        strict = jnp.where(r > c, 1.0, 0.0).astype(F32)
        offs = jnp.dot(strict, padded_b, preferred_element_type=F32,
                       precision=lax.Precision.HIGHEST)
        base_scr[...] = offs
        ends = offs[:, 0:1] + padded
        total = jnp.sum(padded, axis=0, keepdims=True)
        n_lanes = tinfo_ref.shape[1]
        tile_start = (lax.broadcasted_iota(jnp.int32, (N_EXPERTS, n_lanes), 1) * MOE_TILE).astype(F32)
        texp = jnp.sum(jnp.where(ends <= tile_start, 1.0, 0.0), axis=0, keepdims=True)
        last_active = jnp.sum(jnp.where(ends <= total - MOE_TILE, 1.0, 0.0), axis=0, keepdims=True)
        texp = jnp.minimum(texp, last_active)
        nact = jnp.broadcast_to(total * (1.0 / MOE_TILE), (1, n_lanes))
        rid = lax.broadcasted_iota(jnp.int32, tinfo_ref.shape, 0)
        tinfo_ref[...] = jnp.where(rid == 0, texp, jnp.where(rid == 1, nact, 0.0)).astype(jnp.int32)

    @pl.when(ph == 1)
    def _():
        incl = _dot(ohs.astype(BF16), tri_ref[...])
        slot = base_scr[:, 0:1] + incl - ohs
        pos1 = jnp.sum(oh1 * slot, axis=0, keepdims=True)
        pos2 = jnp.sum(oh2 * slot, axis=0, keepdims=True)
        rid = lax.broadcasted_iota(jnp.int32, pos_ref.shape, 0)
        pos_ref[...] = jnp.where(rid == 0, pos1, jnp.where(rid == 1, pos2, 0.0)).astype(jnp.int32)
        base_scr[...] += jnp.sum(ohs, axis=1, keepdims=True)


def _route(ri, tri, *, n_tile_lanes):
    n = ri.shape[1]
    rb = min(ROUTE_BLOCK, n)
    return pl.pallas_call(
        _route_kernel,
        grid=(2, n // rb),
        in_specs=[
            pl.BlockSpec((SUBLANES, rb), lambda p, i: (0, i)),
            pl.BlockSpec((rb, rb), lambda p, i: (0, 0)),
        ],
        out_specs=[
            pl.BlockSpec((SUBLANES, rb), lambda p, i: (0, i * p)),
            pl.BlockSpec((SUBLANES, n_tile_lanes), lambda p, i: (0, 0)),
        ],
        out_shape=[
            jax.ShapeDtypeStruct((SUBLANES, n), jnp.int32),
            jax.ShapeDtypeStruct((SUBLANES, n_tile_lanes), jnp.int32),
        ],
        scratch_shapes=[pltpu.VMEM((N_EXPERTS, rb), F32), pltpu.VMEM((N_EXPERTS, LANES), F32)],
        compiler_params=pltpu.CompilerParams(dimension_semantics=("arbitrary", "arbitrary")),
        name="route",
    )(ri, tri)


def _scatter_kernel(tb, pos_hbm, h_hbm, xs_in_hbm, xs_hbm, idx_a, idx_b, hbuf,
                    idx_sems, blk_sems, row_sems):
    del xs_in_hbm
    i = pl.program_id(0)
    n_steps = pl.num_programs(0)
    idx_bufs = (idx_a, idx_b)

    def idx_copy(step, par):
        return pltpu.make_async_copy(pos_hbm.at[step], idx_bufs[par], idx_sems.at[par])

    def blk_copy(step, par):
        return pltpu.make_async_copy(h_hbm.at[pl.ds(step * tb, tb)], hbuf.at[par], blk_sems.at[par])

    def row_copy(par, r, slot):
        return pltpu.make_async_copy(hbuf.at[par, pl.ds(r, 1)], xs_hbm.at[pl.ds(slot, 1)],
                                     row_sems.at[par])

    def drain_rows(par):
        def body(r, carry):
            for k in range(2):
                row_copy(par, 0, 0).wait()
            return carry
        lax.fori_loop(0, tb, body, 0, unroll=8)

    def step(par):
        @pl.when(i == 0)
        def _():
            idx_copy(0, 0).start()
            blk_copy(0, 0).start()

        @pl.when(i > 0)
        def _():
            drain_rows(1 - par)

        @pl.when(i + 1 < n_steps)
        def _():
            idx_copy(i + 1, 1 - par).start()
            blk_copy(i + 1, 1 - par).start()

        idx_copy(i, par).wait()
        blk_copy(i, par).wait()

        def issue(r, carry):
            for k in range(2):
                row_copy(par, r, idx_bufs[par][k * tb + r]).start(priority=k)
            return carry

        lax.fori_loop(0, tb, issue, 0, unroll=8)

        @pl.when(i == n_steps - 1)
        def _():
            drain_rows(par)

    @pl.when(i % 2 == 0)
    def _():
        step(0)

    @pl.when(i % 2 == 1)
    def _():
        step(1)


def _scatter(pos_blocks, h, xs_zero):
    tb = pos_blocks.shape[1] // 2
    return pl.pallas_call(
        functools.partial(_scatter_kernel, tb),
        grid=(pos_blocks.shape[0],),
        in_specs=[
            pl.BlockSpec(memory_space=pl.ANY),
            pl.BlockSpec(memory_space=pl.ANY),
            pl.BlockSpec(memory_space=pl.ANY),
        ],
        out_specs=pl.BlockSpec(memory_space=pl.ANY),
        out_shape=jax.ShapeDtypeStruct(xs_zero.shape, xs_zero.dtype),
        scratch_shapes=[pltpu.SMEM((2 * tb,), jnp.int32), pltpu.SMEM((2 * tb,), jnp.int32),
                        pltpu.VMEM((2, tb, D_MODEL), F32),
                        pltpu.SemaphoreType.DMA((2,)), pltpu.SemaphoreType.DMA((2,)),
                        pltpu.SemaphoreType.DMA((2,))],
        input_output_aliases={2: 0},
        compiler_params=pltpu.CompilerParams(dimension_semantics=("arbitrary",)),
        name="scatter_rows",
    )(pos_blocks, h, xs_zero)


def _experts_kernel(texp_ref, nact_ref, xs_ref, nffn_ref, wg_ref, wu_ref, wd_ref, ys_ref,
                    wg_b, wu_b, wd_b):
    i = pl.program_id(0)
    e = texp_ref[i]
    e_prev = texp_ref[jnp.maximum(i - 1, 0)]

    @pl.when((i == 0) | (e != e_prev))
    def _():
        wg_b[...] = wg_ref[0].astype(BF16)
        wu_b[...] = wu_ref[0].astype(BF16)
        wd_b[...] = wd_ref[0].astype(BF16)

    @pl.when(i < nact_ref[0])
    def _():
        hrow = xs_ref[...]
        x = (hrow * lax.rsqrt(jnp.mean(hrow * hrow, axis=-1, keepdims=True) + NORM_EPS)
             * nffn_ref[...]).astype(BF16)
        gate = _dot(x, wg_b[...])
        up = _dot(x, wu_b[...])
        hdn = (gate * _sigmoid(gate) * up).astype(BF16)
        ys_ref[...] = _dot(hdn, wd_b[...])


def _experts(texp, nact, xs, nffn, wg, wu, wd):
    n_tiles = xs.shape[0] // MOE_TILE
    tile_blk = lambda i, texp, nact: (jnp.minimum(i, nact[0] - 1), 0)
    w_blk = lambda i, texp, nact: (texp[i], 0, 0)
    grid_spec = pltpu.PrefetchScalarGridSpec(
        num_scalar_prefetch=2,
        grid=(n_tiles,),
        in_specs=[
            pl.BlockSpec((MOE_TILE, D_MODEL), tile_blk),
            pl.BlockSpec((1, D_MODEL), lambda i, texp, nact: (0, 0)),
            pl.BlockSpec((1, D_MODEL, D_FF_EXPERT), w_blk),
            pl.BlockSpec((1, D_MODEL, D_FF_EXPERT), w_blk),
            pl.BlockSpec((1, D_FF_EXPERT, D_MODEL), w_blk),
        ],
        out_specs=pl.BlockSpec((MOE_TILE, D_MODEL), tile_blk),
        scratch_shapes=[pltpu.VMEM((D_MODEL, D_FF_EXPERT), BF16),
                        pltpu.VMEM((D_MODEL, D_FF_EXPERT), BF16),
                        pltpu.VMEM((D_FF_EXPERT, D_MODEL), BF16)],
    )
    return pl.pallas_call(
        _experts_kernel,
        grid_spec=grid_spec,
        out_shape=jax.ShapeDtypeStruct(xs.shape, xs.dtype),
        input_output_aliases={2: 0},
        compiler_params=pltpu.CompilerParams(
            dimension_semantics=("arbitrary",), vmem_limit_bytes=VMEM_LIMIT),
        name="experts",
    )(texp, nact, xs, nffn, wg, wu, wd)


def _combine_kernel(pos_hbm, ys_hbm, h_ref, rf_ref, nfin_ref, out_ref,
                    idx_a, idx_b, ybuf, idx_sems, row_sems):
    i = pl.program_id(0)
    n_steps = pl.num_programs(0)
    tb = h_ref.shape[0]
    idx_bufs = (idx_a, idx_b)

    def idx_copy(step, par):
        return pltpu.make_async_copy(pos_hbm.at[step], idx_bufs[par], idx_sems.at[par])

    def issue_rows(par):
        def body(g, carry):
            for sub in range(SUBLANES):
                r = g * SUBLANES + sub
                for k in range(2):
                    pltpu.make_async_copy(
                        ys_hbm.at[pl.ds(idx_bufs[par][k * tb + r], 1)],
                        ybuf.at[par, k, g, pl.ds(sub, 1)],
                        row_sems.at[par]).start(priority=k)
            return carry
        lax.fori_loop(0, tb // SUBLANES, body, 0)

    def drain_rows(par):
        def body(r, carry):
            for k in range(2):
                pltpu.make_async_copy(ys_hbm.at[pl.ds(0, 1)], ybuf.at[par, k, 0, pl.ds(0, 1)],
                                      row_sems.at[par]).wait()
            return carry
        lax.fori_loop(0, tb, body, 0, unroll=8)

    def step(par):
        @pl.when(i == 0)
        def _():
            idx_copy(0, 0).start()
            idx_copy(0, 0).wait()
            issue_rows(0)

            @pl.when(n_steps > 1)
            def _():
                idx_copy(1, 1).start()

        @pl.when(i + 1 < n_steps)
        def _():
            idx_copy(i + 1, 1 - par).wait()
            issue_rows(1 - par)

        @pl.when(i + 2 < n_steps)
        def _():
            idx_copy(i + 2, par).start()

        drain_rows(par)
        rf = rf_ref[...]
        rf_cols = jnp.transpose(jnp.concatenate(
            [rf, jnp.zeros((LANES - SUBLANES, tb), F32)], axis=0))
        y1 = ybuf[par, 0].reshape(tb, D_MODEL)
        y2 = ybuf[par, 1].reshape(tb, D_MODEL)
        hh = h_ref[...] + rf_cols[:, 0:1] * y1 + rf_cols[:, 1:2] * y2
        out_ref[...] = (hh * lax.rsqrt(jnp.mean(hh * hh, axis=-1, keepdims=True) + NORM_EPS)
                        * nfin_ref[...])

    @pl.when(i % 2 == 0)
    def _():
        step(0)

    @pl.when(i % 2 == 1)
    def _():
        step(1)


def _combine(pos_blocks, ys, h, rf, nfin):
    n = h.shape[0]
    tb = pos_blocks.shape[1] // 2
    return pl.pallas_call(
        _combine_kernel,
        grid=(n // tb,),
        in_specs=[
            pl.BlockSpec(memory_space=pl.ANY),
            pl.BlockSpec(memory_space=pl.ANY),
            pl.BlockSpec((tb, D_MODEL), lambda i: (i, 0)),
            pl.BlockSpec((SUBLANES, tb), lambda i: (0, i)),
            pl.BlockSpec((1, D_MODEL), lambda i: (0, 0)),
        ],
        out_specs=pl.BlockSpec((tb, D_MODEL), lambda i: (i, 0)),
        out_shape=jax.ShapeDtypeStruct((n, D_MODEL), F32),
        scratch_shapes=[pltpu.SMEM((2 * tb,), jnp.int32), pltpu.SMEM((2 * tb,), jnp.int32),
                        pltpu.VMEM((2, 2, tb // SUBLANES, SUBLANES, D_MODEL), F32),
                        pltpu.SemaphoreType.DMA((2,)), pltpu.SemaphoreType.DMA((2,))],
        compiler_params=pltpu.CompilerParams(dimension_semantics=("arbitrary",)),
        name="combine",
    )(pos_blocks, ys, h, rf, nfin)


def _moe(h, rf, ri, nffn, wg, wu, wd, nfin):
    n = h.shape[0]
    max_tiles = _max_tiles(n)
    n_tile_lanes = -(-max_tiles // LANES) * LANES
    rb = min(ROUTE_BLOCK, n)
    tri = jnp.triu(jnp.ones((rb, rb), BF16))
    pos, tinfo = _route(ri, tri, n_tile_lanes=n_tile_lanes)
    texp = tinfo[0, :max_tiles]
    nact = tinfo[1, :1]
    tb = min(ROW_BLOCK, n)
    pos_blocks = pos[0:2].reshape(2, n // tb, tb).transpose(1, 0, 2).reshape(n // tb, 2 * tb)
    xs_zero = jnp.zeros((max_tiles * MOE_TILE, D_MODEL), F32)
    xs = _scatter(pos_blocks, h, xs_zero)
    ys = _experts(texp, nact, xs, nffn, wg, wu, wd)
    return _combine(pos_blocks, ys, h, rf, nfin)


def _pad_lanes_row(v):
    return jnp.pad(v.astype(F32), (0, LANES - v.shape[0]))[None, :]


def _block_forward(x, meta_tokens, norm_mix_w, w_in, conv_w, pool_w, pool_scale, a_log, dt_bias,
                   dn_norm_w, w_out, norm_ffn_w, router_group_w, router_group_b, router_expert_w,
                   router_expert_b, expert_w_gate, expert_w_up, expert_w_down, norm_final_w,
                   *, mixer_rows):
    bsz, seq, _ = x.shape
    n = bsz * seq
    x2d = x.reshape(n, D_MODEL)

    ab0 = 5 * D_MODEL
    w_main = jnp.concatenate([w_in[:, :ab0], w_in[:, ab0 + 2 * DN_HEADS:]], axis=1).astype(BF16)
    wab = jnp.pad(w_in[:, ab0:ab0 + 2 * DN_HEADS], ((0, 0), (0, LANES - 2 * DN_HEADS))).astype(BF16)
    wabt = wab.T
    nmix = norm_mix_w[None, :]
    alog = _pad_lanes_row(a_log)
    dtb = _pad_lanes_row(dt_bias)
    alogt = jnp.broadcast_to(alog.T, (LANES, LANES))
    dtbt = jnp.broadcast_to(dtb.T, (LANES, LANES))
    wr = jnp.zeros((D_MODEL, LANES), F32)
    wr = wr.at[:, 0:N_EXPERT_GROUPS].set(router_group_w)
    wr = wr.at[:, SUBLANES:SUBLANES + N_EXPERTS].set(router_expert_w)
    br = jnp.zeros((LANES,), F32)
    br = br.at[0:N_EXPERT_GROUPS].set(router_group_b)
    br = br.at[SUBLANES:SUBLANES + N_EXPERTS].set(router_expert_b)
    params = (nmix, w_main, wab, wabt,
              conv_w, pool_w.astype(BF16), pool_scale[None, :], alog, dtb, alogt, dtbt,
              dn_norm_w[None, :], w_out.astype(BF16), norm_ffn_w[None, :],
              wr.T, jnp.broadcast_to(br[:, None], (LANES, LANES)))

    pad_rows = CHUNK - N_META
    xm = jnp.concatenate([jnp.zeros((pad_rows, D_MODEL), F32), meta_tokens], axis=0)
    zeros_s = jnp.zeros((DN_HEADS, DN_HEAD_DIM, DN_HEAD_DIM), F32)
    zeros_ph = jnp.zeros((POOL_HALO, D_MODEL), F32)
    zeros_ch = jnp.zeros((CONV_HALO, 3 * D_MODEL), F32)
    meta_out = _mixer(xm, zeros_s, zeros_ph, zeros_ch, params,
                      batch=1, t_rows=CHUNK, pad_rows=pad_rows)
    s_meta, ph_meta, ch_meta = meta_out[3], meta_out[4], meta_out[5]

    h, rf, ri, _, _, _ = _mixer(x2d, s_meta, ph_meta, ch_meta, params,
                                batch=bsz, t_rows=mixer_rows, pad_rows=0)
    wg = expert_w_gate.reshape(N_EXPERTS, D_MODEL, D_FF_EXPERT)
    wu = expert_w_up.reshape(N_EXPERTS, D_MODEL, D_FF_EXPERT)
    wd = expert_w_down.reshape(N_EXPERTS, D_FF_EXPERT, D_MODEL)
    out = _moe(h, rf, ri, norm_ffn_w[None, :], wg, wu, wd, norm_final_w[None, :])
    return out.reshape(bsz, seq, D_MODEL)


def kernel(x, meta_tokens, norm_mix_w, w_in, conv_w, pool_w, pool_scale, a_log, dt_bias, dn_norm_w, w_out, norm_ffn_w, router_group_w, router_group_b, router_expert_w, router_expert_b, expert_w_gate, expert_w_up, expert_w_down, norm_final_w):
    assert norm_mix_w.shape[0] == 1, "single-layer block"
    seq = x.shape[1]
    return _block_forward(
        x, meta_tokens, norm_mix_w[0], w_in[0], conv_w[0], pool_w[0], pool_scale[0], a_log[0],
        dt_bias[0], dn_norm_w[0], w_out[0], norm_ffn_w[0], router_group_w[0], router_group_b[0],
        router_expert_w[0], router_expert_b[0], expert_w_gate[0], expert_w_up[0], expert_w_down[0],
        norm_final_w,
        mixer_rows=min(256, seq))
```

```python
import functools
import math

import jax
import jax.numpy as jnp
from jax import lax
from jax.experimental import pallas as pl
from jax.experimental.pallas import tpu as pltpu

F32 = jnp.float32
BF16 = jnp.bfloat16

D_MODEL = 1024
N_META = 16
POOL_WINDOWS = (2, 4, 8, 16)
POOL_GROUP_DIM = 256
DN_HEADS = 8
DN_HEAD_DIM = 128
CONV_WIDTH = 4
CHUNK = 64
N_EXPERT_GROUPS = 4
EXPERTS_PER_GROUP = 8
N_EXPERTS = 32
D_FF_EXPERT = 512
NORM_EPS = 1e-6

LANES = 128
SUBLANES = 8
P_MAIN_COLS = 7 * D_MODEL
POOL_HALO = 16
CONV_HALO = 8
VMEM_LIMIT = 56 * 1024 * 1024
MOE_TILE = 256
ROUTE_BLOCK = 512
ROW_BLOCK = 256


def _dot(a, b):
    return jnp.dot(a, b, preferred_element_type=F32)


def _dot_nt(a, b):
    return lax.dot_general(a, b, (((1,), (1,)), ((), ())), preferred_element_type=F32)


def _dot_tn(a, b):
    return lax.dot_general(a, b, (((0,), (0,)), ((), ())), preferred_element_type=F32)


def _sigmoid(x):
    return 1.0 / (1.0 + jnp.exp(-x))


def _interleave(*stage_generators):
    live = {i: g for i, g in enumerate(stage_generators)}
    results = [None] * len(stage_generators)
    while live:
        for i in list(live):
            try:
                next(live[i])
            except StopIteration as done:
                results[i] = done.value
                del live[i]
    return results


def _intra_chunk(qs, ks, vs, bcols, gcols, grows):
    c = qs[0].shape[0]
    hs = range(len(qs))
    ii = lax.broadcasted_iota(jnp.int32, (c, c), 0)
    jj = lax.broadcasted_iota(jnp.int32, (c, c), 1)
    dec =[jnp.exp(jnp.where(ii >= jj, gcols[h] - grows[h], -jnp.inf)) for h in hs]
    kb = [ks[h].astype(BF16) for h in hs]
    qkb = [jnp.concatenate([qs[h].astype(BF16), kb[h]], axis=0) for h in hs]
    qkk = [_dot_nt(qkb[h], kb[h]) for h in hs]
    yield
    egc = [jnp.exp(gcols[h]) for h in hs]
    pw = [jnp.where(ii > jj, -(bcols[h] * qkk[h][c:] * dec[h]), 0.0) for h in hs]
    sol = [jnp.concatenate([vs[h] * bcols[h], ks[h] * (bcols[h] * egc[h])], axis=1) for h in hs]
    width = 2 * DN_HEAD_DIM
    levels = int(math.log2(c))
    for lvl in range(levels):
        pb = [pw[h].astype(BF16) for h in hs]
        if lvl < levels - 1:
            r = [_dot(pb[h], jnp.concatenate([sol[h].astype(BF16), pb[h]], axis=1)) for h in hs]
            sol = [sol[h] + r[h][:, :width] for h in hs]
            pw = [r[h][:, width:] for h in hs]
        else:
            sol = [sol[h] + _dot(pb[h], sol[h].astype(BF16)) for h in hs]
        yield
    qd = [qs[h] * egc[h] for h in hs]
    glast = [gcols[h][c - 1:c, :] for h in hs]
    kd = [(ks[h] * jnp.exp(glast[h] - gcols[h])).astype(BF16) for h in hs]
    u = [sol[h][:, :DN_HEAD_DIM] for h in hs]
    wq = [jnp.concatenate([sol[h][:, DN_HEAD_DIM:], qd[h]], axis=0).astype(BF16) for h in hs]
    qkd = [(qkk[h][:c] * dec[h]).astype(BF16) for h in hs]
    cd = [jnp.broadcast_to(jnp.exp(glast[h]), (1, DN_HEAD_DIM)) for h in hs]
    return u, wq, qkd, kd, cd


def _scan_chunk(u, wq, qkd, kd, cd, s):
    c = u[0].shape[0]
    hs = range(len(u))
    sb = [s[h].astype(BF16) for h in hs]
    ws = [_dot(wq[h], sb[h]) for h in hs]
    yield
    vb = [(u[h] - ws[h][:c]).astype(BF16) for h in hs]
    o = [ws[h][c:] + _dot(qkd[h], vb[h]) for h in hs]
    s_new = [s[h] * cd[h] + _dot_tn(kd[h], vb[h]) for h in hs]
    yield
    return o, s_new


def _mixer_kernel(pad_rows, t_rows,
                  x_ref, s0_ref, ph0_ref, ch0_ref, nmix_ref, win_ref, wab_ref, wabt_ref,
                  convw_ref, poolw_ref, pscale_ref, alog_ref, dtb_ref, alogt_ref, dtbt_ref,
                  dnw_ref, wout_ref, nffn_ref, wrt_ref, brt_ref,
                  h_ref, rf_ref, ri_ref, s_out_ref, ph_out_ref, ch_out_ref,
                  s_scr, pool_buf, conv_buf, un_scr, ypool_scr, qkv_scr, o_scr, beta_scr, gcol_scr, grow_scr,
                  u_scr, wq_scr, qkd_scr, kd_scr, cd_scr):
    t = t_rows
    j = pl.program_id(1)
    n_chunks = t // CHUNK

    @pl.when(j == 0)
    def _():
        s_scr[...] = s0_ref[...]
        pool_buf[0:POOL_HALO, :] = ph0_ref[...]
        conv_buf[0:CONV_HALO, :] = ch0_ref[...]

    x = x_ref[...]
    un = (x * lax.rsqrt(jnp.mean(x * x, axis=-1, keepdims=True) + NORM_EPS) * nmix_ref[...]).astype(BF16)

    un_scr[...] = un
    sub = POOL_GROUP_DIM
    n_sub = D_MODEL // sub

    def project(col0):
        return _dot(un, win_ref[:, col0:col0 + sub])

    def pool_group(gi):
        win = POOL_WINDOWS[gi]
        cs = slice(gi * sub, (gi + 1) * sub)
        acc = pool_buf[:, cs]
        shift = 1
        while shift < win:
            acc = acc + pltpu.roll(acc, shift, axis=0)
            shift *= 2
        pooled = acc[POOL_HALO:, :] * (1.0 / win) - pool_buf[POOL_HALO:POOL_HALO + t, cs]
        ypool_scr[:, cs] = _dot(pooled.astype(BF16), poolw_ref[gi]) * pscale_ref[:, cs]

    def conv_sub(col0):
        cs = slice(col0, col0 + sub)
        acc = convw_ref[CONV_WIDTH - 1:CONV_WIDTH, cs] * conv_buf[CONV_HALO:CONV_HALO + t, cs]
        for kk in range(CONV_WIDTH - 1):
            off = CONV_HALO - (CONV_WIDTH - 1) + kk
            acc = acc + convw_ref[kk:kk + 1, cs] * conv_buf[off:off + t, cs]
        act = acc * _sigmoid(acc)
        if col0 >= 2 * D_MODEL:
            qkv_scr[:, cs] = act
            return
        for hh in range(sub // DN_HEAD_DIM):
            part = act[:, hh * DN_HEAD_DIM:(hh + 1) * DN_HEAD_DIM]
            nrm = lax.rsqrt(jnp.sum(part * part, axis=-1, keepdims=True) + NORM_EPS)
            if col0 < D_MODEL:
                nrm = nrm * (DN_HEAD_DIM ** -0.5)
            qkv_scr[:, col0 + hh * DN_HEAD_DIM:col0 + (hh + 1) * DN_HEAD_DIM] = part * nrm

    for si in range(n_sub):
        pool_buf[POOL_HALO:POOL_HALO + t, si * sub:(si + 1) * sub] = project(si * sub)
    for si in range(n_sub):
        conv_buf[CONV_HALO:CONV_HALO + t, si * sub:(si + 1) * sub] = project(D_MODEL + si * sub)
        pool_group(si)
    for blk in range(1, 3):
        for si in range(n_sub):
            c0 = blk * D_MODEL + si * sub
            conv_buf[CONV_HALO:CONV_HALO + t, c0:c0 + sub] = project(D_MODEL + c0)
            conv_sub(c0 - D_MODEL)
    pab = _dot(un, wab_ref[...])
    pabt = _dot_nt(wabt_ref[...], un)

    gcol_all = -jnp.exp(alog_ref[...]) * _softplus(pab + dtb_ref[...])
    beta_all = _sigmoid(pab)
    ab_rows = 2 * DN_HEADS
    grow_all = (-jnp.exp(alogt_ref[0:ab_rows, 0:1])
                * _softplus(pabt[0:ab_rows, :] + dtbt_ref[0:ab_rows, 0:1]))
    if pad_rows:
        rid = lax.broadcasted_iota(jnp.int32, (t, LANES), 0)
        gcol_all = jnp.where(rid >= pad_rows, gcol_all, 0.0)
        beta_all = jnp.where(rid >= pad_rows, beta_all, 0.0)
        cid = lax.broadcasted_iota(jnp.int32, (ab_rows, t), 1)
        grow_all = jnp.where(cid >= pad_rows, grow_all, 0.0)
    beta_scr[...] = beta_all
    ci = lax.broadcasted_iota(jnp.int32, (CHUNK, CHUNK), 0)
    cj = lax.broadcasted_iota(jnp.int32, (CHUNK, CHUNK), 1)
    tri_l = jnp.where(ci >= cj, 1.0, 0.0).astype(F32)
    tri_u = jnp.where(ci <= cj, 1.0, 0.0).astype(F32)
    for c in range(n_chunks):
        rs = slice(c * CHUNK, (c + 1) * CHUNK)
        gcol_scr[rs, :] = jnp.dot(tri_l, gcol_all[rs, :], preferred_element_type=F32,
                                  precision=lax.Precision.HIGHEST)
        grow_scr[c] = jnp.dot(grow_all[:, rs], tri_u, preferred_element_type=F32,
                              precision=lax.Precision.HIGHEST)
        if c < n_sub:
            conv_sub(2 * D_MODEL + c * sub)
    for si in range(n_chunks, n_sub):
        conv_sub(2 * D_MODEL + si * sub)

    heads = range(DN_HEADS)

    def head_cols(base, hh):
        return slice(base + hh * DN_HEAD_DIM, base + (hh + 1) * DN_HEAD_DIM)

    def intra_stage(chunks):
        qs, ks, vs, bcols, gcols, grows = [], [], [], [], [], []
        for c in chunks:
            rs = slice(c * CHUNK, (c + 1) * CHUNK)
            gcol_c = gcol_scr[rs, :]
            beta_c = beta_scr[rs, :]
            grow_c = grow_scr[c]
            for hh in heads:
                qs.append(qkv_scr[rs, head_cols(0, hh)])
                ks.append(qkv_scr[rs, head_cols(D_MODEL, hh)])
                vs.append(qkv_scr[rs, head_cols(2 * D_MODEL, hh)])
                bcols.append(beta_c[:, DN_HEADS + hh:DN_HEADS + hh + 1])
                gcols.append(gcol_c[:, hh:hh + 1])
                grows.append(grow_c[hh:hh + 1, :])
        u, wq, qkd, kd, cd = yield from _intra_chunk(qs, ks, vs, bcols, gcols, grows)
        for ci, c in enumerate(chunks):
            for hh in heads:
                idx = ci * DN_HEADS + hh
                u_scr[c, hh] = u[idx]
                wq_scr[c, hh] = wq[idx]
                qkd_scr[c, hh] = qkd[idx]
                kd_scr[c, hh] = kd[idx]
            cd_scr[c] = jnp.concatenate(cd[ci * DN_HEADS:(ci + 1) * DN_HEADS], axis=0)

    def scan_stage(chunks):
        for c in chunks:
            cds = cd_scr[c]
            o, s_new = yield from _scan_chunk(
                [u_scr[c, hh] for hh in heads], [wq_scr[c, hh] for hh in heads],
                [qkd_scr[c, hh] for hh in heads], [kd_scr[c, hh] for hh in heads],
                [cds[hh:hh + 1, :] for hh in heads], [s_scr[hh] for hh in heads])
            for hh in heads:
                s_scr[hh] = s_new[hh]
                o_scr[c * CHUNK:(c + 1) * CHUNK, head_cols(0, hh)] = o[hh]

    def output_stage(r0, nr):
        rs = slice(r0, r0 + nr)
        unh = un_scr[rs, :]

        def gate_proj(col0):
            return [_dot(unh, win_ref[:, col0 + si * sub:col0 + (si + 1) * sub]) for si in range(n_sub)]

        z = gate_proj(4 * D_MODEL)
        yield
        y_dn = []
        for hh in heads:
            o = o_scr[rs, head_cols(0, hh)]
            o = o * lax.rsqrt(jnp.mean(o * o, axis=-1, keepdims=True) + NORM_EPS) * dnw_ref[...]
            per = sub // DN_HEAD_DIM
            zz = z[hh // per][:, (hh % per) * DN_HEAD_DIM:(hh % per + 1) * DN_HEAD_DIM]
            y_dn.append(o * (zz * _sigmoid(zz)))
        g_pool = gate_proj(5 * D_MODEL)
        yield
        g_dn = gate_proj(6 * D_MODEL)
        yield
        merged = []
        per = sub // DN_HEAD_DIM
        for si in range(n_sub):
            y_dn_s = jnp.concatenate(y_dn[si * per:(si + 1) * per], axis=1)
            merged.append((_sigmoid(g_pool[si]) * ypool_scr[rs, si * sub:(si + 1) * sub]
                           + _sigmoid(g_dn[si]) * y_dn_s).astype(BF16))
        h = x_ref[rs, :] + _dot(jnp.concatenate(merged, axis=1), wout_ref[...])
        h_ref[rs, :] = h
        yield
        xt = h * lax.rsqrt(jnp.mean(h * h, axis=-1, keepdims=True) + NORM_EPS) * nffn_ref[...]
        logits = lax.dot_general(wrt_ref[...], xt, (((1,), (1,)), ((), ())),
                                 preferred_element_type=F32, precision=lax.Precision.HIGHEST)
        yield
        logits = logits + brt_ref[...][:, 0:1]
        rid8 = lax.broadcasted_iota(jnp.int32, (SUBLANES, nr), 0)
        lg = jnp.where(rid8 < N_EXPERT_GROUPS, logits[0:SUBLANES, :], -jnp.inf)
        gmax = jnp.max(lg, axis=0, keepdims=True)
        g_idx = jnp.min(jnp.where(lg == gmax, rid8, SUBLANES), axis=0, keepdims=True)
        p_grp = 1.0 / jnp.sum(jnp.exp(lg - gmax), axis=0, keepdims=True)
        sel = jnp.zeros((EXPERTS_PER_GROUP, nr), F32)
        for gi in range(N_EXPERT_GROUPS):
            e0 = SUBLANES + gi * EXPERTS_PER_GROUP
            sel = jnp.where(g_idx == gi, logits[e0:e0 + EXPERTS_PER_GROUP, :], sel)
        m1 = jnp.max(sel, axis=0, keepdims=True)
        i1 = jnp.min(jnp.where(sel == m1, rid8, SUBLANES), axis=0, keepdims=True)
        sel2 = jnp.where(rid8 == i1, -jnp.inf, sel)
        m2 = jnp.max(sel2, axis=0, keepdims=True)
        i2 = jnp.min(jnp.where(sel2 == m2, rid8, SUBLANES), axis=0, keepdims=True)
        e21 = jnp.exp(m2 - m1)
        w1 = 1.0 / (1.0 + e21)
        c1 = p_grp * w1
        c2 = p_grp * (e21 * w1)
        id1 = g_idx * EXPERTS_PER_GROUP + i1
        id2 = g_idx * EXPERTS_PER_GROUP + i2
        rf_ref[:, rs] = jnp.where(rid8 == 0, c1, jnp.where(rid8 == 1, c2, 0.0))
        ri_ref[:, rs] = jnp.where(rid8 == 0, id1, jnp.where(rid8 == 1, id2, 0))

    group = 2 if n_chunks % 2 == 0 else 1
    groups = [list(range(g0, g0 + group)) for g0 in range(0, n_chunks, group)]
    _interleave(intra_stage(groups[0]))
    for gi in range(1, len(groups)):
        _interleave(intra_stage(groups[gi]), scan_stage(groups[gi - 1]))
    rows_before_last = groups[-1][0] * CHUNK
    if rows_before_last >= LANES:
        _interleave(scan_stage(groups[-1]), output_stage(0, rows_before_last))
        _interleave(output_stage(rows_before_last, t - rows_before_last))
    else:
        _interleave(scan_stage(groups[-1]))
        _interleave(output_stage(0, t))

    pool_buf[0:POOL_HALO, :] = pool_buf[t:t + POOL_HALO, :]
    conv_buf[0:CONV_HALO, :] = conv_buf[t:t + CONV_HALO, :]
    s_out_ref[...] = s_scr[...]
    ph_out_ref[...] = pool_buf[0:POOL_HALO, :]
    ch_out_ref[...] = conv_buf[0:CONV_HALO, :]


def _softplus(x):
    return jnp.maximum(x, 0.0) + jnp.log1p(jnp.exp(-jnp.abs(x)))


def _mixer(x2d, s0, ph0, ch0, params, *, batch, t_rows, pad_rows):
    n = x2d.shape[0]
    n_t = n // batch // t_rows
    t = t_rows
    row_blk = lambda b, j: (b * n_t + j, 0)
    col_blk = lambda b, j: (0, b * n_t + j)
    const2 = lambda b, j: (0, 0)
    const3 = lambda b, j: (0, 0, 0)
    (nmix, w_main, wab, wabt,
     convw, poolw, pscale, alog, dtb, alogt, dtbt, dnw, wout, nffn, wrt, brt) = params
    in_specs = [
        pl.BlockSpec((t, D_MODEL), row_blk),
        pl.BlockSpec((DN_HEADS, DN_HEAD_DIM, DN_HEAD_DIM), const3),
        pl.BlockSpec((POOL_HALO, D_MODEL), const2),
        pl.BlockSpec((CONV_HALO, 3 * D_MODEL), const2),
        pl.BlockSpec(nmix.shape, const2),
        pl.BlockSpec(w_main.shape, const2, pipeline_mode=pl.Buffered(1)),
        pl.BlockSpec(wab.shape, const2),
        pl.BlockSpec(wabt.shape, const2),
        pl.BlockSpec(convw.shape, const2),
        pl.BlockSpec(poolw.shape, const3),
        pl.BlockSpec(pscale.shape, const2),
        pl.BlockSpec(alog.shape, const2),
        pl.BlockSpec(dtb.shape, const2),
        pl.BlockSpec(alogt.shape, const2),
        pl.BlockSpec(dtbt.shape, const2),
        pl.BlockSpec(dnw.shape, const2),
        pl.BlockSpec(wout.shape, const2),
        pl.BlockSpec(nffn.shape, const2),
        pl.BlockSpec(wrt.shape, const2),
        pl.BlockSpec(brt.shape, const2),
    ]
    out_specs = [
        pl.BlockSpec((t, D_MODEL), row_blk),
        pl.BlockSpec((SUBLANES, t), col_blk),
        pl.BlockSpec((SUBLANES, t), col_blk),
        pl.BlockSpec((DN_HEADS, DN_HEAD_DIM, DN_HEAD_DIM), const3),
        pl.BlockSpec((POOL_HALO, D_MODEL), const2),
        pl.BlockSpec((CONV_HALO, 3 * D_MODEL), const2),
    ]
    out_shape = [
        jax.ShapeDtypeStruct((n, D_MODEL), F32),
        jax.ShapeDtypeStruct((SUBLANES, n), F32),
        jax.ShapeDtypeStruct((SUBLANES, n), jnp.int32),
        jax.ShapeDtypeStruct((DN_HEADS, DN_HEAD_DIM, DN_HEAD_DIM), F32),
        jax.ShapeDtypeStruct((POOL_HALO, D_MODEL), F32),
        jax.ShapeDtypeStruct((CONV_HALO, 3 * D_MODEL), F32),
    ]
    scratch = [
        pltpu.VMEM((DN_HEADS, DN_HEAD_DIM, DN_HEAD_DIM), F32),
        pltpu.VMEM((t + POOL_HALO, D_MODEL), F32),
        pltpu.VMEM((t + CONV_HALO, 3 * D_MODEL), F32),
        pltpu.VMEM((t, D_MODEL), BF16),
        pltpu.VMEM((t, D_MODEL), F32),
        pltpu.VMEM((t, 3 * D_MODEL), F32),
        pltpu.VMEM((t, D_MODEL), F32),
        pltpu.VMEM((t, LANES), F32),
        pltpu.VMEM((t, LANES), F32),
        pltpu.VMEM((t // CHUNK, 2 * DN_HEADS, CHUNK), F32),
        pltpu.VMEM((t // CHUNK, DN_HEADS, CHUNK, DN_HEAD_DIM), F32),
        pltpu.VMEM((t // CHUNK, DN_HEADS, 2 * CHUNK, DN_HEAD_DIM), BF16),
        pltpu.VMEM((t // CHUNK, DN_HEADS, CHUNK, CHUNK), BF16),
        pltpu.VMEM((t // CHUNK, DN_HEADS, CHUNK, DN_HEAD_DIM), BF16),
        pltpu.VMEM((t // CHUNK, DN_HEADS, DN_HEAD_DIM), F32),
    ]
    return pl.pallas_call(
        functools.partial(_mixer_kernel, pad_rows, t_rows),
        grid=(batch, n_t),
        in_specs=in_specs,
        out_specs=out_specs,
        out_shape=out_shape,
        scratch_shapes=scratch,
        compiler_params=pltpu.CompilerParams(
            dimension_semantics=("arbitrary", "arbitrary"), vmem_limit_bytes=VMEM_LIMIT),
        name="mixer_meta" if pad_rows else "mixer",
    )(x2d, s0, ph0, ch0, *params)


def _max_tiles(n_tokens):
    return 2 * n_tokens // MOE_TILE + N_EXPERTS


def _route_kernel(ri_ref, tri_ref, pos_ref, tinfo_ref, cnt_scr, base_scr):
    ph = pl.program_id(0)
    i = pl.program_id(1)
    rb = ri_ref.shape[1]
    eid = lax.broadcasted_iota(jnp.int32, (N_EXPERTS, rb), 0)
    oh1 = jnp.where(ri_ref[0:1, :] == eid, 1.0, 0.0).astype(F32)
    oh2 = jnp.where(ri_ref[1:2, :] == eid, 1.0, 0.0).astype(F32)
    ohs = oh1 + oh2

    @pl.when((ph == 0) & (i == 0))
    def _():
        cnt_scr[...] = jnp.zeros_like(cnt_scr)

    @pl.when(ph == 0)
    def _():
        cnt_scr[...] += ohs

    @pl.when((ph == 1) & (i == 0))
    def _():
        counts = jnp.sum(cnt_scr[...], axis=1, keepdims=True)
        padded = jnp.floor((counts + (MOE_TILE - 1)) * (1.0 / MOE_TILE)) * MOE_TILE
        padded_b = jnp.broadcast_to(padded, (N_EXPERTS, LANES))
        r = lax.broadcasted_iota(jnp.int32, (N_EXPERTS, N_EXPERTS), 0)
        c = lax.broadcasted_iota(jnp.int32, (N_EXPERTS, N_EXPERTS), 1)
        strict = jnp.where(r > c, 1.0, 0.0).astype(F32)
        offs = jnp.dot(strict, padded_b, preferred_element_type=F32,
                       precision=lax.Precision.HIGHEST)
        base_scr[...] = offs
        ends = offs[:, 0:1] + padded
        total = jnp.sum(padded, axis=0, keepdims=True)
        n_lanes = tinfo_ref.shape[1]
        tile_start = (lax.broadcasted_iota(jnp.int32, (N_EXPERTS, n_lanes), 1) * MOE_TILE).astype(F32)
        texp = jnp.sum(jnp.where(ends <= tile_start, 1.0, 0.0), axis=0, keepdims=True)
        last_active = jnp.sum(jnp.where(ends <= total - MOE_TILE, 1.0, 0.0), axis=0, keepdims=True)
        texp = jnp.minimum(texp, last_active)
        nact = jnp.broadcast_to(total * (1.0 / MOE_TILE), (1, n_lanes))
        rid = lax.broadcasted_iota(jnp.int32, tinfo_ref.shape, 0)
        tinfo_ref[...] = jnp.where(rid == 0, texp, jnp.where(rid == 1, nact, 0.0)).astype(jnp.int32)

    @pl.when(ph == 1)
    def _():
        incl = _dot(ohs.astype(BF16), tri_ref[...])
        slot = base_scr[:, 0:1] + incl - ohs
        pos1 = jnp.sum(oh1 * slot, axis=0, keepdims=True)
        pos2 = jnp.sum(oh2 * slot, axis=0, keepdims=True)
        rid = lax.broadcasted_iota(jnp.int32, pos_ref.shape, 0)
        pos_ref[...] = jnp.where(rid == 0, pos1, jnp.where(rid == 1, pos2, 0.0)).astype(jnp.int32)
        base_scr[...] += jnp.sum(ohs, axis=1, keepdims=True)


def _route(ri, tri, *, n_tile_lanes):
    n = ri.shape[1]
    rb = min(ROUTE_BLOCK, n)
    return pl.pallas_call(
        _route_kernel,
        grid=(2, n // rb),
        in_specs=[
            pl.BlockSpec((SUBLANES, rb), lambda p, i: (0, i)),
            pl.BlockSpec((rb, rb), lambda p, i: (0, 0)),
        ],
        out_specs=[
            pl.BlockSpec((SUBLANES, rb), lambda p, i: (0, i * p)),
            pl.BlockSpec((SUBLANES, n_tile_lanes), lambda p, i: (0, 0)),
        ],
        out_shape=[
            jax.ShapeDtypeStruct((SUBLANES, n), jnp.int32),
            jax.ShapeDtypeStruct((SUBLANES, n_tile_lanes), jnp.int32),
        ],
        scratch_shapes=[pltpu.VMEM((N_EXPERTS, rb), F32), pltpu.VMEM((N_EXPERTS, LANES), F32)],
        compiler_params=pltpu.CompilerParams(dimension_semantics=("arbitrary", "arbitrary")),
        name="route",
    )(ri, tri)


def _scatter_kernel(tb, pos_hbm, h_hbm, xs_in_hbm, xs_hbm, idx_a, idx_b, hbuf,
                    idx_sems, blk_sems, row_sems):
    del xs_in_hbm
    i = pl.program_id(0)
    n_steps = pl.num_programs(0)
    idx_bufs = (idx_a, idx_b)

    def idx_copy(step, par):
        return pltpu.make_async_copy(pos_hbm.at[step], idx_bufs[par], idx_sems.at[par])

    def blk_copy(step, par):
        return pltpu.make_async_copy(h_hbm.at[pl.ds(step * tb, tb)], hbuf.at[par], blk_sems.at[par])

    def row_copy(par, r, slot):
        return pltpu.make_async_copy(hbuf.at[par, pl.ds(r, 1)], xs_hbm.at[pl.ds(slot, 1)],
                                     row_sems.at[par])

    def drain_rows(par):
        def body(r, carry):
            for k in range(2):
                row_copy(par, 0, 0).wait()
            return carry
        lax.fori_loop(0, tb, body, 0, unroll=8)

    def step(par):
        @pl.when(i == 0)
        def _():
            idx_copy(0, 0).start()
            blk_copy(0, 0).start()

        @pl.when(i > 0)
        def _():
            drain_rows(1 - par)

        @pl.when(i + 1 < n_steps)
        def _():
            idx_copy(i + 1, 1 - par).start()
            blk_copy(i + 1, 1 - par).start()

        idx_copy(i, par).wait()
        blk_copy(i, par).wait()

        def issue(r, carry):
            for k in range(2):
                row_copy(par, r, idx_bufs[par][k * tb + r]).start(priority=k)
            return carry

        lax.fori_loop(0, tb, issue, 0, unroll=8)

        @pl.when(i == n_steps - 1)
        def _():
            drain_rows(par)

    @pl.when(i % 2 == 0)
    def _():
        step(0)

    @pl.when(i % 2 == 1)
    def _():
        step(1)


def _scatter(pos_blocks, h, xs_zero):
    tb = pos_blocks.shape[1] // 2
    return pl.pallas_call(
        functools.partial(_scatter_kernel, tb),
        grid=(pos_blocks.shape[0],),
        in_specs=[
            pl.BlockSpec(memory_space=pl.ANY),
            pl.BlockSpec(memory_space=pl.ANY),
            pl.BlockSpec(memory_space=pl.ANY),
        ],
        out_specs=pl.BlockSpec(memory_space=pl.ANY),
        out_shape=jax.ShapeDtypeStruct(xs_zero.shape, xs_zero.dtype),
        scratch_shapes=[pltpu.SMEM((2 * tb,), jnp.int32), pltpu.SMEM((2 * tb,), jnp.int32),
                        pltpu.VMEM((2, tb, D_MODEL), F32),
                        pltpu.SemaphoreType.DMA((2,)), pltpu.SemaphoreType.DMA((2,)),
                        pltpu.SemaphoreType.DMA((2,))],
        input_output_aliases={2: 0},
        compiler_params=pltpu.CompilerParams(dimension_semantics=("arbitrary",)),
        name="scatter_rows",
    )(pos_blocks, h, xs_zero)


def _experts_kernel(texp_ref, nact_ref, xs_ref, nffn_ref, wg_ref, wu_ref, wd_ref, ys_ref,
                    wg_b, wu_b, wd_b):
    i = pl.program_id(0)
    e = texp_ref[i]
    e_prev = texp_ref[jnp.maximum(i - 1, 0)]

    @pl.when((i == 0) | (e != e_prev))
    def _():
        wg_b[...] = wg_ref[0].astype(BF16)
        wu_b[...] = wu_ref[0].astype(BF16)
        wd_b[...] = wd_ref[0].astype(BF16)

    @pl.when(i < nact_ref[0])
    def _():
        hrow = xs_ref[...]
        x = (hrow * lax.rsqrt(jnp.mean(hrow * hrow, axis=-1, keepdims=True) + NORM_EPS)
             * nffn_ref[...]).astype(BF16)
        gate = _dot(x, wg_b[...])
        up = _dot(x, wu_b[...])
        hdn = (gate * _sigmoid(gate) * up).astype(BF16)
        ys_ref[...] = _dot(hdn, wd_b[...])


def _experts(texp, nact, xs, nffn, wg, wu, wd):
    n_tiles = xs.shape[0] // MOE_TILE
    tile_blk = lambda i, texp, nact: (jnp.minimum(i, nact[0] - 1), 0)
    w_blk = lambda i, texp, nact: (texp[i], 0, 0)
    grid_spec = pltpu.PrefetchScalarGridSpec(
        num_scalar_prefetch=2,
        grid=(n_tiles,),
        in_specs=[
            pl.BlockSpec((MOE_TILE, D_MODEL), tile_blk),
            pl.BlockSpec((1, D_MODEL), lambda i, texp, nact: (0, 0)),
            pl.BlockSpec((1, D_MODEL, D_FF_EXPERT), w_blk),
            pl.BlockSpec((1, D_MODEL, D_FF_EXPERT), w_blk),
            pl.BlockSpec((1, D_FF_EXPERT, D_MODEL), w_blk),
        ],
        out_specs=pl.BlockSpec((MOE_TILE, D_MODEL), tile_blk),
        scratch_shapes=[pltpu.VMEM((D_MODEL, D_FF_EXPERT), BF16),
                        pltpu.VMEM((D_MODEL, D_FF_EXPERT), BF16),
                        pltpu.VMEM((D_FF_EXPERT, D_MODEL), BF16)],
    )
    return pl.pallas_call(
        _experts_kernel,
        grid_spec=grid_spec,
        out_shape=jax.ShapeDtypeStruct(xs.shape, xs.dtype),
        input_output_aliases={2: 0},
        compiler_params=pltpu.CompilerParams(
            dimension_semantics=("arbitrary",), vmem_limit_bytes=VMEM_LIMIT),
        name="experts",
    )(texp, nact, xs, nffn, wg, wu, wd)


def _combine_kernel(pos_hbm, ys_hbm, h_ref, rf_ref, nfin_ref, out_ref,
                    idx_a, idx_b, ybuf, idx_sems, row_sems):
    i = pl.program_id(0)
    n_steps = pl.num_programs(0)
    tb = h_ref.shape[0]
    idx_bufs = (idx_a, idx_b)

    def idx_copy(step, par):
        return pltpu.make_async_copy(pos_hbm.at[step], idx_bufs[par], idx_sems.at[par])

    def issue_rows(par):
        def body(g, carry):
            for sub in range(SUBLANES):
                r = g * SUBLANES + sub
                for k in range(2):
                    pltpu.make_async_copy(
                        ys_hbm.at[pl.ds(idx_bufs[par][k * tb + r], 1)],
                        ybuf.at[par, k, g, pl.ds(sub, 1)],
                        row_sems.at[par]).start(priority=k)
            return carry
        lax.fori_loop(0, tb // SUBLANES, body, 0)

    def drain_rows(par):
        def body(r, carry):
            for k in range(2):
                pltpu.make_async_copy(ys_hbm.at[pl.ds(0, 1)], ybuf.at[par, k, 0, pl.ds(0, 1)],
                                      row_sems.at[par]).wait()
            return carry
        lax.fori_loop(0, tb, body, 0, unroll=8)

    def step(par):
        @pl.when(i == 0)
        def _():
            idx_copy(0, 0).start()
            idx_copy(0, 0).wait()
            issue_rows(0)

            @pl.when(n_steps > 1)
            def _():
                idx_copy(1, 1).start()

        @pl.when(i + 1 < n_steps)
        def _():
            idx_copy(i + 1, 1 - par).wait()
            issue_rows(1 - par)

        @pl.when(i + 2 < n_steps)
        def _():
            idx_copy(i + 2, par).start()

        drain_rows(par)
        rf = rf_ref[...]
        rf_cols = jnp.transpose(jnp.concatenate(
            [rf, jnp.zeros((LANES - SUBLANES, tb), F32)], axis=0))
        y1 = ybuf[par, 0].reshape(tb, D_MODEL)
        y2 = ybuf[par, 1].reshape(tb, D_MODEL)
        hh = h_ref[...] + rf_cols[:, 0:1] * y1 + rf_cols[:, 1:2] * y2
        out_ref[...] = (hh * lax.rsqrt(jnp.mean(hh * hh, axis=-1, keepdims=True) + NORM_EPS)
                        * nfin_ref[...])

    @pl.when(i % 2 == 0)
    def _():
        step(0)

    @pl.when(i % 2 == 1)
    def _():
        step(1)


def _combine(pos_blocks, ys, h, rf, nfin):
    n = h.shape[0]
    tb = pos_blocks.shape[1] // 2
    return pl.pallas_call(
        _combine_kernel,
        grid=(n // tb,),
        in_specs=[
            pl.BlockSpec(memory_space=pl.ANY),
            pl.BlockSpec(memory_space=pl.ANY),
            pl.BlockSpec((tb, D_MODEL), lambda i: (i, 0)),
            pl.BlockSpec((SUBLANES, tb), lambda i: (0, i)),
            pl.BlockSpec((1, D_MODEL), lambda i: (0, 0)),
        ],
        out_specs=pl.BlockSpec((tb, D_MODEL), lambda i: (i, 0)),
        out_shape=jax.ShapeDtypeStruct((n, D_MODEL), F32),
        scratch_shapes=[pltpu.SMEM((2 * tb,), jnp.int32), pltpu.SMEM((2 * tb,), jnp.int32),
                        pltpu.VMEM((2, 2, tb // SUBLANES, SUBLANES, D_MODEL), F32),
                        pltpu.SemaphoreType.DMA((2,)), pltpu.SemaphoreType.DMA((2,))],
        compiler_params=pltpu.CompilerParams(dimension_semantics=("arbitrary",)),
        name="combine",
    )(pos_blocks, ys, h, rf, nfin)


def _moe(h, rf, ri, nffn, wg, wu, wd, nfin):
    n = h.shape[0]
    max_tiles = _max_tiles(n)
    n_tile_lanes = -(-max_tiles // LANES) * LANES
    rb = min(ROUTE_BLOCK, n)
    tri = jnp.triu(jnp.ones((rb, rb), BF16))
    pos, tinfo = _route(ri, tri, n_tile_lanes=n_tile_lanes)
    texp = tinfo[0, :max_tiles]
    nact = tinfo[1, :1]
    tb = min(ROW_BLOCK, n)
    pos_blocks = pos[0:2].reshape(2, n // tb, tb).transpose(1, 0, 2).reshape(n // tb, 2 * tb)
    xs_zero = jnp.zeros((max_tiles * MOE_TILE, D_MODEL), F32)
    xs = _scatter(pos_blocks, h, xs_zero)
    ys = _experts(texp, nact, xs, nffn, wg, wu, wd)
    return _combine(pos_blocks, ys, h, rf, nfin)


def _pad_lanes_row(v):
    return jnp.pad(v.astype(F32), (0, LANES - v.shape[0]))[None, :]


def _block_forward(x, meta_tokens, norm_mix_w, w_in, conv_w, pool_w, pool_scale, a_log, dt_bias,
                   dn_norm_w, w_out, norm_ffn_w, router_group_w, router_group_b, router_expert_w,
                   router_expert_b, expert_w_gate, expert_w_up, expert_w_down, norm_final_w,
                   *, mixer_rows):
    bsz, seq, _ = x.shape
    n = bsz * seq
    x2d = x.reshape(n, D_MODEL)

    ab0 = 5 * D_MODEL
    w_main = jnp.concatenate([w_in[:, :ab0], w_in[:, ab0 + 2 * DN_HEADS:]], axis=1).astype(BF16)
    wab = jnp.pad(w_in[:, ab0:ab0 + 2 * DN_HEADS], ((0, 0), (0, LANES - 2 * DN_HEADS))).astype(BF16)
    wabt = wab.T
    nmix = norm_mix_w[None, :]
    alog = _pad_lanes_row(a_log)
    dtb = _pad_lanes_row(dt_bias)
    alogt = jnp.broadcast_to(alog.T, (LANES, LANES))
    dtbt = jnp.broadcast_to(dtb.T, (LANES, LANES))
    wr = jnp.zeros((D_MODEL, LANES), F32)
    wr = wr.at[:, 0:N_EXPERT_GROUPS].set(router_group_w)
    wr = wr.at[:, SUBLANES:SUBLANES + N_EXPERTS].set(router_expert_w)
    br = jnp.zeros((LANES,), F32)
    br = br.at[0:N_EXPERT_GROUPS].set(router_group_b)
    br = br.at[SUBLANES:SUBLANES + N_EXPERTS].set(router_expert_b)
    params = (nmix, w_main, wab, wabt,
              conv_w, pool_w.astype(BF16), pool_scale[None, :], alog, dtb, alogt, dtbt,
              dn_norm_w[None, :], w_out.astype(BF16), norm_ffn_w[None, :],
              wr.T, jnp.broadcast_to(br[:, None], (LANES, LANES)))

    pad_rows = CHUNK - N_META
    xm = jnp.concatenate([jnp.zeros((pad_rows, D_MODEL), F32), meta_tokens], axis=0)
    zeros_s = jnp.zeros((DN_HEADS, DN_HEAD_DIM, DN_HEAD_DIM), F32)
    zeros_ph = jnp.zeros((POOL_HALO, D_MODEL), F32)
    zeros_ch = jnp.zeros((CONV_HALO, 3 * D_MODEL), F32)
    meta_out = _mixer(xm, zeros_s, zeros_ph, zeros_ch, params,
                      batch=1, t_rows=CHUNK, pad_rows=pad_rows)
    s_meta, ph_meta, ch_meta = meta_out[3], meta_out[4], meta_out[5]

    h, rf, ri, _, _, _ = _mixer(x2d, s_meta, ph_meta, ch_meta, params,
                                batch=bsz, t_rows=mixer_rows, pad_rows=0)
    wg = expert_w_gate.reshape(N_EXPERTS, D_MODEL, D_FF_EXPERT)
    wu = expert_w_up.reshape(N_EXPERTS, D_MODEL, D_FF_EXPERT)
    wd = expert_w_down.reshape(N_EXPERTS, D_FF_EXPERT, D_MODEL)
    out = _moe(h, rf, ri, norm_ffn_w[None, :], wg, wu, wd, norm_final_w[None, :])
    return out.reshape(bsz, seq, D_MODEL)


def kernel(x, meta_tokens, norm_mix_w, w_in, conv_w, pool_w, pool_scale, a_log, dt_bias, dn_norm_w, w_out, norm_ffn_w, router_group_w, router_group_b, router_expert_w, router_expert_b, expert_w_gate, expert_w_up, expert_w_down, norm_final_w):
    assert norm_mix_w.shape[0] == 1, "single-layer block"
    seq = x.shape[1]
    return _block_forward(
        x, meta_tokens, norm_mix_w[0], w_in[0], conv_w[0], pool_w[0], pool_scale[0], a_log[0],
        dt_bias[0], dn_norm_w[0], w_out[0], norm_ffn_w[0], router_group_w[0], router_group_b[0],
        router_expert_w[0], router_expert_b[0], expert_w_gate[0], expert_w_up[0], expert_w_down[0],
        norm_final_w,
        mixer_rows=min(256, seq))
```

```python
import functools
import math

import jax
import jax.numpy as jnp
from jax import lax
from jax.experimental import pallas as pl
from jax.experimental.pallas import tpu as pltpu

F32 = jnp.float32
BF16 = jnp.bfloat16

D_MODEL = 1024
N_META = 16
POOL_WINDOWS = (2, 4, 8, 16)
POOL_GROUP_DIM = 256
DN_HEADS = 8
DN_HEAD_DIM = 128
CONV_WIDTH = 4
CHUNK = 64
N_EXPERT_GROUPS = 4
EXPERTS_PER_GROUP = 8
N_EXPERTS = 32
D_FF_EXPERT = 512
NORM_EPS = 1e-6

LANES = 128
SUBLANES = 8
P_MAIN_COLS = 7 * D_MODEL
POOL_HALO = 16
CONV_HALO = 8
VMEM_LIMIT = 56 * 1024 * 1024
MOE_TILE = 256
ROUTE_BLOCK = 512
ROW_BLOCK = 256


def _dot(a, b):
    return jnp.dot(a, b, preferred_element_type=F32)


def _dot_nt(a, b):
    return lax.dot_general(a, b, (((1,), (1,)), ((), ())), preferred_element_type=F32)


def _dot_tn(a, b):
    return lax.dot_general(a, b, (((0,), (0,)), ((), ())), preferred_element_type=F32)


def _sigmoid(x):
    return 1.0 / (1.0 + jnp.exp(-x))


def _interleave(*stage_generators):
    live = {i: g for i, g in enumerate(stage_generators)}
    results = [None] * len(stage_generators)
    while live:
        for i in list(live):
            try:
                next(live[i])
            except StopIteration as done:
                results[i] = done.value
                del live[i]
    return results


def _intra_chunk(qs, ks, vs, bcols, gcols, grows):
    c = qs[0].shape[0]
    hs = range(len(qs))
    ii = lax.broadcasted_iota(jnp.int32, (c, c), 0)
    jj = lax.broadcasted_iota(jnp.int32, (c, c), 1)
    dec =[jnp.exp(jnp.where(ii >= jj, gcols[h] - grows[h], -jnp.inf)) for h in hs]
    kb = [ks[h].astype(BF16) for h in hs]
    qkb = [jnp.concatenate([qs[h].astype(BF16), kb[h]], axis=0) for h in hs]
    qkk = [_dot_nt(qkb[h], kb[h]) for h in hs]
    yield
    egc = [jnp.exp(gcols[h]) for h in hs]
    pw = [jnp.where(ii > jj, -(bcols[h] * qkk[h][c:] * dec[h]), 0.0) for h in hs]
    sol = [jnp.concatenate([vs[h] * bcols[h], ks[h] * (bcols[h] * egc[h])], axis=1) for h in hs]
    width = 2 * DN_HEAD_DIM
    levels = int(math.log2(c))
    for lvl in range(levels):
        pb = [pw[h].astype(BF16) for h in hs]
        if lvl < levels - 1:
            r = [_dot(pb[h], jnp.concatenate([sol[h].astype(BF16), pb[h]], axis=1)) for h in hs]
            sol = [sol[h] + r[h][:, :width] for h in hs]
            pw = [r[h][:, width:] for h in hs]
        else:
            sol = [sol[h] + _dot(pb[h], sol[h].astype(BF16)) for h in hs]
        yield
    qd = [qs[h] * egc[h] for h in hs]
    glast = [gcols[h][c - 1:c, :] for h in hs]
    kd = [(ks[h] * jnp.exp(glast[h] - gcols[h])).astype(BF16) for h in hs]
    u = [sol[h][:, :DN_HEAD_DIM] for h in hs]
    wq = [jnp.concatenate([sol[h][:, DN_HEAD_DIM:], qd[h]], axis=0).astype(BF16) for h in hs]
    qkd = [(qkk[h][:c] * dec[h]).astype(BF16) for h in hs]
    cd = [jnp.broadcast_to(jnp.exp(glast[h]), (1, DN_HEAD_DIM)) for h in hs]
    return u, wq, qkd, kd, cd


def _scan_chunk(u, wq, qkd, kd, cd, s):
    c = u[0].shape[0]
    hs = range(len(u))
    sb = [s[h].astype(BF16) for h in hs]
    ws = [_dot(wq[h], sb[h]) for h in hs]
    yield
    vb = [(u[h] - ws[h][:c]).astype(BF16) for h in hs]
    o = [ws[h][c:] + _dot(qkd[h], vb[h]) for h in hs]
    s_new = [s[h] * cd[h] + _dot_tn(kd[h], vb[h]) for h in hs]
    yield
    return o, s_new


def _mixer_kernel(pad_rows, t_rows,
                  x_ref, s0_ref, ph0_ref, ch0_ref, nmix_ref, win_ref, wab_ref, wabt_ref,
                  convw_ref, poolw_ref, pscale_ref, alog_ref, dtb_ref, alogt_ref, dtbt_ref,
                  dnw_ref, wout_ref, nffn_ref, wrt_ref, brt_ref,
                  h_ref, rf_ref, ri_ref, s_out_ref, ph_out_ref, ch_out_ref,
                  s_scr, pool_buf, conv_buf, un_scr, ypool_scr, qkv_scr, o_scr, beta_scr, gcol_scr, grow_scr,
                  u_scr, wq_scr, qkd_scr, kd_scr, cd_scr):
    t = t_rows
    j = pl.program_id(1)
    n_chunks = t // CHUNK

    @pl.when(j == 0)
    def _():
        s_scr[...] = s0_ref[...]
        pool_buf[0:POOL_HALO, :] = ph0_ref[...]
        conv_buf[0:CONV_HALO, :] = ch0_ref[...]

    x = x_ref[...]
    un = (x * lax.rsqrt(jnp.mean(x * x, axis=-1, keepdims=True) + NORM_EPS) * nmix_ref[...]).astype(BF16)

    un_scr[...] = un
    sub = POOL_GROUP_DIM
    n_sub = D_MODEL // sub

    def project(col0):
        return _dot(un, win_ref[:, col0:col0 + sub])

    def pool_group(gi):
        win = POOL_WINDOWS[gi]
        cs = slice(gi * sub, (gi + 1) * sub)
        acc = pool_buf[:, cs]
        shift = 1
        while shift < win:
            acc = acc + pltpu.roll(acc, shift, axis=0)
            shift *= 2
        pooled = acc[POOL_HALO:, :] * (1.0 / win) - pool_buf[POOL_HALO:POOL_HALO + t, cs]
        ypool_scr[:, cs] = _dot(pooled.astype(BF16), poolw_ref[gi]) * pscale_ref[:, cs]

    def conv_sub(col0):
        cs = slice(col0, col0 + sub)
        acc = convw_ref[CONV_WIDTH - 1:CONV_WIDTH, cs] * conv_buf[CONV_HALO:CONV_HALO + t, cs]
        for kk in range(CONV_WIDTH - 1):
            off = CONV_HALO - (CONV_WIDTH - 1) + kk
            acc = acc + convw_ref[kk:kk + 1, cs] * conv_buf[off:off + t, cs]
        act = acc * _sigmoid(acc)
        if col0 >= 2 * D_MODEL:
            qkv_scr[:, cs] = act
            return
        for hh in range(sub // DN_HEAD_DIM):
            part = act[:, hh * DN_HEAD_DIM:(hh + 1) * DN_HEAD_DIM]
            nrm = lax.rsqrt(jnp.sum(part * part, axis=-1, keepdims=True) + NORM_EPS)
            if col0 < D_MODEL:
                nrm = nrm * (DN_HEAD_DIM ** -0.5)
            qkv_scr[:, col0 + hh * DN_HEAD_DIM:col0 + (hh + 1) * DN_HEAD_DIM] = part * nrm

    for si in range(n_sub):
        pool_buf[POOL_HALO:POOL_HALO + t, si * sub:(si + 1) * sub] = project(si * sub)
    for si in range(n_sub):
        conv_buf[CONV_HALO:CONV_HALO + t, si * sub:(si + 1) * sub] = project(D_MODEL + si * sub)
        pool_group(si)
    for blk in range(1, 3):
        for si in range(n_sub):
            c0 = blk * D_MODEL + si * sub
            conv_buf[CONV_HALO:CONV_HALO + t, c0:c0 + sub] = project(D_MODEL + c0)
            conv_sub(c0 - D_MODEL)
    pab = _dot(un, wab_ref[...])
    pabt = _dot_nt(wabt_ref[...], un)

    gcol_all = -jnp.exp(alog_ref[...]) * _softplus(pab + dtb_ref[...])
    beta_all = _sigmoid(pab)
    ab_rows = 2 * DN_HEADS
    grow_all = (-jnp.exp(alogt_ref[0:ab_rows, 0:1])
                * _softplus(pabt[0:ab_rows, :] + dtbt_ref[0:ab_rows, 0:1]))
    if pad_rows:
        rid = lax.broadcasted_iota(jnp.int32, (t, LANES), 0)
        gcol_all = jnp.where(rid >= pad_rows, gcol_all, 0.0)
        beta_all = jnp.where(rid >= pad_rows, beta_all, 0.0)
        cid = lax.broadcasted_iota(jnp.int32, (ab_rows, t), 1)
        grow_all = jnp.where(cid >= pad_rows, grow_all, 0.0)
    beta_scr[...] = beta_all
    ci = lax.broadcasted_iota(jnp.int32, (CHUNK, CHUNK), 0)
    cj = lax.broadcasted_iota(jnp.int32, (CHUNK, CHUNK), 1)
    tri_l = jnp.where(ci >= cj, 1.0, 0.0).astype(F32)
    tri_u = jnp.where(ci <= cj, 1.0, 0.0).astype(F32)
    for c in range(n_chunks):
        rs = slice(c * CHUNK, (c + 1) * CHUNK)
        gcol_scr[rs, :] = jnp.dot(tri_l, gcol_all[rs, :], preferred_element_type=F32,
                                  precision=lax.Precision.HIGHEST)
        grow_scr[c] = jnp.dot(grow_all[:, rs], tri_u, preferred_element_type=F32,
                              precision=lax.Precision.HIGHEST)
        if c < n_sub:
            conv_sub(2 * D_MODEL + c * sub)
    for si in range(n_chunks, n_sub):
        conv_sub(2 * D_MODEL + si * sub)

    heads = range(DN_HEADS)

    def head_cols(base, hh):
        return slice(base + hh * DN_HEAD_DIM, base + (hh + 1) * DN_HEAD_DIM)

    def intra_stage(chunks):
        qs, ks, vs, bcols, gcols, grows = [], [], [], [], [], []
        for c in chunks:
            rs = slice(c * CHUNK, (c + 1) * CHUNK)
            gcol_c = gcol_scr[rs, :]
            beta_c = beta_scr[rs, :]
            grow_c = grow_scr[c]
            for hh in heads:
                qs.append(qkv_scr[rs, head_cols(0, hh)])
                ks.append(qkv_scr[rs, head_cols(D_MODEL, hh)])
                vs.append(qkv_scr[rs, head_cols(2 * D_MODEL, hh)])
                bcols.append(beta_c[:, DN_HEADS + hh:DN_HEADS + hh + 1])
                gcols.append(gcol_c[:, hh:hh + 1])
                grows.append(grow_c[hh:hh + 1, :])
        u, wq, qkd, kd, cd = yield from _intra_chunk(qs, ks, vs, bcols, gcols, grows)
        for ci, c in enumerate(chunks):
            for hh in heads:
                idx = ci * DN_HEADS + hh
                u_scr[c, hh] = u[idx]
                wq_scr[c, hh] = wq[idx]
                qkd_scr[c, hh] = qkd[idx]
                kd_scr[c, hh] = kd[idx]
            cd_scr[c] = jnp.concatenate(cd[ci * DN_HEADS:(ci + 1) * DN_HEADS], axis=0)

    def scan_stage(chunks):
        for c in chunks:
            cds = cd_scr[c]
            o, s_new = yield from _scan_chunk(
                [u_scr[c, hh] for hh in heads], [wq_scr[c, hh] for hh in heads],
                [qkd_scr[c, hh] for hh in heads], [kd_scr[c, hh] for hh in heads],
                [cds[hh:hh + 1, :] for hh in heads], [s_scr[hh] for hh in heads])
            for hh in heads:
                s_scr[hh] = s_new[hh]
                o_scr[c * CHUNK:(c + 1) * CHUNK, head_cols(0, hh)] = o[hh]

    def output_stage(r0, nr):
        rs = slice(r0, r0 + nr)
        unh = un_scr[rs, :]

        def gate_proj(col0):
            return [_dot(unh, win_ref[:, col0 + si * sub:col0 + (si + 1) * sub]) for si in range(n_sub)]

        z = gate_proj(4 * D_MODEL)
        yield
        y_dn = []
        for hh in heads:
            o = o_scr[rs, head_cols(0, hh)]
            o = o * lax.rsqrt(jnp.mean(o * o, axis=-1, keepdims=True) + NORM_EPS) * dnw_ref[...]
            per = sub // DN_HEAD_DIM
            zz = z[hh // per][:, (hh % per) * DN_HEAD_DIM:(hh % per + 1) * DN_HEAD_DIM]
            y_dn.append(o * (zz * _sigmoid(zz)))
        g_pool = gate_proj(5 * D_MODEL)
        yield
        g_dn = gate_proj(6 * D_MODEL)
        yield
        merged = []
        per = sub // DN_HEAD_DIM
        for si in range(n_sub):
            y_dn_s = jnp.concatenate(y_dn[si * per:(si + 1) * per], axis=1)
            merged.append((_sigmoid(g_pool[si]) * ypool_scr[rs, si * sub:(si + 1) * sub]
                           + _sigmoid(g_dn[si]) * y_dn_s).astype(BF16))
        h = x_ref[rs, :] + _dot(jnp.concatenate(merged, axis=1), wout_ref[...])
        h_ref[rs, :] = h
        yield
        xt = h * lax.rsqrt(jnp.mean(h * h, axis=-1, keepdims=True) + NORM_EPS) * nffn_ref[...]
        logits = lax.dot_general(wrt_ref[...], xt, (((1,), (1,)), ((), ())),
                                 preferred_element_type=F32, precision=lax.Precision.HIGHEST)
        yield
        logits = logits + brt_ref[...][:, 0:1]
        rid8 = lax.broadcasted_iota(jnp.int32, (SUBLANES, nr), 0)
        lg = jnp.where(rid8 < N_EXPERT_GROUPS, logits[0:SUBLANES, :], -jnp.inf)
        gmax = jnp.max(lg, axis=0, keepdims=True)
        g_idx = jnp.min(jnp.where(lg == gmax, rid8, SUBLANES), axis=0, keepdims=True)
        p_grp = 1.0 / jnp.sum(jnp.exp(lg - gmax), axis=0, keepdims=True)
        sel = jnp.zeros((EXPERTS_PER_GROUP, nr), F32)
        for gi in range(N_EXPERT_GROUPS):
            e0 = SUBLANES + gi * EXPERTS_PER_GROUP
            sel = jnp.where(g_idx == gi, logits[e0:e0 + EXPERTS_PER_GROUP, :], sel)
        m1 = jnp.max(sel, axis=0, keepdims=True)
        i1 = jnp.min(jnp.where(sel == m1, rid8, SUBLANES), axis=0, keepdims=True)
        sel2 = jnp.where(rid8 == i1, -jnp.inf, sel)
        m2 = jnp.max(sel2, axis=0, keepdims=True)
        i2 = jnp.min(jnp.where(sel2 == m2, rid8, SUBLANES), axis=0, keepdims=True)
        e21 = jnp.exp(m2 - m1)
        w1 = 1.0 / (1.0 + e21)
        c1 = p_grp * w1
        c2 = p_grp * (e21 * w1)
        id1 = g_idx * EXPERTS_PER_GROUP + i1
        id2 = g_idx * EXPERTS_PER_GROUP + i2
        rf_ref[:, rs] = jnp.where(rid8 == 0, c1, jnp.where(rid8 == 1, c2, 0.0))
        ri_ref[:, rs] = jnp.where(rid8 == 0, id1, jnp.where(rid8 == 1, id2, 0))

    group = 2 if n_chunks % 2 == 0 else 1
    groups = [list(range(g0, g0 + group)) for g0 in range(0, n_chunks, group)]
    _interleave(intra_stage(groups[0]))
    for gi in range(1, len(groups)):
        _interleave(intra_stage(groups[gi]), scan_stage(groups[gi - 1]))
    rows_before_last = groups[-1][0] * CHUNK
    if rows_before_last >= LANES:
        _interleave(scan_stage(groups[-1]), output_stage(0, rows_before_last))
        _interleave(output_stage(rows_before_last, t - rows_before_last))
    else:
        _interleave(scan_stage(groups[-1]))
        _interleave(output_stage(0, t))

    pool_buf[0:POOL_HALO, :] = pool_buf[t:t + POOL_HALO, :]
    conv_buf[0:CONV_HALO, :] = conv_buf[t:t + CONV_HALO, :]
    s_out_ref[...] = s_scr[...]
    ph_out_ref[...] = pool_buf[0:POOL_HALO, :]
    ch_out_ref[...] = conv_buf[0:CONV_HALO, :]


def _softplus(x):
    return jnp.maximum(x, 0.0) + jnp.log1p(jnp.exp(-jnp.abs(x)))


def _mixer(x2d, s0, ph0, ch0, params, *, batch, t_rows, pad_rows):
    n = x2d.shape[0]
    n_t = n // batch // t_rows
    t = t_rows
    row_blk = lambda b, j: (b * n_t + j, 0)
    col_blk = lambda b, j: (0, b * n_t + j)
    const2 = lambda b, j: (0, 0)
    const3 = lambda b, j: (0, 0, 0)
    (nmix, w_main, wab, wabt,
     convw, poolw, pscale, alog, dtb, alogt, dtbt, dnw, wout, nffn, wrt, brt) = params
    in_specs = [
        pl.BlockSpec((t, D_MODEL), row_blk),
        pl.BlockSpec((DN_HEADS, DN_HEAD_DIM, DN_HEAD_DIM), const3),
        pl.BlockSpec((POOL_HALO, D_MODEL), const2),
        pl.BlockSpec((CONV_HALO, 3 * D_MODEL), const2),
        pl.BlockSpec(nmix.shape, const2),
        pl.BlockSpec(w_main.shape, const2, pipeline_mode=pl.Buffered(1)),
        pl.BlockSpec(wab.shape, const2),
        pl.BlockSpec(wabt.shape, const2),
        pl.BlockSpec(convw.shape, const2),
        pl.BlockSpec(poolw.shape, const3),
        pl.BlockSpec(pscale.shape, const2),
        pl.BlockSpec(alog.shape, const2),
        pl.BlockSpec(dtb.shape, const2),
        pl.BlockSpec(alogt.shape, const2),
        pl.BlockSpec(dtbt.shape, const2),
        pl.BlockSpec(dnw.shape, const2),
        pl.BlockSpec(wout.shape, const2),
        pl.BlockSpec(nffn.shape, const2),
        pl.BlockSpec(wrt.shape, const2),
        pl.BlockSpec(brt.shape, const2),
    ]
    out_specs = [
        pl.BlockSpec((t, D_MODEL), row_blk),
        pl.BlockSpec((SUBLANES, t), col_blk),
        pl.BlockSpec((SUBLANES, t), col_blk),
        pl.BlockSpec((DN_HEADS, DN_HEAD_DIM, DN_HEAD_DIM), const3),
        pl.BlockSpec((POOL_HALO, D_MODEL), const2),
        pl.BlockSpec((CONV_HALO, 3 * D_MODEL), const2),
    ]
    out_shape = [
        jax.ShapeDtypeStruct((n, D_MODEL), F32),
        jax.ShapeDtypeStruct((SUBLANES, n), F32),
        jax.ShapeDtypeStruct((SUBLANES, n), jnp.int32),
        jax.ShapeDtypeStruct((DN_HEADS, DN_HEAD_DIM, DN_HEAD_DIM), F32),
        jax.ShapeDtypeStruct((POOL_HALO, D_MODEL), F32),
        jax.ShapeDtypeStruct((CONV_HALO, 3 * D_MODEL), F32),
    ]
    scratch = [
        pltpu.VMEM((DN_HEADS, DN_HEAD_DIM, DN_HEAD_DIM), F32),
        pltpu.VMEM((t + POOL_HALO, D_MODEL), F32),
        pltpu.VMEM((t + CONV_HALO, 3 * D_MODEL), F32),
        pltpu.VMEM((t, D_MODEL), BF16),
        pltpu.VMEM((t, D_MODEL), F32),
        pltpu.VMEM((t, 3 * D_MODEL), F32),
        pltpu.VMEM((t, D_MODEL), F32),
        pltpu.VMEM((t, LANES), F32),
        pltpu.VMEM((t, LANES), F32),
        pltpu.VMEM((t // CHUNK, 2 * DN_HEADS, CHUNK), F32),
        pltpu.VMEM((t // CHUNK, DN_HEADS, CHUNK, DN_HEAD_DIM), F32),
        pltpu.VMEM((t // CHUNK, DN_HEADS, 2 * CHUNK, DN_HEAD_DIM), BF16),
        pltpu.VMEM((t // CHUNK, DN_HEADS, CHUNK, CHUNK), BF16),
        pltpu.VMEM((t // CHUNK, DN_HEADS, CHUNK, DN_HEAD_DIM), BF16),
        pltpu.VMEM((t // CHUNK, DN_HEADS, DN_HEAD_DIM), F32),
    ]
    return pl.pallas_call(
        functools.partial(_mixer_kernel, pad_rows, t_rows),
        grid=(batch, n_t),
        in_specs=in_specs,
        out_specs=out_specs,
        out_shape=out_shape,
        scratch_shapes=scratch,
        compiler_params=pltpu.CompilerParams(
            dimension_semantics=("arbitrary", "arbitrary"), vmem_limit_bytes=VMEM_LIMIT),
        name="mixer_meta" if pad_rows else "mixer",
    )(x2d, s0, ph0, ch0, *params)


def _max_tiles(n_tokens):
    return 2 * n_tokens // MOE_TILE + N_EXPERTS


def _route_kernel(ri_ref, tri_ref, pos_ref, tinfo_ref, cnt_scr, base_scr):
    ph = pl.program_id(0)
    i = pl.program_id(1)
    rb = ri_ref.shape[1]
    eid = lax.broadcasted_iota(jnp.int32, (N_EXPERTS, rb), 0)
    oh1 = jnp.where(ri_ref[0:1, :] == eid, 1.0, 0.0).astype(F32)
    oh2 = jnp.where(ri_ref[1:2, :] == eid, 1.0, 0.0).astype(F32)
    ohs = oh1 + oh2

    @pl.when((ph == 0) & (i == 0))
    def _():
        cnt_scr[...] = jnp.zeros_like(cnt_scr)

    @pl.when(ph == 0)
    def _():
        cnt_scr[...] += ohs

    @pl.when((ph == 1) & (i == 0))
    def _():
        counts = jnp.sum(cnt_scr[...], axis=1, keepdims=True)
        padded = jnp.floor((counts + (MOE_TILE - 1)) * (1.0 / MOE_TILE)) * MOE_TILE
        padded_b = jnp.broadcast_to(padded, (N_EXPERTS, LANES))
        r = lax.broadcasted_iota(jnp.int32, (N_EXPERTS, N_EXPERTS), 0)
        c = lax.broadcasted_iota(jnp.int32, (N_EXPERTS, N_EXPERTS), 1)
        strict = jnp.where(r > c, 1.0, 0.0).astype(F32)
        offs = jnp.dot(strict, padded_b, preferred_element_type=F32,
                       precision=lax.Precision.HIGHEST)
        base_scr[...] = offs
        ends = offs[:, 0:1] + padded
        total = jnp.sum(padded, axis=0, keepdims=True)
        n_lanes = tinfo_ref.shape[1]
        tile_start = (lax.broadcasted_iota(jnp.int32, (N_EXPERTS, n_lanes), 1) * MOE_TILE).astype(F32)
        texp = jnp.sum(jnp.where(ends <= tile_start, 1.0, 0.0), axis=0, keepdims=True)
        last_active = jnp.sum(jnp.where(ends <= total - MOE_TILE, 1.0, 0.0), axis=0, keepdims=True)
        texp = jnp.minimum(texp, last_active)
        nact = jnp.broadcast_to(total * (1.0 / MOE_TILE), (1, n_lanes))
        rid = lax.broadcasted_iota(jnp.int32, tinfo_ref.shape, 0)
        tinfo_ref[...] = jnp.where(rid == 0, texp, jnp.where(rid == 1, nact, 0.0)).astype(jnp.int32)

    @pl.when(ph == 1)
    def _():
        incl = _dot(ohs.astype(BF16), tri_ref[...])
        slot = base_scr[:, 0:1] + incl - ohs
        pos1 = jnp.sum(oh1 * slot, axis=0, keepdims=True)
        pos2 = jnp.sum(oh2 * slot, axis=0, keepdims=True)
        rid = lax.broadcasted_iota(jnp.int32, pos_ref.shape, 0)
        pos_ref[...] = jnp.where(rid == 0, pos1, jnp.where(rid == 1, pos2, 0.0)).astype(jnp.int32)
        base_scr[...] += jnp.sum(ohs, axis=1, keepdims=True)


def _route(ri, tri, *, n_tile_lanes):
    n = ri.shape[1]
    rb = min(ROUTE_BLOCK, n)
    return pl.pallas_call(
        _route_kernel,
        grid=(2, n // rb),
        in_specs=[
            pl.BlockSpec((SUBLANES, rb), lambda p, i: (0, i)),
            pl.BlockSpec((rb, rb), lambda p, i: (0, 0)),
        ],
        out_specs=[
            pl.BlockSpec((SUBLANES, rb), lambda p, i: (0, i * p)),
            pl.BlockSpec((SUBLANES, n_tile_lanes), lambda p, i: (0, 0)),
        ],
        out_shape=[
            jax.ShapeDtypeStruct((SUBLANES, n), jnp.int32),
            jax.ShapeDtypeStruct((SUBLANES, n_tile_lanes), jnp.int32),
        ],
        scratch_shapes=[pltpu.VMEM((N_EXPERTS, rb), F32), pltpu.VMEM((N_EXPERTS, LANES), F32)],
        compiler_params=pltpu.CompilerParams(dimension_semantics=("arbitrary", "arbitrary")),
        name="route",
    )(ri, tri)


def _invert_kernel(tb, pos_hbm, zero_hbm, tok_hbm, idx_a, idx_b, tok_smem, idx_sems, out_sem):
    i = pl.program_id(0)
    n_steps = pl.num_programs(0)
    idx_bufs = (idx_a, idx_b)

    def idx_copy(step, par):
        return pltpu.make_async_copy(pos_hbm.at[step], idx_bufs[par], idx_sems.at[par])

    @pl.when(i == 0)
    def _():
        idx_copy(0, 0).start()
        clear = pltpu.make_async_copy(zero_hbm, tok_smem, out_sem)
        clear.start()
        clear.wait()

    def step(par):
        @pl.when(i + 1 < n_steps)
        def _():
            idx_copy(i + 1, 1 - par).start()

        idx_copy(i, par).wait()
        base = i * tb

        def fill(r, carry):
            for k in range(2):
                tok_smem[idx_bufs[par][k * tb + r]] = base + r
            return carry
        lax.fori_loop(0, tb, fill, 0, unroll=8)

    @pl.when(i % 2 == 0)
    def _():
        step(0)

    @pl.when(i % 2 == 1)
    def _():
        step(1)

    @pl.when(i == n_steps - 1)
    def _():
        out_cp = pltpu.make_async_copy(tok_smem, tok_hbm, out_sem)
        out_cp.start()
        out_cp.wait()


def _invert(pos_blocks, n_slots):
    tb = pos_blocks.shape[1] // 2
    return pl.pallas_call(
        functools.partial(_invert_kernel, tb),
        grid=(pos_blocks.shape[0],),
        in_specs=[pl.BlockSpec(memory_space=pl.ANY), pl.BlockSpec(memory_space=pl.ANY)],
        out_specs=pl.BlockSpec(memory_space=pl.ANY),
        out_shape=jax.ShapeDtypeStruct((n_slots,), jnp.int32),
        scratch_shapes=[pltpu.SMEM((2 * tb,), jnp.int32), pltpu.SMEM((2 * tb,), jnp.int32),
                        pltpu.SMEM((n_slots,), jnp.int32),
                        pltpu.SemaphoreType.DMA((2,)), pltpu.SemaphoreType.DMA(())],
        compiler_params=pltpu.CompilerParams(dimension_semantics=("arbitrary",)),
        name="invert_slots",
    )(pos_blocks, jnp.zeros((n_slots,), jnp.int32))


def _experts_kernel(texp_ref, nact_ref, tok_hbm, h_hbm, nffn_ref, wg_ref, wu_ref, wd_ref, ys_ref,
                    tok_a, tok_b, xbuf, wg_b, wu_b, wd_b, idx_sems, row_sems):
    i = pl.program_id(0)
    n_steps = pl.num_programs(0)
    n_act = nact_ref[0]
    tok_bufs = (tok_a, tok_b)

    def idx_copy(step, par):
        return pltpu.make_async_copy(tok_hbm.at[pl.ds(step * MOE_TILE, MOE_TILE)], tok_bufs[par],
                                     idx_sems.at[par])

    def row_copy(par, r, token):
        return pltpu.make_async_copy(h_hbm.at[pl.ds(token, 1)], xbuf.at[par, pl.ds(r, 1)],
                                     row_sems.at[par])

    def issue_rows(par):
        def body(g, carry):
            for k in range(2):
                r = 2 * g + k
                row_copy(par, r, tok_bufs[par][r]).start(priority=k)
            return carry
        lax.fori_loop(0, MOE_TILE // 2, body, 0, unroll=4)

    def drain_rows(par):
        def body(r, carry):
            row_copy(par, 0, 0).wait()
            return carry
        lax.fori_loop(0, MOE_TILE, body, 0, unroll=8)

    e = texp_ref[i]
    e_prev = texp_ref[jnp.maximum(i - 1, 0)]

    @pl.when((i == 0) | (e != e_prev))
    def _():
        wg_b[...] = wg_ref[0].astype(BF16)
        wu_b[...] = wu_ref[0].astype(BF16)
        wd_b[...] = wd_ref[0].astype(BF16)

    def step(par):
        @pl.when(i == 0)
        def _():
            idx_copy(0, 0).start()
            idx_copy(0, 0).wait()
            issue_rows(0)

            @pl.when(n_steps > 1)
            def _():
                idx_copy(1, 1).start()

        @pl.when(i + 1 < n_steps)
        def _():
            idx_copy(i + 1, 1 - par).wait()

        @pl.when(i + 2 < n_steps)
        def _():
            idx_copy(i + 2, par).start()

        def compute_stages():
            hrow = xbuf[par]
            x = (hrow * lax.rsqrt(jnp.mean(hrow * hrow, axis=-1, keepdims=True) + NORM_EPS)
                 * nffn_ref[...]).astype(BF16)
            yield
            half = D_FF_EXPERT // 2
            gate, up = [], []
            for cb in range(2):
                gate.append(_dot(x, wg_b[:, cb * half:(cb + 1) * half]))
                yield
            for cb in range(2):
                up.append(_dot(x, wu_b[:, cb * half:(cb + 1) * half]))
                yield
            hdn = jnp.concatenate(
                [(gate[cb] * _sigmoid(gate[cb]) * up[cb]).astype(BF16) for cb in range(2)], axis=1)
            yield
            quarter = D_MODEL // 4
            for cb in range(4):
                ys_ref[:, cb * quarter:(cb + 1) * quarter] = _dot(
                    hdn, wd_b[:, cb * quarter:(cb + 1) * quarter])
                yield

        def issue_stages(per_stage=32):
            for r0 in range(0, MOE_TILE, per_stage):
                for r in range(r0, r0 + per_stage):
                    row_copy(1 - par, r, tok_bufs[1 - par][r]).start(priority=r % 2)
                yield

        @pl.when(i + 1 < n_act)
        def _():
            drain_rows(par)
            _interleave(compute_stages(), issue_stages())

        @pl.when(i + 1 == n_act)
        def _():
            drain_rows(par)
            _interleave(compute_stages())

        @pl.when(i >= n_act)
        def _():
            ys_ref[...] = jnp.zeros_like(ys_ref)

    @pl.when(i % 2 == 0)
    def _():
        step(0)

    @pl.when(i % 2 == 1)
    def _():
        step(1)


def _experts(texp, nact, tok, h, nffn, wg, wu, wd):
    n_tiles = tok.shape[0] // MOE_TILE
    w_blk = lambda i, texp, nact: (texp[i], 0, 0)
    grid_spec = pltpu.PrefetchScalarGridSpec(
        num_scalar_prefetch=2,
        grid=(n_tiles,),
        in_specs=[
            pl.BlockSpec(memory_space=pl.ANY),
            pl.BlockSpec(memory_space=pl.ANY),
            pl.BlockSpec((1, D_MODEL), lambda i, texp, nact: (0, 0)),
            pl.BlockSpec((1, D_MODEL, D_FF_EXPERT), w_blk),
            pl.BlockSpec((1, D_MODEL, D_FF_EXPERT), w_blk),
            pl.BlockSpec((1, D_FF_EXPERT, D_MODEL), w_blk),
        ],
        out_specs=pl.BlockSpec((MOE_TILE, D_MODEL), lambda i, texp, nact: (i, 0)),
        scratch_shapes=[pltpu.SMEM((MOE_TILE,), jnp.int32), pltpu.SMEM((MOE_TILE,), jnp.int32),
                        pltpu.VMEM((2, MOE_TILE, D_MODEL), F32),
                        pltpu.VMEM((D_MODEL, D_FF_EXPERT), BF16),
                        pltpu.VMEM((D_MODEL, D_FF_EXPERT), BF16),
                        pltpu.VMEM((D_FF_EXPERT, D_MODEL), BF16),
                        pltpu.SemaphoreType.DMA((2,)), pltpu.SemaphoreType.DMA((2,))],
    )
    return pl.pallas_call(
        _experts_kernel,
        grid_spec=grid_spec,
        out_shape=jax.ShapeDtypeStruct((tok.shape[0], D_MODEL), F32),
        compiler_params=pltpu.CompilerParams(
            dimension_semantics=("arbitrary",), vmem_limit_bytes=VMEM_LIMIT),
        name="experts",
    )(texp, nact, tok, h, nffn, wg, wu, wd)


def _combine_kernel(pos_hbm, ys_hbm, h_ref, rf_ref, nfin_ref, out_ref,
                    idx_a, idx_b, ybuf, idx_sems, row_sems):
    i = pl.program_id(0)
    n_steps = pl.num_programs(0)
    tb = h_ref.shape[0]
    idx_bufs = (idx_a, idx_b)

    def idx_copy(step, par):
        return pltpu.make_async_copy(pos_hbm.at[step], idx_bufs[par], idx_sems.at[par])

    def issue_rows(par):
        def body(g, carry):
            for sub in range(SUBLANES):
                r = g * SUBLANES + sub
                for k in range(2):
                    pltpu.make_async_copy(
                        ys_hbm.at[pl.ds(idx_bufs[par][k * tb + r], 1)],
                        ybuf.at[par, k, g, pl.ds(sub, 1)],
                        row_sems.at[par]).start(priority=k)
            return carry
        lax.fori_loop(0, tb // SUBLANES, body, 0)

    def drain_rows(par):
        def body(r, carry):
            for k in range(2):
                pltpu.make_async_copy(ys_hbm.at[pl.ds(0, 1)], ybuf.at[par, k, 0, pl.ds(0, 1)],
                                      row_sems.at[par]).wait()
            return carry
        lax.fori_loop(0, tb, body, 0, unroll=8)

    def step(par):
        @pl.when(i == 0)
        def _():
            idx_copy(0, 0).start()
            idx_copy(0, 0).wait()
            issue_rows(0)

            @pl.when(n_steps > 1)
            def _():
                idx_copy(1, 1).start()

        @pl.when(i + 1 < n_steps)
        def _():
            idx_copy(i + 1, 1 - par).wait()

        @pl.when(i + 2 < n_steps)
        def _():
            idx_copy(i + 2, par).start()

        n_stage = 8
        rows = tb // n_stage

        def compute_stages():
            rf = rf_ref[...]
            rf_cols = jnp.transpose(jnp.concatenate(
                [rf, jnp.zeros((LANES - SUBLANES, tb), F32)], axis=0))
            for st in range(n_stage):
                rs = slice(st * rows, (st + 1) * rows)
                ts = slice(st * rows // SUBLANES, (st + 1) * rows // SUBLANES)
                y1 = ybuf[par, 0, ts].reshape(rows, D_MODEL)
                y2 = ybuf[par, 1, ts].reshape(rows, D_MODEL)
                hh = h_ref[rs, :] + rf_cols[rs, 0:1] * y1 + rf_cols[rs, 1:2] * y2
                out_ref[rs, :] = (hh * lax.rsqrt(jnp.mean(hh * hh, axis=-1, keepdims=True) + NORM_EPS)
                                  * nfin_ref[...])
                yield

        def issue_stages():
            for st in range(n_stage):
                for r in range(st * rows, (st + 1) * rows):
                    for k in range(2):
                        pltpu.make_async_copy(
                            ys_hbm.at[pl.ds(idx_bufs[1 - par][k * tb + r], 1)],
                            ybuf.at[1 - par, k, r // SUBLANES, pl.ds(r % SUBLANES, 1)],
                            row_sems.at[1 - par]).start(priority=k)
                yield

        drain_rows(par)

        @pl.when(i + 1 < n_steps)
        def _():
            _interleave(issue_stages(), compute_stages())

        @pl.when(i + 1 == n_steps)
        def _():
            _interleave(compute_stages())

    @pl.when(i % 2 == 0)
    def _():
        step(0)

    @pl.when(i % 2 == 1)
    def _():
        step(1)


def _combine(pos_blocks, ys, h, rf, nfin):
    n = h.shape[0]
    tb = pos_blocks.shape[1] // 2
    return pl.pallas_call(
        _combine_kernel,
        grid=(n // tb,),
        in_specs=[
            pl.BlockSpec(memory_space=pl.ANY),
            pl.BlockSpec(memory_space=pl.ANY),
            pl.BlockSpec((tb, D_MODEL), lambda i: (i, 0)),
            pl.BlockSpec((SUBLANES, tb), lambda i: (0, i)),
            pl.BlockSpec((1, D_MODEL), lambda i: (0, 0)),
        ],
        out_specs=pl.BlockSpec((tb, D_MODEL), lambda i: (i, 0)),
        out_shape=jax.ShapeDtypeStruct((n, D_MODEL), F32),
        scratch_shapes=[pltpu.SMEM((2 * tb,), jnp.int32), pltpu.SMEM((2 * tb,), jnp.int32),
                        pltpu.VMEM((2, 2, tb // SUBLANES, SUBLANES, D_MODEL), F32),
                        pltpu.SemaphoreType.DMA((2,)), pltpu.SemaphoreType.DMA((2,))],
        compiler_params=pltpu.CompilerParams(dimension_semantics=("arbitrary",)),
        name="combine",
    )(pos_blocks, ys, h, rf, nfin)


def _moe(h, rf, ri, nffn, wg, wu, wd, nfin):
    n = h.shape[0]
    max_tiles = _max_tiles(n)
    n_tile_lanes = -(-max_tiles // LANES) * LANES
    rb = min(ROUTE_BLOCK, n)
    tri = jnp.triu(jnp.ones((rb, rb), BF16))
    pos, tinfo = _route(ri, tri, n_tile_lanes=n_tile_lanes)
    texp = tinfo[0, :max_tiles]
    nact = tinfo[1, :1]
    tb = min(ROW_BLOCK, n)
    pos_blocks = pos[0:2].reshape(2, n // tb, tb).transpose(1, 0, 2).reshape(n // tb, 2 * tb)
    tok = _invert(pos_blocks, max_tiles * MOE_TILE)
    ys = _experts(texp, nact, tok, h, nffn, wg, wu, wd)
    return _combine(pos_blocks, ys, h, rf, nfin)


def _pad_lanes_row(v):
    return jnp.pad(v.astype(F32), (0, LANES - v.shape[0]))[None, :]


def _block_forward(x, meta_tokens, norm_mix_w, w_in, conv_w, pool_w, pool_scale, a_log, dt_bias,
                   dn_norm_w, w_out, norm_ffn_w, router_group_w, router_group_b, router_expert_w,
                   router_expert_b, expert_w_gate, expert_w_up, expert_w_down, norm_final_w,
                   *, mixer_rows):
    bsz, seq, _ = x.shape
    n = bsz * seq
    x2d = x.reshape(n, D_MODEL)

    ab0 = 5 * D_MODEL
    w_main = jnp.concatenate([w_in[:, :ab0], w_in[:, ab0 + 2 * DN_HEADS:]], axis=1).astype(BF16)
    wab = jnp.pad(w_in[:, ab0:ab0 + 2 * DN_HEADS], ((0, 0), (0, LANES - 2 * DN_HEADS))).astype(BF16)
    wabt = wab.T
    nmix = norm_mix_w[None, :]
    alog = _pad_lanes_row(a_log)
    dtb = _pad_lanes_row(dt_bias)
    alogt = jnp.broadcast_to(alog.T, (LANES, LANES))
    dtbt = jnp.broadcast_to(dtb.T, (LANES, LANES))
    wr = jnp.zeros((D_MODEL, LANES), F32)
    wr = wr.at[:, 0:N_EXPERT_GROUPS].set(router_group_w)
    wr = wr.at[:, SUBLANES:SUBLANES + N_EXPERTS].set(router_expert_w)
    br = jnp.zeros((LANES,), F32)
    br = br.at[0:N_EXPERT_GROUPS].set(router_group_b)
    br = br.at[SUBLANES:SUBLANES + N_EXPERTS].set(router_expert_b)
    params = (nmix, w_main, wab, wabt,
              conv_w, pool_w.astype(BF16), pool_scale[None, :], alog, dtb, alogt, dtbt,
              dn_norm_w[None, :], w_out.astype(BF16), norm_ffn_w[None, :],
              wr.T, jnp.broadcast_to(br[:, None], (LANES, LANES)))

    pad_rows = CHUNK - N_META
    xm = jnp.concatenate([jnp.zeros((pad_rows, D_MODEL), F32), meta_tokens], axis=0)
    zeros_s = jnp.zeros((DN_HEADS, DN_HEAD_DIM, DN_HEAD_DIM), F32)
    zeros_ph = jnp.zeros((POOL_HALO, D_MODEL), F32)
    zeros_ch = jnp.zeros((CONV_HALO, 3 * D_MODEL), F32)
    meta_out = _mixer(xm, zeros_s, zeros_ph, zeros_ch, params,
                      batch=1, t_rows=CHUNK, pad_rows=pad_rows)
    s_meta, ph_meta, ch_meta = meta_out[3], meta_out[4], meta_out[5]

    h, rf, ri, _, _, _ = _mixer(x2d, s_meta, ph_meta, ch_meta, params,
                                batch=bsz, t_rows=mixer_rows, pad_rows=0)
    wg = expert_w_gate.reshape(N_EXPERTS, D_MODEL, D_FF_EXPERT)
    wu = expert_w_up.reshape(N_EXPERTS, D_MODEL, D_FF_EXPERT)
    wd = expert_w_down.reshape(N_EXPERTS, D_FF_EXPERT, D_MODEL)
    out = _moe(h, rf, ri, norm_ffn_w[None, :], wg, wu, wd, norm_final_w[None, :])
    return out.reshape(bsz, seq, D_MODEL)


def kernel(x, meta_tokens, norm_mix_w, w_in, conv_w, pool_w, pool_scale, a_log, dt_bias, dn_norm_w, w_out, norm_ffn_w, router_group_w, router_group_b, router_expert_w, router_expert_b, expert_w_gate, expert_w_up, expert_w_down, norm_final_w):
    assert norm_mix_w.shape[0] == 1, "single-layer block"
    seq = x.shape[1]
    return _block_forward(
        x, meta_tokens, norm_mix_w[0], w_in[0], conv_w[0], pool_w[0], pool_scale[0], a_log[0],
        dt_bias[0], dn_norm_w[0], w_out[0], norm_ffn_w[0], router_group_w[0], router_group_b[0],
        router_expert_w[0], router_expert_b[0], expert_w_gate[0], expert_w_up[0], expert_w_down[0],
        norm_final_w,
        mixer_rows=min(256, seq))
```

```python
import functools
import math

import jax
import jax.numpy as jnp
from jax import lax
from jax.experimental import pallas as pl
from jax.experimental.pallas import tpu as pltpu

F32 = jnp.float32
BF16 = jnp.bfloat16

D_MODEL = 1024
N_META = 16
POOL_WINDOWS = (2, 4, 8, 16)
POOL_GROUP_DIM = 256
DN_HEADS = 8
DN_HEAD_DIM = 128
CONV_WIDTH = 4
CHUNK = 64
N_EXPERT_GROUPS = 4
EXPERTS_PER_GROUP = 8
N_EXPERTS = 32
D_FF_EXPERT = 512
NORM_EPS = 1e-6

LANES = 128
SUBLANES = 8
P_MAIN_COLS = 7 * D_MODEL
POOL_HALO = 16
CONV_HALO = 8
VMEM_LIMIT = 56 * 1024 * 1024
MOE_TILE = 256
ROUTE_BLOCK = 512
ROW_BLOCK = 256


def _dot(a, b):
    return jnp.dot(a, b, preferred_element_type=F32)


def _dot_nt(a, b):
    return lax.dot_general(a, b, (((1,), (1,)), ((), ())), preferred_element_type=F32)


def _dot_tn(a, b):
    return lax.dot_general(a, b, (((0,), (0,)), ((), ())), preferred_element_type=F32)


def _sigmoid(x):
    return 1.0 / (1.0 + jnp.exp(-x))


def _interleave(*stage_generators):
    live = {i: g for i, g in enumerate(stage_generators)}
    results = [None] * len(stage_generators)
    while live:
        for i in list(live):
            try:
                next(live[i])
            except StopIteration as done:
                results[i] = done.value
                del live[i]
    return results


def _intra_chunk(qs, ks, vs, bcols, gcols, grows):
    c = qs[0].shape[0]
    hs = range(len(qs))
    ii = lax.broadcasted_iota(jnp.int32, (c, c), 0)
    jj = lax.broadcasted_iota(jnp.int32, (c, c), 1)
    dec =[jnp.exp(jnp.where(ii >= jj, gcols[h] - grows[h], -jnp.inf)) for h in hs]
    kb = [ks[h].astype(BF16) for h in hs]
    qkb = [jnp.concatenate([qs[h].astype(BF16), kb[h]], axis=0) for h in hs]
    qkk = [_dot_nt(qkb[h], kb[h]) for h in hs]
    yield
    egc = [jnp.exp(gcols[h]) for h in hs]
    pw = [jnp.where(ii > jj, -(bcols[h] * qkk[h][c:] * dec[h]), 0.0) for h in hs]
    sol = [jnp.concatenate([vs[h] * bcols[h], ks[h] * (bcols[h] * egc[h])], axis=1) for h in hs]
    width = 2 * DN_HEAD_DIM
    levels = int(math.log2(c))
    for lvl in range(levels):
        pb = [pw[h].astype(BF16) for h in hs]
        if lvl < levels - 1:
            r = [_dot(pb[h], jnp.concatenate([sol[h].astype(BF16), pb[h]], axis=1)) for h in hs]
            sol = [sol[h] + r[h][:, :width] for h in hs]
            pw = [r[h][:, width:] for h in hs]
        else:
            sol = [sol[h] + _dot(pb[h], sol[h].astype(BF16)) for h in hs]
        yield
    qd = [qs[h] * egc[h] for h in hs]
    glast = [gcols[h][c - 1:c, :] for h in hs]
    kd = [(ks[h] * jnp.exp(glast[h] - gcols[h])).astype(BF16) for h in hs]
    u = [sol[h][:, :DN_HEAD_DIM] for h in hs]
    wq = [jnp.concatenate([sol[h][:, DN_HEAD_DIM:], qd[h]], axis=0).astype(BF16) for h in hs]
    qkd = [(qkk[h][:c] * dec[h]).astype(BF16) for h in hs]
    cd = [jnp.broadcast_to(jnp.exp(glast[h]), (1, DN_HEAD_DIM)) for h in hs]
    return u, wq, qkd, kd, cd


def _scan_chunk(u, wq, qkd, kd, cd, s):
    c = u[0].shape[0]
    hs = range(len(u))
    sb = [s[h].astype(BF16) for h in hs]
    ws = [_dot(wq[h], sb[h]) for h in hs]
    yield
    vb = [(u[h] - ws[h][:c]).astype(BF16) for h in hs]
    o = [ws[h][c:] + _dot(qkd[h], vb[h]) for h in hs]
    s_new = [s[h] * cd[h] + _dot_tn(kd[h], vb[h]) for h in hs]
    yield
    return o, s_new


def _mixer_kernel(pad_rows, t_rows,
                  x_ref, s0_ref, ph0_ref, ch0_ref, nmix_ref, win_ref, wab_ref, wabt_ref,
                  convw_ref, poolw_ref, pscale_ref, alog_ref, dtb_ref, alogt_ref, dtbt_ref,
                  dnw_ref, wout_ref, nffn_ref, wrt_ref, brt_ref,
                  h_ref, rf_ref, ri_ref, s_out_ref, ph_out_ref, ch_out_ref,
                  s_scr, pool_buf, conv_buf, un_scr, ypool_scr, qkv_scr, o_scr, beta_scr, gcol_scr, grow_scr,
                  u_scr, wq_scr, qkd_scr, kd_scr, cd_scr):
    t = t_rows
    j = pl.program_id(1)
    n_chunks = t // CHUNK

    @pl.when(j == 0)
    def _():
        s_scr[...] = s0_ref[...]
        pool_buf[0:POOL_HALO, :] = ph0_ref[...]
        conv_buf[0:CONV_HALO, :] = ch0_ref[...]

    x = x_ref[...]
    un = (x * lax.rsqrt(jnp.mean(x * x, axis=-1, keepdims=True) + NORM_EPS) * nmix_ref[...]).astype(BF16)

    un_scr[...] = un
    sub = POOL_GROUP_DIM
    n_sub = D_MODEL // sub

    def project(col0):
        return _dot(un, win_ref[:, col0:col0 + sub])

    def pool_group(gi):
        win = POOL_WINDOWS[gi]
        cs = slice(gi * sub, (gi + 1) * sub)
        acc = pool_buf[:, cs]
        shift = 1
        while shift < win:
            acc = acc + pltpu.roll(acc, shift, axis=0)
            shift *= 2
        pooled = acc[POOL_HALO:, :] * (1.0 / win) - pool_buf[POOL_HALO:POOL_HALO + t, cs]
        ypool_scr[:, cs] = _dot(pooled.astype(BF16), poolw_ref[gi]) * pscale_ref[:, cs]

    def conv_sub(col0):
        cs = slice(col0, col0 + sub)
        acc = convw_ref[CONV_WIDTH - 1:CONV_WIDTH, cs] * conv_buf[CONV_HALO:CONV_HALO + t, cs]
        for kk in range(CONV_WIDTH - 1):
            off = CONV_HALO - (CONV_WIDTH - 1) + kk
            acc = acc + convw_ref[kk:kk + 1, cs] * conv_buf[off:off + t, cs]
        act = acc * _sigmoid(acc)
        if col0 >= 2 * D_MODEL:
            qkv_scr[:, cs] = act
            return
        for hh in range(sub // DN_HEAD_DIM):
            part = act[:, hh * DN_HEAD_DIM:(hh + 1) * DN_HEAD_DIM]
            nrm = lax.rsqrt(jnp.sum(part * part, axis=-1, keepdims=True) + NORM_EPS)
            if col0 < D_MODEL:
                nrm = nrm * (DN_HEAD_DIM ** -0.5)
            qkv_scr[:, col0 + hh * DN_HEAD_DIM:col0 + (hh + 1) * DN_HEAD_DIM] = part * nrm

    for si in range(n_sub):
        pool_buf[POOL_HALO:POOL_HALO + t, si * sub:(si + 1) * sub] = project(si * sub)
    for si in range(n_sub):
        conv_buf[CONV_HALO:CONV_HALO + t, si * sub:(si + 1) * sub] = project(D_MODEL + si * sub)
        pool_group(si)
    for blk in range(1, 3):
        for si in range(n_sub):
            c0 = blk * D_MODEL + si * sub
            conv_buf[CONV_HALO:CONV_HALO + t, c0:c0 + sub] = project(D_MODEL + c0)
            conv_sub(c0 - D_MODEL)
    pab = _dot(un, wab_ref[...])
    pabt = _dot_nt(wabt_ref[...], un)

    gcol_all = -jnp.exp(alog_ref[...]) * _softplus(pab + dtb_ref[...])
    beta_all = _sigmoid(pab)
    ab_rows = 2 * DN_HEADS
    grow_all = (-jnp.exp(alogt_ref[0:ab_rows, 0:1])
                * _softplus(pabt[0:ab_rows, :] + dtbt_ref[0:ab_rows, 0:1]))
    if pad_rows:
        rid = lax.broadcasted_iota(jnp.int32, (t, LANES), 0)
        gcol_all = jnp.where(rid >= pad_rows, gcol_all, 0.0)
        beta_all = jnp.where(rid >= pad_rows, beta_all, 0.0)
        cid = lax.broadcasted_iota(jnp.int32, (ab_rows, t), 1)
        grow_all = jnp.where(cid >= pad_rows, grow_all, 0.0)
    beta_scr[...] = beta_all
    ci = lax.broadcasted_iota(jnp.int32, (CHUNK, CHUNK), 0)
    cj = lax.broadcasted_iota(jnp.int32, (CHUNK, CHUNK), 1)
    tri_l = jnp.where(ci >= cj, 1.0, 0.0).astype(F32)
    tri_u = jnp.where(ci <= cj, 1.0, 0.0).astype(F32)
    for c in range(n_chunks):
        rs = slice(c * CHUNK, (c + 1) * CHUNK)
        gcol_scr[rs, :] = jnp.dot(tri_l, gcol_all[rs, :], preferred_element_type=F32,
                                  precision=lax.Precision.HIGHEST)
        grow_scr[c] = jnp.dot(grow_all[:, rs], tri_u, preferred_element_type=F32,
                              precision=lax.Precision.HIGHEST)
        if c < n_sub:
            conv_sub(2 * D_MODEL + c * sub)
    for si in range(n_chunks, n_sub):
        conv_sub(2 * D_MODEL + si * sub)

    heads = range(DN_HEADS)

    def head_cols(base, hh):
        return slice(base + hh * DN_HEAD_DIM, base + (hh + 1) * DN_HEAD_DIM)

    def intra_stage(chunks):
        qs, ks, vs, bcols, gcols, grows = [], [], [], [], [], []
        for c in chunks:
            rs = slice(c * CHUNK, (c + 1) * CHUNK)
            gcol_c = gcol_scr[rs, :]
            beta_c = beta_scr[rs, :]
            grow_c = grow_scr[c]
            for hh in heads:
                qs.append(qkv_scr[rs, head_cols(0, hh)])
                ks.append(qkv_scr[rs, head_cols(D_MODEL, hh)])
                vs.append(qkv_scr[rs, head_cols(2 * D_MODEL, hh)])
                bcols.append(beta_c[:, DN_HEADS + hh:DN_HEADS + hh + 1])
                gcols.append(gcol_c[:, hh:hh + 1])
                grows.append(grow_c[hh:hh + 1, :])
        u, wq, qkd, kd, cd = yield from _intra_chunk(qs, ks, vs, bcols, gcols, grows)
        for ci, c in enumerate(chunks):
            for hh in heads:
                idx = ci * DN_HEADS + hh
                u_scr[c, hh] = u[idx]
                wq_scr[c, hh] = wq[idx]
                qkd_scr[c, hh] = qkd[idx]
                kd_scr[c, hh] = kd[idx]
            cd_scr[c] = jnp.concatenate(cd[ci * DN_HEADS:(ci + 1) * DN_HEADS], axis=0)

    def scan_stage(chunks):
        for c in chunks:
            cds = cd_scr[c]
            o, s_new = yield from _scan_chunk(
                [u_scr[c, hh] for hh in heads], [wq_scr[c, hh] for hh in heads],
                [qkd_scr[c, hh] for hh in heads], [kd_scr[c, hh] for hh in heads],
                [cds[hh:hh + 1, :] for hh in heads], [s_scr[hh] for hh in heads])
            for hh in heads:
                s_scr[hh] = s_new[hh]
                o_scr[c * CHUNK:(c + 1) * CHUNK, head_cols(0, hh)] = o[hh]

    def output_stage(r0, nr):
        rs = slice(r0, r0 + nr)
        unh = un_scr[rs, :]

        def gate_proj(col0):
            return [_dot(unh, win_ref[:, col0 + si * sub:col0 + (si + 1) * sub]) for si in range(n_sub)]

        z = gate_proj(4 * D_MODEL)
        yield
        y_dn = []
        for hh in heads:
            o = o_scr[rs, head_cols(0, hh)]
            o = o * lax.rsqrt(jnp.mean(o * o, axis=-1, keepdims=True) + NORM_EPS) * dnw_ref[...]
            per = sub // DN_HEAD_DIM
            zz = z[hh // per][:, (hh % per) * DN_HEAD_DIM:(hh % per + 1) * DN_HEAD_DIM]
            y_dn.append(o * (zz * _sigmoid(zz)))
        g_pool = gate_proj(5 * D_MODEL)
        yield
        g_dn = gate_proj(6 * D_MODEL)
        yield
        merged = []
        per = sub // DN_HEAD_DIM
        for si in range(n_sub):
            y_dn_s = jnp.concatenate(y_dn[si * per:(si + 1) * per], axis=1)
            merged.append((_sigmoid(g_pool[si]) * ypool_scr[rs, si * sub:(si + 1) * sub]
                           + _sigmoid(g_dn[si]) * y_dn_s).astype(BF16))
        h = x_ref[rs, :] + _dot(jnp.concatenate(merged, axis=1), wout_ref[...])
        h_ref[rs, :] = h
        yield
        xt = h * lax.rsqrt(jnp.mean(h * h, axis=-1, keepdims=True) + NORM_EPS) * nffn_ref[...]
        logits = lax.dot_general(wrt_ref[...], xt, (((1,), (1,)), ((), ())),
                                 preferred_element_type=F32, precision=lax.Precision.HIGHEST)
        yield
        logits = logits + brt_ref[...][:, 0:1]
        rid8 = lax.broadcasted_iota(jnp.int32, (SUBLANES, nr), 0)
        lg = jnp.where(rid8 < N_EXPERT_GROUPS, logits[0:SUBLANES, :], -jnp.inf)
        gmax = jnp.max(lg, axis=0, keepdims=True)
        g_idx = jnp.min(jnp.where(lg == gmax, rid8, SUBLANES), axis=0, keepdims=True)
        p_grp = 1.0 / jnp.sum(jnp.exp(lg - gmax), axis=0, keepdims=True)
        sel = jnp.zeros((EXPERTS_PER_GROUP, nr), F32)
        for gi in range(N_EXPERT_GROUPS):
            e0 = SUBLANES + gi * EXPERTS_PER_GROUP
            sel = jnp.where(g_idx == gi, logits[e0:e0 + EXPERTS_PER_GROUP, :], sel)
        m1 = jnp.max(sel, axis=0, keepdims=True)
        i1 = jnp.min(jnp.where(sel == m1, rid8, SUBLANES), axis=0, keepdims=True)
        sel2 = jnp.where(rid8 == i1, -jnp.inf, sel)
        m2 = jnp.max(sel2, axis=0, keepdims=True)
        i2 = jnp.min(jnp.where(sel2 == m2, rid8, SUBLANES), axis=0, keepdims=True)
        e21 = jnp.exp(m2 - m1)
        w1 = 1.0 / (1.0 + e21)
        c1 = p_grp * w1
        c2 = p_grp * (e21 * w1)
        id1 = g_idx * EXPERTS_PER_GROUP + i1
        id2 = g_idx * EXPERTS_PER_GROUP + i2
        rf_ref[:, rs] = jnp.where(rid8 == 0, c1, jnp.where(rid8 == 1, c2, 0.0))
        ri_ref[:, rs] = jnp.where(rid8 == 0, id1, jnp.where(rid8 == 1, id2, 0))

    group = 2 if n_chunks % 2 == 0 else 1
    groups = [list(range(g0, g0 + group)) for g0 in range(0, n_chunks, group)]
    _interleave(intra_stage(groups[0]))
    for gi in range(1, len(groups)):
        _interleave(intra_stage(groups[gi]), scan_stage(groups[gi - 1]))
    rows_before_last = groups[-1][0] * CHUNK
    if rows_before_last >= LANES:
        _interleave(scan_stage(groups[-1]), output_stage(0, rows_before_last))
        _interleave(output_stage(rows_before_last, t - rows_before_last))
    else:
        _interleave(scan_stage(groups[-1]))
        _interleave(output_stage(0, t))

    pool_buf[0:POOL_HALO, :] = pool_buf[t:t + POOL_HALO, :]
    conv_buf[0:CONV_HALO, :] = conv_buf[t:t + CONV_HALO, :]
    s_out_ref[...] = s_scr[...]
    ph_out_ref[...] = pool_buf[0:POOL_HALO, :]
    ch_out_ref[...] = conv_buf[0:CONV_HALO, :]


def _softplus(x):
    return jnp.maximum(x, 0.0) + jnp.log1p(jnp.exp(-jnp.abs(x)))


def _mixer(x2d, s0, ph0, ch0, params, *, batch, t_rows, pad_rows):
    n = x2d.shape[0]
    n_t = n // batch // t_rows
    t = t_rows
    row_blk = lambda b, j: (b * n_t + j, 0)
    col_blk = lambda b, j: (0, b * n_t + j)
    const2 = lambda b, j: (0, 0)
    const3 = lambda b, j: (0, 0, 0)
    (nmix, w_main, wab, wabt,
     convw, poolw, pscale, alog, dtb, alogt, dtbt, dnw, wout, nffn, wrt, brt) = params
    in_specs = [
        pl.BlockSpec((t, D_MODEL), row_blk),
        pl.BlockSpec((DN_HEADS, DN_HEAD_DIM, DN_HEAD_DIM), const3),
        pl.BlockSpec((POOL_HALO, D_MODEL), const2),
        pl.BlockSpec((CONV_HALO, 3 * D_MODEL), const2),
        pl.BlockSpec(nmix.shape, const2),
        pl.BlockSpec(w_main.shape, const2, pipeline_mode=pl.Buffered(1)),
        pl.BlockSpec(wab.shape, const2),
        pl.BlockSpec(wabt.shape, const2),
        pl.BlockSpec(convw.shape, const2),
        pl.BlockSpec(poolw.shape, const3),
        pl.BlockSpec(pscale.shape, const2),
        pl.BlockSpec(alog.shape, const2),
        pl.BlockSpec(dtb.shape, const2),
        pl.BlockSpec(alogt.shape, const2),
        pl.BlockSpec(dtbt.shape, const2),
        pl.BlockSpec(dnw.shape, const2),
        pl.BlockSpec(wout.shape, const2),
        pl.BlockSpec(nffn.shape, const2),
        pl.BlockSpec(wrt.shape, const2),
        pl.BlockSpec(brt.shape, const2),
    ]
    out_specs = [
        pl.BlockSpec((t, D_MODEL), row_blk),
        pl.BlockSpec((SUBLANES, t), col_blk),
        pl.BlockSpec((SUBLANES, t), col_blk),
        pl.BlockSpec((DN_HEADS, DN_HEAD_DIM, DN_HEAD_DIM), const3),
        pl.BlockSpec((POOL_HALO, D_MODEL), const2),
        pl.BlockSpec((CONV_HALO, 3 * D_MODEL), const2),
    ]
    out_shape = [
        jax.ShapeDtypeStruct((n, D_MODEL), F32),
        jax.ShapeDtypeStruct((SUBLANES, n), F32),
        jax.ShapeDtypeStruct((SUBLANES, n), jnp.int32),
        jax.ShapeDtypeStruct((DN_HEADS, DN_HEAD_DIM, DN_HEAD_DIM), F32),
        jax.ShapeDtypeStruct((POOL_HALO, D_MODEL), F32),
        jax.ShapeDtypeStruct((CONV_HALO, 3 * D_MODEL), F32),
    ]
    scratch = [
        pltpu.VMEM((DN_HEADS, DN_HEAD_DIM, DN_HEAD_DIM), F32),
        pltpu.VMEM((t + POOL_HALO, D_MODEL), F32),
        pltpu.VMEM((t + CONV_HALO, 3 * D_MODEL), F32),
        pltpu.VMEM((t, D_MODEL), BF16),
        pltpu.VMEM((t, D_MODEL), F32),
        pltpu.VMEM((t, 3 * D_MODEL), F32),
        pltpu.VMEM((t, D_MODEL), F32),
        pltpu.VMEM((t, LANES), F32),
        pltpu.VMEM((t, LANES), F32),
        pltpu.VMEM((t // CHUNK, 2 * DN_HEADS, CHUNK), F32),
        pltpu.VMEM((t // CHUNK, DN_HEADS, CHUNK, DN_HEAD_DIM), F32),
        pltpu.VMEM((t // CHUNK, DN_HEADS, 2 * CHUNK, DN_HEAD_DIM), BF16),
        pltpu.VMEM((t // CHUNK, DN_HEADS, CHUNK, CHUNK), BF16),
        pltpu.VMEM((t // CHUNK, DN_HEADS, CHUNK, DN_HEAD_DIM), BF16),
        pltpu.VMEM((t // CHUNK, DN_HEADS, DN_HEAD_DIM), F32),
    ]
    return pl.pallas_call(
        functools.partial(_mixer_kernel, pad_rows, t_rows),
        grid=(batch, n_t),
        in_specs=in_specs,
        out_specs=out_specs,
        out_shape=out_shape,
        scratch_shapes=scratch,
        compiler_params=pltpu.CompilerParams(
            dimension_semantics=("arbitrary", "arbitrary"), vmem_limit_bytes=VMEM_LIMIT),
        name="mixer_meta" if pad_rows else "mixer",
    )(x2d, s0, ph0, ch0, *params)


def _max_tiles(n_tokens):
    return 2 * n_tokens // MOE_TILE + N_EXPERTS


def _route_kernel(ri_ref, tri_ref, pos_ref, tinfo_ref, cnt_scr, base_scr):
    ph = pl.program_id(0)
    i = pl.program_id(1)
    rb = ri_ref.shape[1]
    eid = lax.broadcasted_iota(jnp.int32, (N_EXPERTS, rb), 0)
    oh1 = jnp.where(ri_ref[0:1, :] == eid, 1.0, 0.0).astype(F32)
    oh2 = jnp.where(ri_ref[1:2, :] == eid, 1.0, 0.0).astype(F32)
    ohs = oh1 + oh2

    @pl.when((ph == 0) & (i == 0))
    def _():
        cnt_scr[...] = jnp.zeros_like(cnt_scr)

    @pl.when(ph == 0)
    def _():
        cnt_scr[...] += ohs

    @pl.when((ph == 1) & (i == 0))
    def _():
        counts = jnp.sum(cnt_scr[...], axis=1, keepdims=True)
        padded = jnp.floor((counts + (MOE_TILE - 1)) * (1.0 / MOE_TILE)) * MOE_TILE
        padded_b = jnp.broadcast_to(padded, (N_EXPERTS, LANES))
        r = lax.broadcasted_iota(jnp.int32, (N_EXPERTS, N_EXPERTS), 0)
        c = lax.broadcasted_iota(jnp.int32, (N_EXPERTS, N_EXPERTS), 1)
        strict = jnp.where(r > c, 1.0, 0.0).astype(F32)
        offs = jnp.dot(strict, padded_b, preferred_element_type=F32,
                       precision=lax.Precision.HIGHEST)
        base_scr[...] = offs
        ends = offs[:, 0:1] + padded
        total = jnp.sum(padded, axis=0, keepdims=True)
        n_lanes = tinfo_ref.shape[1]
        tile_start = (lax.broadcasted_iota(jnp.int32, (N_EXPERTS, n_lanes), 1) * MOE_TILE).astype(F32)
        texp = jnp.sum(jnp.where(ends <= tile_start, 1.0, 0.0), axis=0, keepdims=True)
        last_active = jnp.sum(jnp.where(ends <= total - MOE_TILE, 1.0, 0.0), axis=0, keepdims=True)
        texp = jnp.minimum(texp, last_active)
        nact = jnp.broadcast_to(total * (1.0 / MOE_TILE), (1, n_lanes))
        rid = lax.broadcasted_iota(jnp.int32, tinfo_ref.shape, 0)
        tinfo_ref[...] = jnp.where(rid == 0, texp, jnp.where(rid == 1, nact, 0.0)).astype(jnp.int32)

    @pl.when(ph == 1)
    def _():
        incl = _dot(ohs.astype(BF16), tri_ref[...])
        slot = base_scr[:, 0:1] + incl - ohs
        pos1 = jnp.sum(oh1 * slot, axis=0, keepdims=True)
        pos2 = jnp.sum(oh2 * slot, axis=0, keepdims=True)
        rid = lax.broadcasted_iota(jnp.int32, pos_ref.shape, 0)
        pos_ref[...] = jnp.where(rid == 0, pos1, jnp.where(rid == 1, pos2, 0.0)).astype(jnp.int32)
        base_scr[...] += jnp.sum(ohs, axis=1, keepdims=True)


def _route(ri, tri, *, n_tile_lanes):
    n = ri.shape[1]
    rb = min(ROUTE_BLOCK, n)
    return pl.pallas_call(
        _route_kernel,
        grid=(2, n // rb),
        in_specs=[
            pl.BlockSpec((SUBLANES, rb), lambda p, i: (0, i)),
            pl.BlockSpec((rb, rb), lambda p, i: (0, 0)),
        ],
        out_specs=[
            pl.BlockSpec((SUBLANES, rb), lambda p, i: (0, i * p)),
            pl.BlockSpec((SUBLANES, n_tile_lanes), lambda p, i: (0, 0)),
        ],
        out_shape=[
            jax.ShapeDtypeStruct((SUBLANES, n), jnp.int32),
            jax.ShapeDtypeStruct((SUBLANES, n_tile_lanes), jnp.int32),
        ],
        scratch_shapes=[pltpu.VMEM((N_EXPERTS, rb), F32), pltpu.VMEM((N_EXPERTS, LANES), F32)],
        compiler_params=pltpu.CompilerParams(dimension_semantics=("arbitrary", "arbitrary")),
        name="route",
    )(ri, tri)


def _invert_kernel(tb, pos_hbm, zero_hbm, tok_hbm, idx_a, idx_b, tok_smem, idx_sems, out_sem):
    i = pl.program_id(0)
    n_steps = pl.num_programs(0)
    idx_bufs = (idx_a, idx_b)

    def idx_copy(step, par):
        return pltpu.make_async_copy(pos_hbm.at[step], idx_bufs[par], idx_sems.at[par])

    @pl.when(i == 0)
    def _():
        idx_copy(0, 0).start()
        clear = pltpu.make_async_copy(zero_hbm, tok_smem, out_sem)
        clear.start()
        clear.wait()

    def step(par):
        @pl.when(i + 1 < n_steps)
        def _():
            idx_copy(i + 1, 1 - par).start()

        idx_copy(i, par).wait()
        base = i * tb

        def fill(r, carry):
            for k in range(2):
                tok_smem[idx_bufs[par][k * tb + r]] = base + r
            return carry
        lax.fori_loop(0, tb, fill, 0, unroll=8)

    @pl.when(i % 2 == 0)
    def _():
        step(0)

    @pl.when(i % 2 == 1)
    def _():
        step(1)

    @pl.when(i == n_steps - 1)
    def _():
        out_cp = pltpu.make_async_copy(tok_smem, tok_hbm, out_sem)
        out_cp.start()
        out_cp.wait()


def _invert(pos_blocks, n_slots):
    tb = pos_blocks.shape[1] // 2
    return pl.pallas_call(
        functools.partial(_invert_kernel, tb),
        grid=(pos_blocks.shape[0],),
        in_specs=[pl.BlockSpec(memory_space=pl.ANY), pl.BlockSpec(memory_space=pl.ANY)],
        out_specs=pl.BlockSpec(memory_space=pl.ANY),
        out_shape=jax.ShapeDtypeStruct((n_slots,), jnp.int32),
        scratch_shapes=[pltpu.SMEM((2 * tb,), jnp.int32), pltpu.SMEM((2 * tb,), jnp.int32),
                        pltpu.SMEM((n_slots,), jnp.int32),
                        pltpu.SemaphoreType.DMA((2,)), pltpu.SemaphoreType.DMA(())],
        compiler_params=pltpu.CompilerParams(dimension_semantics=("arbitrary",)),
        name="invert_slots",
    )(pos_blocks, jnp.zeros((n_slots,), jnp.int32))


def _experts_kernel(texp_ref, nact_ref, tok_hbm, h_hbm, nffn_ref, wg_ref, wu_ref, wd_ref, ys_ref,
                    tok_a, tok_b, tok_c, xbuf, wg_b, wu_b, wd_b, idx_sems, row_sems):
    i = pl.program_id(0)
    n_steps = pl.num_programs(0)
    n_act = nact_ref[0]
    tok_bufs = (tok_a, tok_b, tok_c)
    depth = len(tok_bufs)

    def idx_copy(step, par):
        return pltpu.make_async_copy(tok_hbm.at[pl.ds(step * MOE_TILE, MOE_TILE)], tok_bufs[par],
                                     idx_sems.at[par])

    def row_copy(par, r, token):
        return pltpu.make_async_copy(h_hbm.at[pl.ds(token, 1)], xbuf.at[par, pl.ds(r, 1)],
                                     row_sems.at[par])

    def issue_rows(par):
        def body(g, carry):
            for k in range(2):
                r = 2 * g + k
                row_copy(par, r, tok_bufs[par][r]).start(priority=k)
            return carry
        lax.fori_loop(0, MOE_TILE // 2, body, 0, unroll=4)

    def drain_rows(par):
        def body(r, carry):
            row_copy(par, 0, 0).wait()
            return carry
        lax.fori_loop(0, MOE_TILE, body, 0, unroll=8)

    e = texp_ref[i]
    e_prev = texp_ref[jnp.maximum(i - 1, 0)]

    @pl.when((i == 0) | (e != e_prev))
    def _():
        wg_b[...] = wg_ref[0].astype(BF16)
        wu_b[...] = wu_ref[0].astype(BF16)
        wd_b[...] = wd_ref[0].astype(BF16)

    def step(par):
        ahead = (par + 2) % depth

        @pl.when(i == 0)
        def _():
            idx_copy(0, 0).start()
            idx_copy(0, 0).wait()
            issue_rows(0)

            @pl.when(n_steps > 1)
            def _():
                idx_copy(1, 1).start()
                idx_copy(1, 1).wait()

            @pl.when(n_act > 1)
            def _():
                issue_rows(1)

            @pl.when(n_steps > 2)
            def _():
                idx_copy(2, 2).start()

        @pl.when(i + 2 < n_steps)
        def _():
            idx_copy(i + 2, ahead).wait()

        @pl.when(i + 3 < n_steps)
        def _():
            idx_copy(i + 3, par).start()

        def compute_stages():
            hrow = xbuf[par]
            x = (hrow * lax.rsqrt(jnp.mean(hrow * hrow, axis=-1, keepdims=True) + NORM_EPS)
                 * nffn_ref[...]).astype(BF16)
            yield
            half = D_FF_EXPERT // 2
            gate, up = [], []
            for cb in range(2):
                gate.append(_dot(x, wg_b[:, cb * half:(cb + 1) * half]))
                yield
            for cb in range(2):
                up.append(_dot(x, wu_b[:, cb * half:(cb + 1) * half]))
                yield
            hdn = jnp.concatenate(
                [(gate[cb] * _sigmoid(gate[cb]) * up[cb]).astype(BF16) for cb in range(2)], axis=1)
            yield
            quarter = D_MODEL // 4
            for cb in range(4):
                ys_ref[:, cb * quarter:(cb + 1) * quarter] = _dot(
                    hdn, wd_b[:, cb * quarter:(cb + 1) * quarter])
                yield

        def issue_stages(per_stage=32):
            for r0 in range(0, MOE_TILE, per_stage):
                for r in range(r0, r0 + per_stage):
                    row_copy(ahead, r, tok_bufs[ahead][r]).start(priority=r % 2)
                yield

        @pl.when(i + 2 < n_act)
        def _():
            drain_rows(par)
            _interleave(compute_stages(), issue_stages())

        @pl.when((i < n_act) & (i + 2 >= n_act))
        def _():
            drain_rows(par)
            _interleave(compute_stages())

        @pl.when(i >= n_act)
        def _():
            ys_ref[...] = jnp.zeros_like(ys_ref)

    for par in range(depth):
        pl.when(i % depth == par)(functools.partial(step, par))


def _experts(texp, nact, tok, h, nffn, wg, wu, wd):
    n_tiles = tok.shape[0] // MOE_TILE
    w_blk = lambda i, texp, nact: (texp[i], 0, 0)
    grid_spec = pltpu.PrefetchScalarGridSpec(
        num_scalar_prefetch=2,
        grid=(n_tiles,),
        in_specs=[
            pl.BlockSpec(memory_space=pl.ANY),
            pl.BlockSpec(memory_space=pl.ANY),
            pl.BlockSpec((1, D_MODEL), lambda i, texp, nact: (0, 0)),
            pl.BlockSpec((1, D_MODEL, D_FF_EXPERT), w_blk),
            pl.BlockSpec((1, D_MODEL, D_FF_EXPERT), w_blk),
            pl.BlockSpec((1, D_FF_EXPERT, D_MODEL), w_blk),
        ],
        out_specs=pl.BlockSpec((MOE_TILE, D_MODEL), lambda i, texp, nact: (i, 0)),
        scratch_shapes=[pltpu.SMEM((MOE_TILE,), jnp.int32), pltpu.SMEM((MOE_TILE,), jnp.int32),
                        pltpu.SMEM((MOE_TILE,), jnp.int32),
                        pltpu.VMEM((3, MOE_TILE, D_MODEL), F32),
                        pltpu.VMEM((D_MODEL, D_FF_EXPERT), BF16),
                        pltpu.VMEM((D_MODEL, D_FF_EXPERT), BF16),
                        pltpu.VMEM((D_FF_EXPERT, D_MODEL), BF16),
                        pltpu.SemaphoreType.DMA((3,)), pltpu.SemaphoreType.DMA((3,))],
    )
    return pl.pallas_call(
        _experts_kernel,
        grid_spec=grid_spec,
        out_shape=jax.ShapeDtypeStruct((tok.shape[0], D_MODEL), F32),
        compiler_params=pltpu.CompilerParams(
            dimension_semantics=("arbitrary",), vmem_limit_bytes=VMEM_LIMIT),
        name="experts",
    )(texp, nact, tok, h, nffn, wg, wu, wd)


def _combine_kernel(pos_hbm, ys_hbm, h_ref, rf_ref, nfin_ref, out_ref,
                    idx_a, idx_b, idx_c, ybuf, idx_sems, row_sems):
    i = pl.program_id(0)
    n_steps = pl.num_programs(0)
    tb = h_ref.shape[0]
    idx_bufs = (idx_a, idx_b, idx_c)
    depth = len(idx_bufs)

    def idx_copy(step, par):
        return pltpu.make_async_copy(pos_hbm.at[step], idx_bufs[par], idx_sems.at[par])

    def issue_rows(par):
        def body(g, carry):
            for sub in range(SUBLANES):
                r = g * SUBLANES + sub
                for k in range(2):
                    pltpu.make_async_copy(
                        ys_hbm.at[pl.ds(idx_bufs[par][k * tb + r], 1)],
                        ybuf.at[par, k, g, pl.ds(sub, 1)],
                        row_sems.at[par]).start(priority=k)
            return carry
        lax.fori_loop(0, tb // SUBLANES, body, 0)

    def drain_rows(par):
        def body(r, carry):
            for k in range(2):
                pltpu.make_async_copy(ys_hbm.at[pl.ds(0, 1)], ybuf.at[par, k, 0, pl.ds(0, 1)],
                                      row_sems.at[par]).wait()
            return carry
        lax.fori_loop(0, tb, body, 0, unroll=8)

    def step(par):
        ahead = (par + 2) % depth

        @pl.when(i == 0)
        def _():
            idx_copy(0, 0).start()
            idx_copy(0, 0).wait()
            issue_rows(0)

            @pl.when(n_steps > 1)
            def _():
                idx_copy(1, 1).start()
                idx_copy(1, 1).wait()
                issue_rows(1)

            @pl.when(n_steps > 2)
            def _():
                idx_copy(2, 2).start()

        @pl.when(i + 2 < n_steps)
        def _():
            idx_copy(i + 2, ahead).wait()

        @pl.when(i + 3 < n_steps)
        def _():
            idx_copy(i + 3, par).start()

        n_stage = 8
        rows = tb // n_stage

        def compute_stages():
            rf = rf_ref[...]
            rf_cols = jnp.transpose(jnp.concatenate(
                [rf, jnp.zeros((LANES - SUBLANES, tb), F32)], axis=0))
            for st in range(n_stage):
                rs = slice(st * rows, (st + 1) * rows)
                ts = slice(st * rows // SUBLANES, (st + 1) * rows // SUBLANES)
                y1 = ybuf[par, 0, ts].reshape(rows, D_MODEL)
                y2 = ybuf[par, 1, ts].reshape(rows, D_MODEL)
                hh = h_ref[rs, :] + rf_cols[rs, 0:1] * y1 + rf_cols[rs, 1:2] * y2
                out_ref[rs, :] = (hh * lax.rsqrt(jnp.mean(hh * hh, axis=-1, keepdims=True) + NORM_EPS)
                                  * nfin_ref[...])
                yield

        def issue_stages():
            for st in range(n_stage):
                for r in range(st * rows, (st + 1) * rows):
                    for k in range(2):
                        pltpu.make_async_copy(
                            ys_hbm.at[pl.ds(idx_bufs[ahead][k * tb + r], 1)],
                            ybuf.at[ahead, k, r // SUBLANES, pl.ds(r % SUBLANES, 1)],
                            row_sems.at[ahead]).start(priority=k)
                yield

        drain_rows(par)

        @pl.when(i + 2 < n_steps)
        def _():
            _interleave(issue_stages(), compute_stages())

        @pl.when(i + 2 >= n_steps)
        def _():
            _interleave(compute_stages())

    for par in range(depth):
        pl.when(i % depth == par)(functools.partial(step, par))


def _combine(pos_blocks, ys, h, rf, nfin):
    n = h.shape[0]
    tb = pos_blocks.shape[1] // 2
    return pl.pallas_call(
        _combine_kernel,
        grid=(n // tb,),
        in_specs=[
            pl.BlockSpec(memory_space=pl.ANY),
            pl.BlockSpec(memory_space=pl.ANY),
            pl.BlockSpec((tb, D_MODEL), lambda i: (i, 0)),
            pl.BlockSpec((SUBLANES, tb), lambda i: (0, i)),
            pl.BlockSpec((1, D_MODEL), lambda i: (0, 0)),
        ],
        out_specs=pl.BlockSpec((tb, D_MODEL), lambda i: (i, 0)),
        out_shape=jax.ShapeDtypeStruct((n, D_MODEL), F32),
        scratch_shapes=[pltpu.SMEM((2 * tb,), jnp.int32), pltpu.SMEM((2 * tb,), jnp.int32),
                        pltpu.SMEM((2 * tb,), jnp.int32),
                        pltpu.VMEM((3, 2, tb // SUBLANES, SUBLANES, D_MODEL), F32),
                        pltpu.SemaphoreType.DMA((3,)), pltpu.SemaphoreType.DMA((3,))],
        compiler_params=pltpu.CompilerParams(dimension_semantics=("arbitrary",)),
        name="combine",
    )(pos_blocks, ys, h, rf, nfin)


def _moe(h, rf, ri, nffn, wg, wu, wd, nfin):
    n = h.shape[0]
    max_tiles = _max_tiles(n)
    n_tile_lanes = -(-max_tiles // LANES) * LANES
    rb = min(ROUTE_BLOCK, n)
    tri = jnp.triu(jnp.ones((rb, rb), BF16))
    pos, tinfo = _route(ri, tri, n_tile_lanes=n_tile_lanes)
    texp = tinfo[0, :max_tiles]
    nact = tinfo[1, :1]
    tb = min(ROW_BLOCK, n)
    pos_blocks = pos[0:2].reshape(2, n // tb, tb).transpose(1, 0, 2).reshape(n // tb, 2 * tb)
    tok = _invert(pos_blocks, max_tiles * MOE_TILE)
    ys = _experts(texp, nact, tok, h, nffn, wg, wu, wd)
    return _combine(pos_blocks, ys, h, rf, nfin)


def _pad_lanes_row(v):
    return jnp.pad(v.astype(F32), (0, LANES - v.shape[0]))[None, :]


def _block_forward(x, meta_tokens, norm_mix_w, w_in, conv_w, pool_w, pool_scale, a_log, dt_bias,
                   dn_norm_w, w_out, norm_ffn_w, router_group_w, router_group_b, router_expert_w,
                   router_expert_b, expert_w_gate, expert_w_up, expert_w_down, norm_final_w,
                   *, mixer_rows):
    bsz, seq, _ = x.shape
    n = bsz * seq
    x2d = x.reshape(n, D_MODEL)

    ab0 = 5 * D_MODEL
    w_main = jnp.concatenate([w_in[:, :ab0], w_in[:, ab0 + 2 * DN_HEADS:]], axis=1).astype(BF16)
    wab = jnp.pad(w_in[:, ab0:ab0 + 2 * DN_HEADS], ((0, 0), (0, LANES - 2 * DN_HEADS))).astype(BF16)
    wabt = wab.T
    nmix = norm_mix_w[None, :]
    alog = _pad_lanes_row(a_log)
    dtb = _pad_lanes_row(dt_bias)
    alogt = jnp.broadcast_to(alog.T, (LANES, LANES))
    dtbt = jnp.broadcast_to(dtb.T, (LANES, LANES))
    wr = jnp.zeros((D_MODEL, LANES), F32)
    wr = wr.at[:, 0:N_EXPERT_GROUPS].set(router_group_w)
    wr = wr.at[:, SUBLANES:SUBLANES + N_EXPERTS].set(router_expert_w)
    br = jnp.zeros((LANES,), F32)
    br = br.at[0:N_EXPERT_GROUPS].set(router_group_b)
    br = br.at[SUBLANES:SUBLANES + N_EXPERTS].set(router_expert_b)
    params = (nmix, w_main, wab, wabt,
              conv_w, pool_w.astype(BF16), pool_scale[None, :], alog, dtb, alogt, dtbt,
              dn_norm_w[None, :], w_out.astype(BF16), norm_ffn_w[None, :],
              wr.T, jnp.broadcast_to(br[:, None], (LANES, LANES)))

    pad_rows = CHUNK - N_META
    xm = jnp.concatenate([jnp.zeros((pad_rows, D_MODEL), F32), meta_tokens], axis=0)
    zeros_s = jnp.zeros((DN_HEADS, DN_HEAD_DIM, DN_HEAD_DIM), F32)
    zeros_ph = jnp.zeros((POOL_HALO, D_MODEL), F32)
    zeros_ch = jnp.zeros((CONV_HALO, 3 * D_MODEL), F32)
    meta_out = _mixer(xm, zeros_s, zeros_ph, zeros_ch, params,
                      batch=1, t_rows=CHUNK, pad_rows=pad_rows)
    s_meta, ph_meta, ch_meta = meta_out[3], meta_out[4], meta_out[5]

    h, rf, ri, _, _, _ = _mixer(x2d, s_meta, ph_meta, ch_meta, params,
                                batch=bsz, t_rows=mixer_rows, pad_rows=0)
    wg = expert_w_gate.reshape(N_EXPERTS, D_MODEL, D_FF_EXPERT)
    wu = expert_w_up.reshape(N_EXPERTS, D_MODEL, D_FF_EXPERT)
    wd = expert_w_down.reshape(N_EXPERTS, D_FF_EXPERT, D_MODEL)
    out = _moe(h, rf, ri, norm_ffn_w[None, :], wg, wu, wd, norm_final_w[None, :])
    return out.reshape(bsz, seq, D_MODEL)


def kernel(x, meta_tokens, norm_mix_w, w_in, conv_w, pool_w, pool_scale, a_log, dt_bias, dn_norm_w, w_out, norm_ffn_w, router_group_w, router_group_b, router_expert_w, router_expert_b, expert_w_gate, expert_w_up, expert_w_down, norm_final_w):
    assert norm_mix_w.shape[0] == 1, "single-layer block"
    seq = x.shape[1]
    return _block_forward(
        x, meta_tokens, norm_mix_w[0], w_in[0], conv_w[0], pool_w[0], pool_scale[0], a_log[0],
        dt_bias[0], dn_norm_w[0], w_out[0], norm_ffn_w[0], router_group_w[0], router_group_b[0],
        router_expert_w[0], router_expert_b[0], expert_w_gate[0], expert_w_up[0], expert_w_down[0],
        norm_final_w,
        mixer_rows=min(256, seq))
```

```python
import functools
import math

import jax
import jax.numpy as jnp
from jax import lax
from jax.experimental import pallas as pl
from jax.experimental.pallas import tpu as pltpu

F32 = jnp.float32
BF16 = jnp.bfloat16

D_MODEL = 1024
N_META = 16
POOL_WINDOWS = (2, 4, 8, 16)
POOL_GROUP_DIM = 256
DN_HEADS = 8
DN_HEAD_DIM = 128
CONV_WIDTH = 4
CHUNK = 64
N_EXPERT_GROUPS = 4
EXPERTS_PER_GROUP = 8
N_EXPERTS = 32
D_FF_EXPERT = 512
NORM_EPS = 1e-6

LANES = 128
SUBLANES = 8
P_MAIN_COLS = 7 * D_MODEL
POOL_HALO = 16
CONV_HALO = 8
VMEM_LIMIT = 56 * 1024 * 1024
MOE_TILE = 256
ROUTE_BLOCK = 512
ROW_BLOCK = 256
GATHER_PRIORITY = 1


def _dot(a, b):
    return jnp.dot(a, b, preferred_element_type=F32)


def _dot_nt(a, b):
    return lax.dot_general(a, b, (((1,), (1,)), ((), ())), preferred_element_type=F32)


def _dot_tn(a, b):
    return lax.dot_general(a, b, (((0,), (0,)), ((), ())), preferred_element_type=F32)


def _sigmoid(x):
    return 1.0 / (1.0 + jnp.exp(-x))


def _interleave(*stage_generators):
    live = {i: g for i, g in enumerate(stage_generators)}
    results = [None] * len(stage_generators)
    while live:
        for i in list(live):
            try:
                next(live[i])
            except StopIteration as done:
                results[i] = done.value
                del live[i]
    return results


def _intra_chunk(qs, ks, vs, bcols, gcols, grows):
    c = qs[0].shape[0]
    hs = range(len(qs))
    ii = lax.broadcasted_iota(jnp.int32, (c, c), 0)
    jj = lax.broadcasted_iota(jnp.int32, (c, c), 1)
    dec =[jnp.exp(jnp.where(ii >= jj, gcols[h] - grows[h], -jnp.inf)) for h in hs]
    kb = [ks[h].astype(BF16) for h in hs]
    qkb = [jnp.concatenate([qs[h].astype(BF16), kb[h]], axis=0) for h in hs]
    qkk = [_dot_nt(qkb[h], kb[h]) for h in hs]
    yield
    egc = [jnp.exp(gcols[h]) for h in hs]
    pw = [jnp.where(ii > jj, -(bcols[h] * qkk[h][c:] * dec[h]), 0.0) for h in hs]
    sol = [jnp.concatenate([vs[h] * bcols[h], ks[h] * (bcols[h] * egc[h])], axis=1) for h in hs]
    width = 2 * DN_HEAD_DIM
    levels = int(math.log2(c))
    for lvl in range(levels):
        pb = [pw[h].astype(BF16) for h in hs]
        if lvl < levels - 1:
            r = [_dot(pb[h], jnp.concatenate([sol[h].astype(BF16), pb[h]], axis=1)) for h in hs]
            sol = [sol[h] + r[h][:, :width] for h in hs]
            pw = [r[h][:, width:] for h in hs]
        else:
            sol = [sol[h] + _dot(pb[h], sol[h].astype(BF16)) for h in hs]
        yield
    qd = [qs[h] * egc[h] for h in hs]
    glast = [gcols[h][c - 1:c, :] for h in hs]
    kd = [(ks[h] * jnp.exp(glast[h] - gcols[h])).astype(BF16) for h in hs]
    u = [sol[h][:, :DN_HEAD_DIM] for h in hs]
    wq = [jnp.concatenate([sol[h][:, DN_HEAD_DIM:], qd[h]], axis=0).astype(BF16) for h in hs]
    qkd = [(qkk[h][:c] * dec[h]).astype(BF16) for h in hs]
    cd = [jnp.broadcast_to(jnp.exp(glast[h]), (1, DN_HEAD_DIM)) for h in hs]
    return u, wq, qkd, kd, cd


def _scan_chunk(u, wq, qkd, kd, cd, s):
    c = u[0].shape[0]
    hs = range(len(u))
    sb = [s[h].astype(BF16) for h in hs]
    ws = [_dot(wq[h], sb[h]) for h in hs]
    yield
    vb = [(u[h] - ws[h][:c]).astype(BF16) for h in hs]
    o = [ws[h][c:] + _dot(qkd[h], vb[h]) for h in hs]
    s_new = [s[h] * cd[h] + _dot_tn(kd[h], vb[h]) for h in hs]
    yield
    return o, s_new


def _mixer_kernel(pad_rows, t_rows,
                  x_ref, s0_ref, ph0_ref, ch0_ref, nmix_ref, win_ref, wab_ref, wabt_ref,
                  convw_ref, poolw_ref, pscale_ref, alog_ref, dtb_ref, alogt_ref, dtbt_ref,
                  dnw_ref, wout_ref, nffn_ref, wrt_ref, brt_ref,
                  h_ref, rf_ref, ri_ref, s_out_ref, ph_out_ref, ch_out_ref,
                  s_scr, pool_buf, conv_buf, un_scr, ypool_scr, qkv_scr, o_scr, beta_scr, gcol_scr, grow_scr,
                  u_scr, wq_scr, qkd_scr, kd_scr, cd_scr):
    t = t_rows
    j = pl.program_id(1)
    n_chunks = t // CHUNK

    @pl.when(j == 0)
    def _():
        s_scr[...] = s0_ref[...]
        pool_buf[0:POOL_HALO, :] = ph0_ref[...]
        conv_buf[0:CONV_HALO, :] = ch0_ref[...]

    x = x_ref[...]
    un = (x * lax.rsqrt(jnp.mean(x * x, axis=-1, keepdims=True) + NORM_EPS) * nmix_ref[...]).astype(BF16)

    un_scr[...] = un
    sub = POOL_GROUP_DIM
    n_sub = D_MODEL // sub

    def project(col0):
        return _dot(un, win_ref[:, col0:col0 + sub])

    def pool_group(gi):
        win = POOL_WINDOWS[gi]
        cs = slice(gi * sub, (gi + 1) * sub)
        acc = pool_buf[:, cs]
        shift = 1
        while shift < win:
            acc = acc + pltpu.roll(acc, shift, axis=0)
            shift *= 2
        pooled = acc[POOL_HALO:, :] * (1.0 / win) - pool_buf[POOL_HALO:POOL_HALO + t, cs]
        ypool_scr[:, cs] = _dot(pooled.astype(BF16), poolw_ref[gi]) * pscale_ref[:, cs]

    def conv_sub(col0):
        cs = slice(col0, col0 + sub)
        acc = convw_ref[CONV_WIDTH - 1:CONV_WIDTH, cs] * conv_buf[CONV_HALO:CONV_HALO + t, cs]
        for kk in range(CONV_WIDTH - 1):
            off = CONV_HALO - (CONV_WIDTH - 1) + kk
            acc = acc + convw_ref[kk:kk + 1, cs] * conv_buf[off:off + t, cs]
        act = acc * _sigmoid(acc)
        if col0 >= 2 * D_MODEL:
            qkv_scr[:, cs] = act
            return
        for hh in range(sub // DN_HEAD_DIM):
            part = act[:, hh * DN_HEAD_DIM:(hh + 1) * DN_HEAD_DIM]
            nrm = lax.rsqrt(jnp.sum(part * part, axis=-1, keepdims=True) + NORM_EPS)
            if col0 < D_MODEL:
                nrm = nrm * (DN_HEAD_DIM ** -0.5)
            qkv_scr[:, col0 + hh * DN_HEAD_DIM:col0 + (hh + 1) * DN_HEAD_DIM] = part * nrm

    for si in range(n_sub):
        pool_buf[POOL_HALO:POOL_HALO + t, si * sub:(si + 1) * sub] = project(si * sub)
    for si in range(n_sub):
        conv_buf[CONV_HALO:CONV_HALO + t, si * sub:(si + 1) * sub] = project(D_MODEL + si * sub)
        pool_group(si)
    for blk in range(1, 3):
        for si in range(n_sub):
            c0 = blk * D_MODEL + si * sub
            conv_buf[CONV_HALO:CONV_HALO + t, c0:c0 + sub] = project(D_MODEL + c0)
            conv_sub(c0 - D_MODEL)
    pab = _dot(un, wab_ref[...])
    pabt = _dot_nt(wabt_ref[...], un)

    gcol_all = -jnp.exp(alog_ref[...]) * _softplus(pab + dtb_ref[...])
    beta_all = _sigmoid(pab)
    ab_rows = 2 * DN_HEADS
    grow_all = (-jnp.exp(alogt_ref[0:ab_rows, 0:1])
                * _softplus(pabt[0:ab_rows, :] + dtbt_ref[0:ab_rows, 0:1]))
    if pad_rows:
        rid = lax.broadcasted_iota(jnp.int32, (t, LANES), 0)
        gcol_all = jnp.where(rid >= pad_rows, gcol_all, 0.0)
        beta_all = jnp.where(rid >= pad_rows, beta_all, 0.0)
        cid = lax.broadcasted_iota(jnp.int32, (ab_rows, t), 1)
        grow_all = jnp.where(cid >= pad_rows, grow_all, 0.0)
    beta_scr[...] = beta_all
    in_chunk_r = lax.broadcasted_iota(jnp.int32, (t, LANES), 0) % CHUNK
    in_chunk_c = lax.broadcasted_iota(jnp.int32, (ab_rows, t), 1) % CHUNK
    lane_scan = t % LANES == 0
    shift = 1
    while shift < CHUNK:
        gcol_all = gcol_all + jnp.where(in_chunk_r >= shift, pltpu.roll(gcol_all, shift, axis=0), 0.0)
        if lane_scan:
            grow_all = grow_all + jnp.where(in_chunk_c >= shift, pltpu.roll(grow_all, shift, axis=1), 0.0)
        shift *= 2
    gcol_scr[...] = gcol_all
    if not lane_scan:
        ci = lax.broadcasted_iota(jnp.int32, (t, t), 0)
        cj = lax.broadcasted_iota(jnp.int32, (t, t), 1)
        tri_u = jnp.where((ci <= cj) & (ci // CHUNK == cj // CHUNK), 1.0, 0.0).astype(F32)
        grow_all = jnp.dot(grow_all, tri_u, preferred_element_type=F32,
                           precision=lax.Precision.HIGHEST)
    for c in range(n_chunks):
        grow_scr[c] = grow_all[:, c * CHUNK:(c + 1) * CHUNK]
    for si in range(n_sub):
        conv_sub(2 * D_MODEL + si * sub)

    heads = range(DN_HEADS)

    def head_cols(base, hh):
        return slice(base + hh * DN_HEAD_DIM, base + (hh + 1) * DN_HEAD_DIM)

    def intra_stage(chunks):
        qs, ks, vs, bcols, gcols, grows = [], [], [], [], [], []
        for c in chunks:
            rs = slice(c * CHUNK, (c + 1) * CHUNK)
            gcol_c = gcol_scr[rs, :]
            beta_c = beta_scr[rs, :]
            grow_c = grow_scr[c]
            for hh in heads:
                qs.append(qkv_scr[rs, head_cols(0, hh)])
                ks.append(qkv_scr[rs, head_cols(D_MODEL, hh)])
                vs.append(qkv_scr[rs, head_cols(2 * D_MODEL, hh)])
                bcols.append(beta_c[:, DN_HEADS + hh:DN_HEADS + hh + 1])
                gcols.append(gcol_c[:, hh:hh + 1])
                grows.append(grow_c[hh:hh + 1, :])
        u, wq, qkd, kd, cd = yield from _intra_chunk(qs, ks, vs, bcols, gcols, grows)
        for ci, c in enumerate(chunks):
            for hh in heads:
                idx = ci * DN_HEADS + hh
                u_scr[c, hh] = u[idx]
                wq_scr[c, hh] = wq[idx]
                qkd_scr[c, hh] = qkd[idx]
                kd_scr[c, hh] = kd[idx]
            cd_scr[c] = jnp.concatenate(cd[ci * DN_HEADS:(ci + 1) * DN_HEADS], axis=0)

    def scan_stage(chunks):
        for c in chunks:
            cds = cd_scr[c]
            o, s_new = yield from _scan_chunk(
                [u_scr[c, hh] for hh in heads], [wq_scr[c, hh] for hh in heads],
                [qkd_scr[c, hh] for hh in heads], [kd_scr[c, hh] for hh in heads],
                [cds[hh:hh + 1, :] for hh in heads], [s_scr[hh] for hh in heads])
            for hh in heads:
                s_scr[hh] = s_new[hh]
                o_scr[c * CHUNK:(c + 1) * CHUNK, head_cols(0, hh)] = o[hh]

    def output_stage(r0, nr):
        rs = slice(r0, r0 + nr)
        unh = un_scr[rs, :]

        def gate_proj(col0):
            return [_dot(unh, win_ref[:, col0 + si * sub:col0 + (si + 1) * sub]) for si in range(n_sub)]

        z = gate_proj(4 * D_MODEL)
        yield
        y_dn = []
        for hh in heads:
            o = o_scr[rs, head_cols(0, hh)]
            o = o * lax.rsqrt(jnp.mean(o * o, axis=-1, keepdims=True) + NORM_EPS) * dnw_ref[...]
            per = sub // DN_HEAD_DIM
            zz = z[hh // per][:, (hh % per) * DN_HEAD_DIM:(hh % per + 1) * DN_HEAD_DIM]
            y_dn.append(o * (zz * _sigmoid(zz)))
        g_pool = gate_proj(5 * D_MODEL)
        yield
        g_dn = gate_proj(6 * D_MODEL)
        yield
        merged = []
        per = sub // DN_HEAD_DIM
        for si in range(n_sub):
            y_dn_s = jnp.concatenate(y_dn[si * per:(si + 1) * per], axis=1)
            merged.append((_sigmoid(g_pool[si]) * ypool_scr[rs, si * sub:(si + 1) * sub]
                           + _sigmoid(g_dn[si]) * y_dn_s).astype(BF16))
        h = x_ref[rs, :] + _dot(jnp.concatenate(merged, axis=1), wout_ref[...])
        h_ref[rs, :] = h
        yield
        xt = h * lax.rsqrt(jnp.mean(h * h, axis=-1, keepdims=True) + NORM_EPS) * nffn_ref[...]
        n_logit_rows = SUBLANES + N_EXPERTS
        logits = lax.dot_general(wrt_ref[0:n_logit_rows, :], xt, (((1,), (1,)), ((), ())),
                                 preferred_element_type=F32, precision=lax.Precision.HIGHEST)
        yield
        logits = logits + brt_ref[0:n_logit_rows, 0:1]
        rid8 = lax.broadcasted_iota(jnp.int32, (SUBLANES, nr), 0)
        lg = jnp.where(rid8 < N_EXPERT_GROUPS, logits[0:SUBLANES, :], -jnp.inf)
        gmax = jnp.max(lg, axis=0, keepdims=True)
        g_idx = jnp.min(jnp.where(lg == gmax, rid8, SUBLANES), axis=0, keepdims=True)
        p_grp = 1.0 / jnp.sum(jnp.exp(lg - gmax), axis=0, keepdims=True)
        sel = jnp.zeros((EXPERTS_PER_GROUP, nr), F32)
        for gi in range(N_EXPERT_GROUPS):
            e0 = SUBLANES + gi * EXPERTS_PER_GROUP
            sel = jnp.where(g_idx == gi, logits[e0:e0 + EXPERTS_PER_GROUP, :], sel)
        m1 = jnp.max(sel, axis=0, keepdims=True)
        i1 = jnp.min(jnp.where(sel == m1, rid8, SUBLANES), axis=0, keepdims=True)
        sel2 = jnp.where(rid8 == i1, -jnp.inf, sel)
        m2 = jnp.max(sel2, axis=0, keepdims=True)
        i2 = jnp.min(jnp.where(sel2 == m2, rid8, SUBLANES), axis=0, keepdims=True)
        e21 = jnp.exp(m2 - m1)
        w1 = 1.0 / (1.0 + e21)
        c1 = p_grp * w1
        c2 = p_grp * (e21 * w1)
        id1 = g_idx * EXPERTS_PER_GROUP + i1
        id2 = g_idx * EXPERTS_PER_GROUP + i2
        rf_ref[:, rs] = jnp.where(rid8 == 0, c1, jnp.where(rid8 == 1, c2, 0.0))
        ri_ref[:, rs] = jnp.where(rid8 == 0, id1, jnp.where(rid8 == 1, id2, 0))

    group = 2 if n_chunks % 2 == 0 else 1
    groups = [list(range(g0, g0 + group)) for g0 in range(0, n_chunks, group)]
    _interleave(intra_stage(groups[0]))
    for gi in range(1, len(groups)):
        _interleave(intra_stage(groups[gi]), scan_stage(groups[gi - 1]))
    rows_before_last = groups[-1][0] * CHUNK
    if rows_before_last >= LANES:
        _interleave(scan_stage(groups[-1]), output_stage(0, rows_before_last))
        _interleave(output_stage(rows_before_last, t - rows_before_last))
    else:
        _interleave(scan_stage(groups[-1]))
        _interleave(output_stage(0, t))

    pool_buf[0:POOL_HALO, :] = pool_buf[t:t + POOL_HALO, :]
    conv_buf[0:CONV_HALO, :] = conv_buf[t:t + CONV_HALO, :]
    s_out_ref[...] = s_scr[...]
    ph_out_ref[...] = pool_buf[0:POOL_HALO, :]
    ch_out_ref[...] = conv_buf[0:CONV_HALO, :]


def _softplus(x):
    return jnp.maximum(x, 0.0) + jnp.log1p(jnp.exp(-jnp.abs(x)))


def _mixer(x2d, s0, ph0, ch0, params, *, batch, t_rows, pad_rows):
    n = x2d.shape[0]
    n_t = n // batch // t_rows
    t = t_rows
    row_blk = lambda b, j: (b * n_t + j, 0)
    col_blk = lambda b, j: (0, b * n_t + j)
    const2 = lambda b, j: (0, 0)
    const3 = lambda b, j: (0, 0, 0)
    (nmix, w_main, wab, wabt,
     convw, poolw, pscale, alog, dtb, alogt, dtbt, dnw, wout, nffn, wrt, brt) = params
    in_specs = [
        pl.BlockSpec((t, D_MODEL), row_blk),
        pl.BlockSpec((DN_HEADS, DN_HEAD_DIM, DN_HEAD_DIM), const3),
        pl.BlockSpec((POOL_HALO, D_MODEL), const2),
        pl.BlockSpec((CONV_HALO, 3 * D_MODEL), const2),
        pl.BlockSpec(nmix.shape, const2),
        pl.BlockSpec(w_main.shape, const2, pipeline_mode=pl.Buffered(1)),
        pl.BlockSpec(wab.shape, const2),
        pl.BlockSpec(wabt.shape, const2),
        pl.BlockSpec(convw.shape, const2),
        pl.BlockSpec(poolw.shape, const3),
        pl.BlockSpec(pscale.shape, const2),
        pl.BlockSpec(alog.shape, const2),
        pl.BlockSpec(dtb.shape, const2),
        pl.BlockSpec(alogt.shape, const2),
        pl.BlockSpec(dtbt.shape, const2),
        pl.BlockSpec(dnw.shape, const2),
        pl.BlockSpec(wout.shape, const2),
        pl.BlockSpec(nffn.shape, const2),
        pl.BlockSpec(wrt.shape, const2),
        pl.BlockSpec(brt.shape, const2),
    ]
    out_specs = [
        pl.BlockSpec((t, D_MODEL), row_blk),
        pl.BlockSpec((SUBLANES, t), col_blk),
        pl.BlockSpec((SUBLANES, t), col_blk),
        pl.BlockSpec((DN_HEADS, DN_HEAD_DIM, DN_HEAD_DIM), const3),
        pl.BlockSpec((POOL_HALO, D_MODEL), const2),
        pl.BlockSpec((CONV_HALO, 3 * D_MODEL), const2),
    ]
    out_shape = [
        jax.ShapeDtypeStruct((n, D_MODEL), F32),
        jax.ShapeDtypeStruct((SUBLANES, n), F32),
        jax.ShapeDtypeStruct((SUBLANES, n), jnp.int32),
        jax.ShapeDtypeStruct((DN_HEADS, DN_HEAD_DIM, DN_HEAD_DIM), F32),
        jax.ShapeDtypeStruct((POOL_HALO, D_MODEL), F32),
        jax.ShapeDtypeStruct((CONV_HALO, 3 * D_MODEL), F32),
    ]
    scratch = [
        pltpu.VMEM((DN_HEADS, DN_HEAD_DIM, DN_HEAD_DIM), F32),
        pltpu.VMEM((t + POOL_HALO, D_MODEL), F32),
        pltpu.VMEM((t + CONV_HALO, 3 * D_MODEL), F32),
        pltpu.VMEM((t, D_MODEL), BF16),
        pltpu.VMEM((t, D_MODEL), F32),
        pltpu.VMEM((t, 3 * D_MODEL), F32),
        pltpu.VMEM((t, D_MODEL), F32),
        pltpu.VMEM((t, LANES), F32),
        pltpu.VMEM((t, LANES), F32),
        pltpu.VMEM((t // CHUNK, 2 * DN_HEADS, CHUNK), F32),
        pltpu.VMEM((t // CHUNK, DN_HEADS, CHUNK, DN_HEAD_DIM), F32),
        pltpu.VMEM((t // CHUNK, DN_HEADS, 2 * CHUNK, DN_HEAD_DIM), BF16),
        pltpu.VMEM((t // CHUNK, DN_HEADS, CHUNK, CHUNK), BF16),
        pltpu.VMEM((t // CHUNK, DN_HEADS, CHUNK, DN_HEAD_DIM), BF16),
        pltpu.VMEM((t // CHUNK, DN_HEADS, DN_HEAD_DIM), F32),
    ]
    return pl.pallas_call(
        functools.partial(_mixer_kernel, pad_rows, t_rows),
        grid=(batch, n_t),
        in_specs=in_specs,
        out_specs=out_specs,
        out_shape=out_shape,
        scratch_shapes=scratch,
        compiler_params=pltpu.CompilerParams(
            dimension_semantics=("arbitrary", "arbitrary"), vmem_limit_bytes=VMEM_LIMIT),
        name="mixer_meta" if pad_rows else "mixer",
    )(x2d, s0, ph0, ch0, *params)


def _max_tiles(n_tokens):
    return 2 * n_tokens // MOE_TILE + N_EXPERTS


def _route_kernel(ri_ref, tri_ref, pos_ref, tinfo_ref, cnt_scr, base_scr):
    ph = pl.program_id(0)
    i = pl.program_id(1)
    rb = ri_ref.shape[1]
    eid = lax.broadcasted_iota(jnp.int32, (N_EXPERTS, rb), 0)
    oh1 = jnp.where(ri_ref[0:1, :] == eid, 1.0, 0.0).astype(F32)
    oh2 = jnp.where(ri_ref[1:2, :] == eid, 1.0, 0.0).astype(F32)
    ohs = oh1 + oh2

    @pl.when((ph == 0) & (i == 0))
    def _():
        cnt_scr[...] = jnp.zeros_like(cnt_scr)

    @pl.when(ph == 0)
    def _():
        cnt_scr[...] += ohs

    @pl.when((ph == 1) & (i == 0))
    def _():
        counts = jnp.sum(cnt_scr[...], axis=1, keepdims=True)
        padded = jnp.floor((counts + (MOE_TILE - 1)) * (1.0 / MOE_TILE)) * MOE_TILE
        padded_b = jnp.broadcast_to(padded, (N_EXPERTS, LANES))
        r = lax.broadcasted_iota(jnp.int32, (N_EXPERTS, N_EXPERTS), 0)
        c = lax.broadcasted_iota(jnp.int32, (N_EXPERTS, N_EXPERTS), 1)
        strict = jnp.where(r > c, 1.0, 0.0).astype(F32)
        offs = jnp.dot(strict, padded_b, preferred_element_type=F32,
                       precision=lax.Precision.HIGHEST)
        base_scr[...] = offs
        ends = offs[:, 0:1] + padded
        total = jnp.sum(padded, axis=0, keepdims=True)
        n_lanes = tinfo_ref.shape[1]
        tile_start = (lax.broadcasted_iota(jnp.int32, (N_EXPERTS, n_lanes), 1) * MOE_TILE).astype(F32)
        texp = jnp.sum(jnp.where(ends <= tile_start, 1.0, 0.0), axis=0, keepdims=True)
        last_active = jnp.sum(jnp.where(ends <= total - MOE_TILE, 1.0, 0.0), axis=0, keepdims=True)
        texp = jnp.minimum(texp, last_active)
        nact = jnp.broadcast_to(total * (1.0 / MOE_TILE), (1, n_lanes))
        rid = lax.broadcasted_iota(jnp.int32, tinfo_ref.shape, 0)
        tinfo_ref[...] = jnp.where(rid == 0, texp, jnp.where(rid == 1, nact, 0.0)).astype(jnp.int32)

    @pl.when(ph == 1)
    def _():
        incl = _dot(ohs.astype(BF16), tri_ref[...])
        slot = base_scr[:, 0:1] + incl - ohs
        pos1 = jnp.sum(oh1 * slot, axis=0, keepdims=True)
        pos2 = jnp.sum(oh2 * slot, axis=0, keepdims=True)
        rid = lax.broadcasted_iota(jnp.int32, pos_ref.shape, 0)
        pos_ref[...] = jnp.where(rid == 0, pos1, jnp.where(rid == 1, pos2, 0.0)).astype(jnp.int32)
        base_scr[...] += jnp.sum(ohs, axis=1, keepdims=True)


def _route(ri, tri, *, n_tile_lanes):
    n = ri.shape[1]
    rb = min(ROUTE_BLOCK, n)
    return pl.pallas_call(
        _route_kernel,
        grid=(2, n // rb),
        in_specs=[
            pl.BlockSpec((SUBLANES, rb), lambda p, i: (0, i)),
            pl.BlockSpec((rb, rb), lambda p, i: (0, 0)),
        ],
        out_specs=[
            pl.BlockSpec((SUBLANES, rb), lambda p, i: (0, i * p)),
            pl.BlockSpec((SUBLANES, n_tile_lanes), lambda p, i: (0, 0)),
        ],
        out_shape=[
            jax.ShapeDtypeStruct((SUBLANES, n), jnp.int32),
            jax.ShapeDtypeStruct((SUBLANES, n_tile_lanes), jnp.int32),
        ],
        scratch_shapes=[pltpu.VMEM((N_EXPERTS, rb), F32), pltpu.VMEM((N_EXPERTS, LANES), F32)],
        compiler_params=pltpu.CompilerParams(dimension_semantics=("arbitrary", "arbitrary")),
        name="route",
    )(ri, tri)


def _invert_kernel(tb, pos_hbm, zero_hbm, tok_hbm, idx_a, idx_b, tok_smem, idx_sems, out_sem):
    i = pl.program_id(0)
    n_steps = pl.num_programs(0)
    idx_bufs = (idx_a, idx_b)

    def idx_copy(step, par):
        return pltpu.make_async_copy(pos_hbm.at[step], idx_bufs[par], idx_sems.at[par])

    @pl.when(i == 0)
    def _():
        idx_copy(0, 0).start()
        clear = pltpu.make_async_copy(zero_hbm, tok_smem, out_sem)
        clear.start()
        clear.wait()

    def step(par):
        @pl.when(i + 1 < n_steps)
        def _():
            idx_copy(i + 1, 1 - par).start()

        idx_copy(i, par).wait()
        base = i * tb

        def fill(r, carry):
            for k in range(2):
                tok_smem[idx_bufs[par][k * tb + r]] = base + r
            return carry
        lax.fori_loop(0, tb, fill, 0, unroll=8)

    @pl.when(i % 2 == 0)
    def _():
        step(0)

    @pl.when(i % 2 == 1)
    def _():
        step(1)

    @pl.when(i == n_steps - 1)
    def _():
        out_cp = pltpu.make_async_copy(tok_smem, tok_hbm, out_sem)
        out_cp.start()
        out_cp.wait()


def _invert(pos_blocks, n_slots):
    tb = pos_blocks.shape[1] // 2
    return pl.pallas_call(
        functools.partial(_invert_kernel, tb),
        grid=(pos_blocks.shape[0],),
        in_specs=[pl.BlockSpec(memory_space=pl.ANY), pl.BlockSpec(memory_space=pl.ANY)],
        out_specs=pl.BlockSpec(memory_space=pl.ANY),
        out_shape=jax.ShapeDtypeStruct((n_slots,), jnp.int32),
        scratch_shapes=[pltpu.SMEM((2 * tb,), jnp.int32), pltpu.SMEM((2 * tb,), jnp.int32),
                        pltpu.SMEM((n_slots,), jnp.int32),
                        pltpu.SemaphoreType.DMA((2,)), pltpu.SemaphoreType.DMA(())],
        compiler_params=pltpu.CompilerParams(dimension_semantics=("arbitrary",)),
        name="invert_slots",
    )(pos_blocks, jnp.zeros((n_slots,), jnp.int32))


def _experts_kernel(texp_ref, nact_ref, tok_hbm, h_hbm, nffn_ref, wg_ref, wu_ref, wd_ref, ys_ref,
                    tok_a, tok_b, tok_c, xbuf, wg_b, wu_b, wd_b, idx_sems, row_sems):
    i = pl.program_id(0)
    n_steps = pl.num_programs(0)
    n_act = nact_ref[0]
    tok_bufs = (tok_a, tok_b, tok_c)
    depth = len(tok_bufs)

    def idx_copy(step, par):
        return pltpu.make_async_copy(tok_hbm.at[pl.ds(step * MOE_TILE, MOE_TILE)], tok_bufs[par],
                                     idx_sems.at[par])

    def row_copy(par, r, token):
        return pltpu.make_async_copy(h_hbm.at[pl.ds(token, 1)], xbuf.at[par, pl.ds(r, 1)],
                                     row_sems.at[par])

    def issue_rows(par):
        def body(g, carry):
            for k in range(2):
                r = 2 * g + k
                row_copy(par, r, tok_bufs[par][r]).start(priority=GATHER_PRIORITY)
            return carry
        lax.fori_loop(0, MOE_TILE // 2, body, 0, unroll=4)

    def drain_rows(par):
        def body(r, carry):
            row_copy(par, 0, 0).wait()
            return carry
        lax.fori_loop(0, MOE_TILE, body, 0, unroll=8)

    e = texp_ref[i]
    e_prev = texp_ref[jnp.maximum(i - 1, 0)]

    @pl.when((i == 0) | (e != e_prev))
    def _():
        wg_b[...] = wg_ref[0].astype(BF16)
        wu_b[...] = wu_ref[0].astype(BF16)
        wd_b[...] = wd_ref[0].astype(BF16)

    def step(par):
        ahead = (par + 2) % depth

        @pl.when(i == 0)
        def _():
            idx_copy(0, 0).start()
            idx_copy(0, 0).wait()
            issue_rows(0)

            @pl.when(n_steps > 1)
            def _():
                idx_copy(1, 1).start()
                idx_copy(1, 1).wait()

            @pl.when(n_act > 1)
            def _():
                issue_rows(1)

            @pl.when(n_steps > 2)
            def _():
                idx_copy(2, 2).start()

        @pl.when(i + 2 < n_steps)
        def _():
            idx_copy(i + 2, ahead).wait()

        @pl.when(i + 3 < n_steps)
        def _():
            idx_copy(i + 3, par).start()

        def compute_stages():
            hrow = xbuf[par]
            x = (hrow * lax.rsqrt(jnp.mean(hrow * hrow, axis=-1, keepdims=True) + NORM_EPS)
                 * nffn_ref[...]).astype(BF16)
            yield
            half = D_FF_EXPERT // 2
            gate, up = [], []
            for cb in range(2):
                gate.append(_dot(x, wg_b[:, cb * half:(cb + 1) * half]))
                yield
            for cb in range(2):
                up.append(_dot(x, wu_b[:, cb * half:(cb + 1) * half]))
                yield
            hdn = jnp.concatenate(
                [(gate[cb] * _sigmoid(gate[cb]) * up[cb]).astype(BF16) for cb in range(2)], axis=1)
            yield
            quarter = D_MODEL // 4
            for cb in range(4):
                ys_ref[:, cb * quarter:(cb + 1) * quarter] = _dot(
                    hdn, wd_b[:, cb * quarter:(cb + 1) * quarter])
                yield

        def issue_stages(per_stage=32):
            for r0 in range(0, MOE_TILE, per_stage):
                for r in range(r0, r0 + per_stage):
                    row_copy(ahead, r, tok_bufs[ahead][r]).start(priority=GATHER_PRIORITY)
                yield

        @pl.when(i + 2 < n_act)
        def _():
            drain_rows(par)
            _interleave(compute_stages(), issue_stages())

        @pl.when((i < n_act) & (i + 2 >= n_act))
        def _():
            drain_rows(par)
            _interleave(compute_stages())

        @pl.when(i >= n_act)
        def _():
            ys_ref[...] = jnp.zeros_like(ys_ref)

    for par in range(depth):
        pl.when(i % depth == par)(functools.partial(step, par))


def _experts(texp, nact, tok, h, nffn, wg, wu, wd):
    n_tiles = tok.shape[0] // MOE_TILE
    w_blk = lambda i, texp, nact: (texp[i], 0, 0)
    grid_spec = pltpu.PrefetchScalarGridSpec(
        num_scalar_prefetch=2,
        grid=(n_tiles,),
        in_specs=[
            pl.BlockSpec(memory_space=pl.ANY),
            pl.BlockSpec(memory_space=pl.ANY),
            pl.BlockSpec((1, D_MODEL), lambda i, texp, nact: (0, 0)),
            pl.BlockSpec((1, D_MODEL, D_FF_EXPERT), w_blk),
            pl.BlockSpec((1, D_MODEL, D_FF_EXPERT), w_blk),
            pl.BlockSpec((1, D_FF_EXPERT, D_MODEL), w_blk),
        ],
        out_specs=pl.BlockSpec((MOE_TILE, D_MODEL), lambda i, texp, nact: (i, 0)),
        scratch_shapes=[pltpu.SMEM((MOE_TILE,), jnp.int32), pltpu.SMEM((MOE_TILE,), jnp.int32),
                        pltpu.SMEM((MOE_TILE,), jnp.int32),
                        pltpu.VMEM((3, MOE_TILE, D_MODEL), F32),
                        pltpu.VMEM((D_MODEL, D_FF_EXPERT), BF16),
                        pltpu.VMEM((D_MODEL, D_FF_EXPERT), BF16),
                        pltpu.VMEM((D_FF_EXPERT, D_MODEL), BF16),
                        pltpu.SemaphoreType.DMA((3,)), pltpu.SemaphoreType.DMA((3,))],
    )
    return pl.pallas_call(
        _experts_kernel,
        grid_spec=grid_spec,
        out_shape=jax.ShapeDtypeStruct((tok.shape[0], D_MODEL), F32),
        compiler_params=pltpu.CompilerParams(
            dimension_semantics=("arbitrary",), vmem_limit_bytes=VMEM_LIMIT),
        name="experts",
    )(texp, nact, tok, h, nffn, wg, wu, wd)


def _combine_kernel(pos_hbm, ys_hbm, h_ref, rf_ref, nfin_ref, out_ref,
                    idx_a, idx_b, idx_c, ybuf, idx_sems, row_sems):
    i = pl.program_id(0)
    n_steps = pl.num_programs(0)
    tb = h_ref.shape[0]
    idx_bufs = (idx_a, idx_b, idx_c)
    depth = len(idx_bufs)

    def idx_copy(step, par):
        return pltpu.make_async_copy(pos_hbm.at[step], idx_bufs[par], idx_sems.at[par])

    def issue_rows(par):
        def body(g, carry):
            for sub in range(SUBLANES):
                r = g * SUBLANES + sub
                for k in range(2):
                    pltpu.make_async_copy(
                        ys_hbm.at[pl.ds(idx_bufs[par][k * tb + r], 1)],
                        ybuf.at[par, k, g, pl.ds(sub, 1)],
                        row_sems.at[par]).start(priority=GATHER_PRIORITY)
            return carry
        lax.fori_loop(0, tb // SUBLANES, body, 0)

    def drain_rows(par):
        def body(r, carry):
            for k in range(2):
                pltpu.make_async_copy(ys_hbm.at[pl.ds(0, 1)], ybuf.at[par, k, 0, pl.ds(0, 1)],
                                      row_sems.at[par]).wait()
            return carry
        lax.fori_loop(0, tb, body, 0, unroll=8)

    def step(par):
        ahead = (par + 2) % depth

        @pl.when(i == 0)
        def _():
            idx_copy(0, 0).start()
            idx_copy(0, 0).wait()
            issue_rows(0)

            @pl.when(n_steps > 1)
            def _():
                idx_copy(1, 1).start()
                idx_copy(1, 1).wait()
                issue_rows(1)

            @pl.when(n_steps > 2)
            def _():
                idx_copy(2, 2).start()

        @pl.when(i + 2 < n_steps)
        def _():
            idx_copy(i + 2, ahead).wait()

        @pl.when(i + 3 < n_steps)
        def _():
            idx_copy(i + 3, par).start()

        n_stage = 8
        rows = tb // n_stage

        def compute_stages():
            rf = rf_ref[...]
            rf_cols = jnp.transpose(jnp.concatenate(
                [rf, jnp.zeros((LANES - SUBLANES, tb), F32)], axis=0))
            for st in range(n_stage):
                rs = slice(st * rows, (st + 1) * rows)
                ts = slice(st * rows // SUBLANES, (st + 1) * rows // SUBLANES)
                y1 = ybuf[par, 0, ts].reshape(rows, D_MODEL)
                y2 = ybuf[par, 1, ts].reshape(rows, D_MODEL)
                hh = h_ref[rs, :] + rf_cols[rs, 0:1] * y1 + rf_cols[rs, 1:2] * y2
                out_ref[rs, :] = (hh * lax.rsqrt(jnp.mean(hh * hh, axis=-1, keepdims=True) + NORM_EPS)
                                  * nfin_ref[...])
                yield

        def issue_stages():
            for st in range(n_stage):
                for r in range(st * rows, (st + 1) * rows):
                    for k in range(2):
                        pltpu.make_async_copy(
                            ys_hbm.at[pl.ds(idx_bufs[ahead][k * tb + r], 1)],
                            ybuf.at[ahead, k, r // SUBLANES, pl.ds(r % SUBLANES, 1)],
                            row_sems.at[ahead]).start(priority=GATHER_PRIORITY)
                yield

        drain_rows(par)

        @pl.when(i + 2 < n_steps)
        def _():
            _interleave(issue_stages(), compute_stages())

        @pl.when(i + 2 >= n_steps)
        def _():
            _interleave(compute_stages())

    for par in range(depth):
        pl.when(i % depth == par)(functools.partial(step, par))


def _combine(pos_blocks, ys, h, rf, nfin):
    n = h.shape[0]
    tb = pos_blocks.shape[1] // 2
    return pl.pallas_call(
        _combine_kernel,
        grid=(n // tb,),
        in_specs=[
            pl.BlockSpec(memory_space=pl.ANY),
            pl.BlockSpec(memory_space=pl.ANY),
            pl.BlockSpec((tb, D_MODEL), lambda i: (i, 0)),
            pl.BlockSpec((SUBLANES, tb), lambda i: (0, i)),
            pl.BlockSpec((1, D_MODEL), lambda i: (0, 0)),
        ],
        out_specs=pl.BlockSpec((tb, D_MODEL), lambda i: (i, 0)),
        out_shape=jax.ShapeDtypeStruct((n, D_MODEL), F32),
        scratch_shapes=[pltpu.SMEM((2 * tb,), jnp.int32), pltpu.SMEM((2 * tb,), jnp.int32),
                        pltpu.SMEM((2 * tb,), jnp.int32),
                        pltpu.VMEM((3, 2, tb // SUBLANES, SUBLANES, D_MODEL), F32),
                        pltpu.SemaphoreType.DMA((3,)), pltpu.SemaphoreType.DMA((3,))],
        compiler_params=pltpu.CompilerParams(dimension_semantics=("arbitrary",)),
        name="combine",
    )(pos_blocks, ys, h, rf, nfin)


def _moe(h, rf, ri, nffn, wg, wu, wd, nfin):
    n = h.shape[0]
    max_tiles = _max_tiles(n)
    n_tile_lanes = -(-max_tiles // LANES) * LANES
    rb = min(ROUTE_BLOCK, n)
    tri = jnp.triu(jnp.ones((rb, rb), BF16))
    pos, tinfo = _route(ri, tri, n_tile_lanes=n_tile_lanes)
    texp = tinfo[0, :max_tiles]
    nact = tinfo[1, :1]
    tb = min(ROW_BLOCK, n)
    pos_blocks = pos[0:2].reshape(2, n // tb, tb).transpose(1, 0, 2).reshape(n // tb, 2 * tb)
    tok = _invert(pos_blocks, max_tiles * MOE_TILE)
    ys = _experts(texp, nact, tok, h, nffn, wg, wu, wd)
    return _combine(pos_blocks, ys, h, rf, nfin)


def _pad_lanes_row(v):
    return jnp.pad(v.astype(F32), (0, LANES - v.shape[0]))[None, :]


def _block_forward(x, meta_tokens, norm_mix_w, w_in, conv_w, pool_w, pool_scale, a_log, dt_bias,
                   dn_norm_w, w_out, norm_ffn_w, router_group_w, router_group_b, router_expert_w,
                   router_expert_b, expert_w_gate, expert_w_up, expert_w_down, norm_final_w,
                   *, mixer_rows):
    bsz, seq, _ = x.shape
    n = bsz * seq
    x2d = x.reshape(n, D_MODEL)

    ab0 = 5 * D_MODEL
    w_main = jnp.concatenate([w_in[:, :ab0], w_in[:, ab0 + 2 * DN_HEADS:]], axis=1).astype(BF16)
    wab = jnp.pad(w_in[:, ab0:ab0 + 2 * DN_HEADS], ((0, 0), (0, LANES - 2 * DN_HEADS))).astype(BF16)
    wabt = wab.T
    nmix = norm_mix_w[None, :]
    alog = _pad_lanes_row(a_log)
    dtb = _pad_lanes_row(dt_bias)
    alogt = jnp.broadcast_to(alog.T, (LANES, LANES))
    dtbt = jnp.broadcast_to(dtb.T, (LANES, LANES))
    wr = jnp.zeros((D_MODEL, LANES), F32)
    wr = wr.at[:, 0:N_EXPERT_GROUPS].set(router_group_w)
    wr = wr.at[:, SUBLANES:SUBLANES + N_EXPERTS].set(router_expert_w)
    br = jnp.zeros((LANES,), F32)
    br = br.at[0:N_EXPERT_GROUPS].set(router_group_b)
    br = br.at[SUBLANES:SUBLANES + N_EXPERTS].set(router_expert_b)
    params = (nmix, w_main, wab, wabt,
              conv_w, pool_w.astype(BF16), pool_scale[None, :], alog, dtb, alogt, dtbt,
              dn_norm_w[None, :], w_out.astype(BF16), norm_ffn_w[None, :],
              wr.T, jnp.broadcast_to(br[:, None], (LANES, LANES)))

    pad_rows = CHUNK - N_META
    xm = jnp.concatenate([jnp.zeros((pad_rows, D_MODEL), F32), meta_tokens], axis=0)
    zeros_s = jnp.zeros((DN_HEADS, DN_HEAD_DIM, DN_HEAD_DIM), F32)
    zeros_ph = jnp.zeros((POOL_HALO, D_MODEL), F32)
    zeros_ch = jnp.zeros((CONV_HALO, 3 * D_MODEL), F32)
    meta_out = _mixer(xm, zeros_s, zeros_ph, zeros_ch, params,
                      batch=1, t_rows=CHUNK, pad_rows=pad_rows)
    s_meta, ph_meta, ch_meta = meta_out[3], meta_out[4], meta_out[5]

    h, rf, ri, _, _, _ = _mixer(x2d, s_meta, ph_meta, ch_meta, params,
                                batch=bsz, t_rows=mixer_rows, pad_rows=0)
    wg = expert_w_gate.reshape(N_EXPERTS, D_MODEL, D_FF_EXPERT)
    wu = expert_w_up.reshape(N_EXPERTS, D_MODEL, D_FF_EXPERT)
    wd = expert_w_down.reshape(N_EXPERTS, D_FF_EXPERT, D_MODEL)
    out = _moe(h, rf, ri, norm_ffn_w[None, :], wg, wu, wd, norm_final_w[None, :])
    return out.reshape(bsz, seq, D_MODEL)


def kernel(x, meta_tokens, norm_mix_w, w_in, conv_w, pool_w, pool_scale, a_log, dt_bias, dn_norm_w, w_out, norm_ffn_w, router_group_w, router_group_b, router_expert_w, router_expert_b, expert_w_gate, expert_w_up, expert_w_down, norm_final_w):
    assert norm_mix_w.shape[0] == 1, "single-layer block"
    seq = x.shape[1]
    return _block_forward(
        x, meta_tokens, norm_mix_w[0], w_in[0], conv_w[0], pool_w[0], pool_scale[0], a_log[0],
        dt_bias[0], dn_norm_w[0], w_out[0], norm_ffn_w[0], router_group_w[0], router_group_b[0],
        router_expert_w[0], router_expert_b[0], expert_w_gate[0], expert_w_up[0], expert_w_down[0],
        norm_final_w,
        mixer_rows=min(256, seq))
```

```python
import functools
import math

import jax
import jax.numpy as jnp
from jax import lax
from jax.experimental import pallas as pl
from jax.experimental.pallas import tpu as pltpu

F32 = jnp.float32
BF16 = jnp.bfloat16

D_MODEL = 1024
N_META = 16
POOL_WINDOWS = (2, 4, 8, 16)
POOL_GROUP_DIM = 256
DN_HEADS = 8
DN_HEAD_DIM = 128
CONV_WIDTH = 4
CHUNK = 64
N_EXPERT_GROUPS = 4
EXPERTS_PER_GROUP = 8
N_EXPERTS = 32
D_FF_EXPERT = 512
NORM_EPS = 1e-6

LANES = 128
SUBLANES = 8
P_MAIN_COLS = 7 * D_MODEL
POOL_HALO = 16
CONV_HALO = 8
VMEM_LIMIT = 56 * 1024 * 1024
MOE_TILE = 256
ROUTE_BLOCK = 512
ROW_BLOCK = 256


def _dot(a, b):
    return jnp.dot(a, b, preferred_element_type=F32)


def _dot_nt(a, b):
    return lax.dot_general(a, b, (((1,), (1,)), ((), ())), preferred_element_type=F32)


def _dot_tn(a, b):
    return lax.dot_general(a, b, (((0,), (0,)), ((), ())), preferred_element_type=F32)


def _sigmoid(x):
    return 1.0 / (1.0 + jnp.exp(-x))


def _interleave(*stage_generators):
    live = {i: g for i, g in enumerate(stage_generators)}
    results = [None] * len(stage_generators)
    while live:
        for i in list(live):
            try:
                next(live[i])
            except StopIteration as done:
                results[i] = done.value
                del live[i]
    return results


def _intra_chunk(qs, ks, vs, bcols, gcols, grows):
    c = qs[0].shape[0]
    hs = range(len(qs))
    ii = lax.broadcasted_iota(jnp.int32, (c, c), 0)
    jj = lax.broadcasted_iota(jnp.int32, (c, c), 1)
    dec =[jnp.exp(jnp.where(ii >= jj, gcols[h] - grows[h], -jnp.inf)) for h in hs]
    kb = [ks[h].astype(BF16) for h in hs]
    qkb = [jnp.concatenate([qs[h].astype(BF16), kb[h]], axis=0) for h in hs]
    qkk = [_dot_nt(qkb[h], kb[h]) for h in hs]
    yield
    egc = [jnp.exp(gcols[h]) for h in hs]
    pw = [jnp.where(ii > jj, -(bcols[h] * qkk[h][c:] * dec[h]), 0.0) for h in hs]
    sol = [jnp.concatenate([vs[h] * bcols[h], ks[h] * (bcols[h] * egc[h])], axis=1) for h in hs]
    width = 2 * DN_HEAD_DIM
    levels = int(math.log2(c))
    for lvl in range(levels):
        pb = [pw[h].astype(BF16) for h in hs]
        if lvl < levels - 1:
            r = [_dot(pb[h], jnp.concatenate([sol[h].astype(BF16), pb[h]], axis=1)) for h in hs]
            sol = [sol[h] + r[h][:, :width] for h in hs]
            pw = [r[h][:, width:] for h in hs]
        else:
            sol = [sol[h] + _dot(pb[h], sol[h].astype(BF16)) for h in hs]
        yield
    qd = [qs[h] * egc[h] for h in hs]
    glast = [gcols[h][c - 1:c, :] for h in hs]
    kd = [(ks[h] * jnp.exp(glast[h] - gcols[h])).astype(BF16) for h in hs]
    u = [sol[h][:, :DN_HEAD_DIM] for h in hs]
    wq = [jnp.concatenate([sol[h][:, DN_HEAD_DIM:], qd[h]], axis=0).astype(BF16) for h in hs]
    qkd = [(qkk[h][:c] * dec[h]).astype(BF16) for h in hs]
    cd = [jnp.broadcast_to(jnp.exp(glast[h]), (1, DN_HEAD_DIM)) for h in hs]
    return u, wq, qkd, kd, cd


def _scan_chunk(u, wq, qkd, kd, cd, s):
    c = u[0].shape[0]
    hs = range(len(u))
    sb = [s[h].astype(BF16) for h in hs]
    ws = [_dot(wq[h], sb[h]) for h in hs]
    yield
    vb = [(u[h] - ws[h][:c]).astype(BF16) for h in hs]
    o = [ws[h][c:] + _dot(qkd[h], vb[h]) for h in hs]
    s_new = [s[h] * cd[h] + _dot_tn(kd[h], vb[h]) for h in hs]
    yield
    return o, s_new


def _mixer_kernel(pad_rows, t_rows,
                  x_ref, s0_ref, ph0_ref, ch0_ref, nmix_ref, win_ref, wab_ref, wabt_ref,
                  convw_ref, poolw_ref, pscale_ref, alog_ref, dtb_ref, alogt_ref, dtbt_ref,
                  dnw_ref, wout_ref, nffn_ref, wrt_ref, brt_ref,
                  h_ref, rf_ref, ri_ref, s_out_ref, ph_out_ref, ch_out_ref,
                  s_scr, pool_buf, conv_buf, un_scr, ypool_scr, qkv_scr, o_scr, beta_scr, gcol_scr, grow_scr,
                  u_scr, wq_scr, qkd_scr, kd_scr, cd_scr):
    t = t_rows
    j = pl.program_id(1)
    n_chunks = t // CHUNK

    @pl.when(j == 0)
    def _():
        s_scr[...] = s0_ref[...]
        pool_buf[0:POOL_HALO, :] = ph0_ref[...]
        conv_buf[0:CONV_HALO, :] = ch0_ref[...]

    x = x_ref[...]
    un = (x * lax.rsqrt(jnp.mean(x * x, axis=-1, keepdims=True) + NORM_EPS) * nmix_ref[...]).astype(BF16)

    un_scr[...] = un
    sub = POOL_GROUP_DIM
    n_sub = D_MODEL // sub

    def project(col0):
        return _dot(un, win_ref[:, col0:col0 + sub])

    def pool_group(gi):
        win = POOL_WINDOWS[gi]
        cs = slice(gi * sub, (gi + 1) * sub)
        acc = pool_buf[:, cs]
        shift = 1
        while shift < win:
            acc = acc + pltpu.roll(acc, shift, axis=0)
            shift *= 2
        pooled = acc[POOL_HALO:, :] * (1.0 / win) - pool_buf[POOL_HALO:POOL_HALO + t, cs]
        ypool_scr[:, cs] = _dot(pooled.astype(BF16), poolw_ref[gi]) * pscale_ref[:, cs]

    def conv_sub(col0):
        cs = slice(col0, col0 + sub)
        acc = convw_ref[CONV_WIDTH - 1:CONV_WIDTH, cs] * conv_buf[CONV_HALO:CONV_HALO + t, cs]
        for kk in range(CONV_WIDTH - 1):
            off = CONV_HALO - (CONV_WIDTH - 1) + kk
            acc = acc + convw_ref[kk:kk + 1, cs] * conv_buf[off:off + t, cs]
        act = acc * _sigmoid(acc)
        if col0 >= 2 * D_MODEL:
            qkv_scr[:, cs] = act
            return
        for hh in range(sub // DN_HEAD_DIM):
            part = act[:, hh * DN_HEAD_DIM:(hh + 1) * DN_HEAD_DIM]
            nrm = lax.rsqrt(jnp.sum(part * part, axis=-1, keepdims=True) + NORM_EPS)
            if col0 < D_MODEL:
                nrm = nrm * (DN_HEAD_DIM ** -0.5)
            qkv_scr[:, col0 + hh * DN_HEAD_DIM:col0 + (hh + 1) * DN_HEAD_DIM] = part * nrm

    for si in range(n_sub):
        pool_buf[POOL_HALO:POOL_HALO + t, si * sub:(si + 1) * sub] = project(si * sub)
    for si in range(n_sub):
        conv_buf[CONV_HALO:CONV_HALO + t, si * sub:(si + 1) * sub] = project(D_MODEL + si * sub)
        pool_group(si)
    for blk in range(1, 3):
        for si in range(n_sub):
            c0 = blk * D_MODEL + si * sub
            conv_buf[CONV_HALO:CONV_HALO + t, c0:c0 + sub] = project(D_MODEL + c0)
            conv_sub(c0 - D_MODEL)
    pab = _dot(un, wab_ref[...])
    pabt = _dot_nt(wabt_ref[...], un)

    gcol_all = -jnp.exp(alog_ref[...]) * _softplus(pab + dtb_ref[...])
    beta_all = _sigmoid(pab)
    ab_rows = 2 * DN_HEADS
    grow_all = (-jnp.exp(alogt_ref[0:ab_rows, 0:1])
                * _softplus(pabt[0:ab_rows, :] + dtbt_ref[0:ab_rows, 0:1]))
    if pad_rows:
        rid = lax.broadcasted_iota(jnp.int32, (t, LANES), 0)
        gcol_all = jnp.where(rid >= pad_rows, gcol_all, 0.0)
        beta_all = jnp.where(rid >= pad_rows, beta_all, 0.0)
        cid = lax.broadcasted_iota(jnp.int32, (ab_rows, t), 1)
        grow_all = jnp.where(cid >= pad_rows, grow_all, 0.0)
    beta_scr[...] = beta_all
    in_chunk_r = lax.broadcasted_iota(jnp.int32, (t, LANES), 0) % CHUNK
    in_chunk_c = lax.broadcasted_iota(jnp.int32, (ab_rows, t), 1) % CHUNK
    lane_scan = t % LANES == 0
    shift = 1
    while shift < CHUNK:
        gcol_all = gcol_all + jnp.where(in_chunk_r >= shift, pltpu.roll(gcol_all, shift, axis=0), 0.0)
        if lane_scan:
            grow_all = grow_all + jnp.where(in_chunk_c >= shift, pltpu.roll(grow_all, shift, axis=1), 0.0)
        shift *= 2
    gcol_scr[...] = gcol_all
    if not lane_scan:
        ci = lax.broadcasted_iota(jnp.int32, (t, t), 0)
        cj = lax.broadcasted_iota(jnp.int32, (t, t), 1)
        tri_u = jnp.where((ci <= cj) & (ci // CHUNK == cj // CHUNK), 1.0, 0.0).astype(F32)
        grow_all = jnp.dot(grow_all, tri_u, preferred_element_type=F32,
                           precision=lax.Precision.HIGHEST)
    for c in range(n_chunks):
        grow_scr[c] = grow_all[:, c * CHUNK:(c + 1) * CHUNK]
    for si in range(n_sub):
        conv_sub(2 * D_MODEL + si * sub)

    heads = range(DN_HEADS)

    def head_cols(base, hh):
        return slice(base + hh * DN_HEAD_DIM, base + (hh + 1) * DN_HEAD_DIM)

    def intra_stage(chunks):
        qs, ks, vs, bcols, gcols, grows = [], [], [], [], [], []
        for c in chunks:
            rs = slice(c * CHUNK, (c + 1) * CHUNK)
            gcol_c = gcol_scr[rs, :]
            beta_c = beta_scr[rs, :]
            grow_c = grow_scr[c]
            for hh in heads:
                qs.append(qkv_scr[rs, head_cols(0, hh)])
                ks.append(qkv_scr[rs, head_cols(D_MODEL, hh)])
                vs.append(qkv_scr[rs, head_cols(2 * D_MODEL, hh)])
                bcols.append(beta_c[:, DN_HEADS + hh:DN_HEADS + hh + 1])
                gcols.append(gcol_c[:, hh:hh + 1])
                grows.append(grow_c[hh:hh + 1, :])
        u, wq, qkd, kd, cd = yield from _intra_chunk(qs, ks, vs, bcols, gcols, grows)
        for ci, c in enumerate(chunks):
            for hh in heads:
                idx = ci * DN_HEADS + hh
                u_scr[c, hh] = u[idx]
                wq_scr[c, hh] = wq[idx]
                qkd_scr[c, hh] = qkd[idx]
                kd_scr[c, hh] = kd[idx]
            cd_scr[c] = jnp.concatenate(cd[ci * DN_HEADS:(ci + 1) * DN_HEADS], axis=0)

    def scan_stage(chunks):
        for c in chunks:
            cds = cd_scr[c]
            o, s_new = yield from _scan_chunk(
                [u_scr[c, hh] for hh in heads], [wq_scr[c, hh] for hh in heads],
                [qkd_scr[c, hh] for hh in heads], [kd_scr[c, hh] for hh in heads],
                [cds[hh:hh + 1, :] for hh in heads], [s_scr[hh] for hh in heads])
            for hh in heads:
                s_scr[hh] = s_new[hh]
                o_scr[c * CHUNK:(c + 1) * CHUNK, head_cols(0, hh)] = o[hh]

    def output_stage(r0, nr):
        rs = slice(r0, r0 + nr)
        unh = un_scr[rs, :]

        def gate_proj(col0):
            return [_dot(unh, win_ref[:, col0 + si * sub:col0 + (si + 1) * sub]) for si in range(n_sub)]

        z = gate_proj(4 * D_MODEL)
        yield
        y_dn = []
        for hh in heads:
            o = o_scr[rs, head_cols(0, hh)]
            o = o * lax.rsqrt(jnp.mean(o * o, axis=-1, keepdims=True) + NORM_EPS) * dnw_ref[...]
            per = sub // DN_HEAD_DIM
            zz = z[hh // per][:, (hh % per) * DN_HEAD_DIM:(hh % per + 1) * DN_HEAD_DIM]
            y_dn.append(o * (zz * _sigmoid(zz)))
        g_pool = gate_proj(5 * D_MODEL)
        yield
        g_dn = gate_proj(6 * D_MODEL)
        yield
        merged = []
        per = sub // DN_HEAD_DIM
        for si in range(n_sub):
            y_dn_s = jnp.concatenate(y_dn[si * per:(si + 1) * per], axis=1)
            merged.append((_sigmoid(g_pool[si]) * ypool_scr[rs, si * sub:(si + 1) * sub]
                           + _sigmoid(g_dn[si]) * y_dn_s).astype(BF16))
        h = x_ref[rs, :] + _dot(jnp.concatenate(merged, axis=1), wout_ref[...])
        h_ref[rs, :] = h
        yield
        xt = h * lax.rsqrt(jnp.mean(h * h, axis=-1, keepdims=True) + NORM_EPS) * nffn_ref[...]
        n_logit_rows = SUBLANES + N_EXPERTS
        logits = lax.dot_general(wrt_ref[0:n_logit_rows, :], xt, (((1,), (1,)), ((), ())),
                                 preferred_element_type=F32, precision=lax.Precision.HIGHEST)
        yield
        logits = logits + brt_ref[0:n_logit_rows, 0:1]
        rid8 = lax.broadcasted_iota(jnp.int32, (SUBLANES, nr), 0)
        lg = jnp.where(rid8 < N_EXPERT_GROUPS, logits[0:SUBLANES, :], -jnp.inf)
        gmax = jnp.max(lg, axis=0, keepdims=True)
        g_idx = jnp.min(jnp.where(lg == gmax, rid8, SUBLANES), axis=0, keepdims=True)
        p_grp = 1.0 / jnp.sum(jnp.exp(lg - gmax), axis=0, keepdims=True)
        sel = jnp.zeros((EXPERTS_PER_GROUP, nr), F32)
        for gi in range(N_EXPERT_GROUPS):
            e0 = SUBLANES + gi * EXPERTS_PER_GROUP
            sel = jnp.where(g_idx == gi, logits[e0:e0 + EXPERTS_PER_GROUP, :], sel)
        m1 = jnp.max(sel, axis=0, keepdims=True)
        i1 = jnp.min(jnp.where(sel == m1, rid8, SUBLANES), axis=0, keepdims=True)
        sel2 = jnp.where(rid8 == i1, -jnp.inf, sel)
        m2 = jnp.max(sel2, axis=0, keepdims=True)
        i2 = jnp.min(jnp.where(sel2 == m2, rid8, SUBLANES), axis=0, keepdims=True)
        e21 = jnp.exp(m2 - m1)
        w1 = 1.0 / (1.0 + e21)
        c1 = p_grp * w1
        c2 = p_grp * (e21 * w1)
        id1 = g_idx * EXPERTS_PER_GROUP + i1
        id2 = g_idx * EXPERTS_PER_GROUP + i2
        rf_ref[:, rs] = jnp.where(rid8 == 0, c1, jnp.where(rid8 == 1, c2, 0.0))
        ri_ref[:, rs] = jnp.where(rid8 == 0, id1, jnp.where(rid8 == 1, id2, 0))

    group = 2 if n_chunks % 2 == 0 else 1
    groups = [list(range(g0, g0 + group)) for g0 in range(0, n_chunks, group)]
    _interleave(intra_stage(groups[0]))
    for gi in range(1, len(groups)):
        _interleave(intra_stage(groups[gi]), scan_stage(groups[gi - 1]))
    rows_before_last = groups[-1][0] * CHUNK
    if rows_before_last >= LANES:
        _interleave(scan_stage(groups[-1]), output_stage(0, rows_before_last))
        _interleave(output_stage(rows_before_last, t - rows_before_last))
    else:
        _interleave(scan_stage(groups[-1]))
        _interleave(output_stage(0, t))

    pool_buf[0:POOL_HALO, :] = pool_buf[t:t + POOL_HALO, :]
    conv_buf[0:CONV_HALO, :] = conv_buf[t:t + CONV_HALO, :]
    s_out_ref[...] = s_scr[...]
    ph_out_ref[...] = pool_buf[0:POOL_HALO, :]
    ch_out_ref[...] = conv_buf[0:CONV_HALO, :]


def _softplus(x):
    return jnp.maximum(x, 0.0) + jnp.log1p(jnp.exp(-jnp.abs(x)))


def _mixer(x2d, s0, ph0, ch0, params, *, batch, t_rows, pad_rows):
    n = x2d.shape[0]
    n_t = n // batch // t_rows
    t = t_rows
    row_blk = lambda b, j: (b * n_t + j, 0)
    col_blk = lambda b, j: (0, b * n_t + j)
    const2 = lambda b, j: (0, 0)
    const3 = lambda b, j: (0, 0, 0)
    (nmix, w_main, wab, wabt,
     convw, poolw, pscale, alog, dtb, alogt, dtbt, dnw, wout, nffn, wrt, brt) = params
    in_specs = [
        pl.BlockSpec((t, D_MODEL), row_blk),
        pl.BlockSpec((DN_HEADS, DN_HEAD_DIM, DN_HEAD_DIM), const3),
        pl.BlockSpec((POOL_HALO, D_MODEL), const2),
        pl.BlockSpec((CONV_HALO, 3 * D_MODEL), const2),
        pl.BlockSpec(nmix.shape, const2),
        pl.BlockSpec(w_main.shape, const2, pipeline_mode=pl.Buffered(1)),
        pl.BlockSpec(wab.shape, const2),
        pl.BlockSpec(wabt.shape, const2),
        pl.BlockSpec(convw.shape, const2),
        pl.BlockSpec(poolw.shape, const3),
        pl.BlockSpec(pscale.shape, const2),
        pl.BlockSpec(alog.shape, const2),
        pl.BlockSpec(dtb.shape, const2),
        pl.BlockSpec(alogt.shape, const2),
        pl.BlockSpec(dtbt.shape, const2),
        pl.BlockSpec(dnw.shape, const2),
        pl.BlockSpec(wout.shape, const2),
        pl.BlockSpec(nffn.shape, const2),
        pl.BlockSpec(wrt.shape, const2),
        pl.BlockSpec(brt.shape, const2),
    ]
    out_specs = [
        pl.BlockSpec((t, D_MODEL), row_blk),
        pl.BlockSpec((SUBLANES, t), col_blk),
        pl.BlockSpec((SUBLANES, t), col_blk),
        pl.BlockSpec((DN_HEADS, DN_HEAD_DIM, DN_HEAD_DIM), const3),
        pl.BlockSpec((POOL_HALO, D_MODEL), const2),
        pl.BlockSpec((CONV_HALO, 3 * D_MODEL), const2),
    ]
    out_shape = [
        jax.ShapeDtypeStruct((n, D_MODEL), F32),
        jax.ShapeDtypeStruct((SUBLANES, n), F32),
        jax.ShapeDtypeStruct((SUBLANES, n), jnp.int32),
        jax.ShapeDtypeStruct((DN_HEADS, DN_HEAD_DIM, DN_HEAD_DIM), F32),
        jax.ShapeDtypeStruct((POOL_HALO, D_MODEL), F32),
        jax.ShapeDtypeStruct((CONV_HALO, 3 * D_MODEL), F32),
    ]
    scratch = [
        pltpu.VMEM((DN_HEADS, DN_HEAD_DIM, DN_HEAD_DIM), F32),
        pltpu.VMEM((t + POOL_HALO, D_MODEL), F32),
        pltpu.VMEM((t + CONV_HALO, 3 * D_MODEL), F32),
        pltpu.VMEM((t, D_MODEL), BF16),
        pltpu.VMEM((t, D_MODEL), F32),
        pltpu.VMEM((t, 3 * D_MODEL), F32),
        pltpu.VMEM((t, D_MODEL), F32),
        pltpu.VMEM((t, LANES), F32),
        pltpu.VMEM((t, LANES), F32),
        pltpu.VMEM((t // CHUNK, 2 * DN_HEADS, CHUNK), F32),
        pltpu.VMEM((t // CHUNK, DN_HEADS, CHUNK, DN_HEAD_DIM), F32),
        pltpu.VMEM((t // CHUNK, DN_HEADS, 2 * CHUNK, DN_HEAD_DIM), BF16),
        pltpu.VMEM((t // CHUNK, DN_HEADS, CHUNK, CHUNK), BF16),
        pltpu.VMEM((t // CHUNK, DN_HEADS, CHUNK, DN_HEAD_DIM), BF16),
        pltpu.VMEM((t // CHUNK, DN_HEADS, DN_HEAD_DIM), F32),
    ]
    return pl.pallas_call(
        functools.partial(_mixer_kernel, pad_rows, t_rows),
        grid=(batch, n_t),
        in_specs=in_specs,
        out_specs=out_specs,
        out_shape=out_shape,
        scratch_shapes=scratch,
        compiler_params=pltpu.CompilerParams(
            dimension_semantics=("arbitrary", "arbitrary"), vmem_limit_bytes=VMEM_LIMIT),
        name="mixer_meta" if pad_rows else "mixer",
    )(x2d, s0, ph0, ch0, *params)


def _max_tiles(n_tokens):
    return 2 * n_tokens // MOE_TILE + N_EXPERTS


def _route_kernel(ri_ref, tri_ref, pos_ref, tinfo_ref, cnt_scr, base_scr):
    ph = pl.program_id(0)
    i = pl.program_id(1)
    rb = ri_ref.shape[1]
    eid = lax.broadcasted_iota(jnp.int32, (N_EXPERTS, rb), 0)
    oh1 = jnp.where(ri_ref[0:1, :] == eid, 1.0, 0.0).astype(F32)
    oh2 = jnp.where(ri_ref[1:2, :] == eid, 1.0, 0.0).astype(F32)
    ohs = oh1 + oh2

    @pl.when((ph == 0) & (i == 0))
    def _():
        cnt_scr[...] = jnp.zeros_like(cnt_scr)

    @pl.when(ph == 0)
    def _():
        cnt_scr[...] += ohs

    @pl.when((ph == 1) & (i == 0))
    def _():
        counts = jnp.sum(cnt_scr[...], axis=1, keepdims=True)
        padded = jnp.floor((counts + (MOE_TILE - 1)) * (1.0 / MOE_TILE)) * MOE_TILE
        padded_b = jnp.broadcast_to(padded, (N_EXPERTS, LANES))
        r = lax.broadcasted_iota(jnp.int32, (N_EXPERTS, N_EXPERTS), 0)
        c = lax.broadcasted_iota(jnp.int32, (N_EXPERTS, N_EXPERTS), 1)
        strict = jnp.where(r > c, 1.0, 0.0).astype(F32)
        offs = jnp.dot(strict, padded_b, preferred_element_type=F32,
                       precision=lax.Precision.HIGHEST)
        base_scr[...] = offs
        ends = offs[:, 0:1] + padded
        total = jnp.sum(padded, axis=0, keepdims=True)
        n_lanes = tinfo_ref.shape[1]
        tile_start = (lax.broadcasted_iota(jnp.int32, (N_EXPERTS, n_lanes), 1) * MOE_TILE).astype(F32)
        texp = jnp.sum(jnp.where(ends <= tile_start, 1.0, 0.0), axis=0, keepdims=True)
        last_active = jnp.sum(jnp.where(ends <= total - MOE_TILE, 1.0, 0.0), axis=0, keepdims=True)
        texp = jnp.minimum(texp, last_active)
        nact = jnp.broadcast_to(total * (1.0 / MOE_TILE), (1, n_lanes))
        rid = lax.broadcasted_iota(jnp.int32, tinfo_ref.shape, 0)
        tinfo_ref[...] = jnp.where(rid == 0, texp, jnp.where(rid == 1, nact, 0.0)).astype(jnp.int32)

    @pl.when(ph == 1)
    def _():
        incl = _dot(ohs.astype(BF16), tri_ref[...])
        slot = base_scr[:, 0:1] + incl - ohs
        pos1 = jnp.sum(oh1 * slot, axis=0, keepdims=True)
        pos2 = jnp.sum(oh2 * slot, axis=0, keepdims=True)
        rid = lax.broadcasted_iota(jnp.int32, pos_ref.shape, 0)
        pos_ref[...] = jnp.where(rid == 0, pos1, jnp.where(rid == 1, pos2, 0.0)).astype(jnp.int32)
        base_scr[...] += jnp.sum(ohs, axis=1, keepdims=True)


def _route(ri, tri, *, n_tile_lanes):
    n = ri.shape[1]
    rb = min(ROUTE_BLOCK, n)
    return pl.pallas_call(
        _route_kernel,
        grid=(2, n // rb),
        in_specs=[
            pl.BlockSpec((SUBLANES, rb), lambda p, i: (0, i)),
            pl.BlockSpec((rb, rb), lambda p, i: (0, 0)),
        ],
        out_specs=[
            pl.BlockSpec((SUBLANES, rb), lambda p, i: (0, i * p)),
            pl.BlockSpec((SUBLANES, n_tile_lanes), lambda p, i: (0, 0)),
        ],
        out_shape=[
            jax.ShapeDtypeStruct((SUBLANES, n), jnp.int32),
            jax.ShapeDtypeStruct((SUBLANES, n_tile_lanes), jnp.int32),
        ],
        scratch_shapes=[pltpu.VMEM((N_EXPERTS, rb), F32), pltpu.VMEM((N_EXPERTS, LANES), F32)],
        compiler_params=pltpu.CompilerParams(dimension_semantics=("arbitrary", "arbitrary")),
        name="route",
    )(ri, tri)


def _invert_kernel(tb, pos_hbm, zero_hbm, tok_hbm, idx_a, idx_b, tok_smem, idx_sems, out_sem):
    i = pl.program_id(0)
    n_steps = pl.num_programs(0)
    idx_bufs = (idx_a, idx_b)

    def idx_copy(step, par):
        return pltpu.make_async_copy(pos_hbm.at[step], idx_bufs[par], idx_sems.at[par])

    @pl.when(i == 0)
    def _():
        idx_copy(0, 0).start()
        clear = pltpu.make_async_copy(zero_hbm, tok_smem, out_sem)
        clear.start()
        clear.wait()

    def step(par):
        @pl.when(i + 1 < n_steps)
        def _():
            idx_copy(i + 1, 1 - par).start()

        idx_copy(i, par).wait()
        base = i * tb

        def fill(r, carry):
            for k in range(2):
                tok_smem[idx_bufs[par][k * tb + r]] = base + r
            return carry
        lax.fori_loop(0, tb, fill, 0, unroll=8)

    @pl.when(i % 2 == 0)
    def _():
        step(0)

    @pl.when(i % 2 == 1)
    def _():
        step(1)

    @pl.when(i == n_steps - 1)
    def _():
        out_cp = pltpu.make_async_copy(tok_smem, tok_hbm, out_sem)
        out_cp.start()
        out_cp.wait()


def _invert(pos_blocks, n_slots):
    tb = pos_blocks.shape[1] // 2
    return pl.pallas_call(
        functools.partial(_invert_kernel, tb),
        grid=(pos_blocks.shape[0],),
        in_specs=[pl.BlockSpec(memory_space=pl.ANY), pl.BlockSpec(memory_space=pl.ANY)],
        out_specs=pl.BlockSpec(memory_space=pl.ANY),
        out_shape=jax.ShapeDtypeStruct((n_slots,), jnp.int32),
        scratch_shapes=[pltpu.SMEM((2 * tb,), jnp.int32), pltpu.SMEM((2 * tb,), jnp.int32),
                        pltpu.SMEM((n_slots,), jnp.int32),
                        pltpu.SemaphoreType.DMA((2,)), pltpu.SemaphoreType.DMA(())],
        compiler_params=pltpu.CompilerParams(dimension_semantics=("arbitrary",)),
        name="invert_slots",
    )(pos_blocks, jnp.zeros((n_slots,), jnp.int32))


def _experts_kernel(texp_ref, nact_ref, tok_hbm, h_hbm, nffn_ref, wg_ref, wu_ref, wd_ref, ys_ref,
                    tok_a, tok_b, tok_c, xbuf, wg_b, wu_b, wd_b, idx_sems, row_sems):
    i = pl.program_id(0)
    n_steps = pl.num_programs(0)
    n_act = nact_ref[0]
    tok_bufs = (tok_a, tok_b, tok_c)
    depth = len(tok_bufs)

    def idx_copy(step, par):
        return pltpu.make_async_copy(tok_hbm.at[pl.ds(step * MOE_TILE, MOE_TILE)], tok_bufs[par],
                                     idx_sems.at[par])

    def row_copy(par, r, token):
        return pltpu.make_async_copy(h_hbm.at[pl.ds(token, 1)], xbuf.at[par, pl.ds(r, 1)],
                                     row_sems.at[par])

    def issue_rows(par):
        def body(g, carry):
            for k in range(2):
                r = 2 * g + k
                row_copy(par, r, tok_bufs[par][r]).start(priority=k)
            return carry
        lax.fori_loop(0, MOE_TILE // 2, body, 0, unroll=4)

    def drain_rows(par):
        pltpu.make_async_copy(h_hbm.at[pl.ds(0, MOE_TILE)], xbuf.at[par], row_sems.at[par]).wait()

    e = texp_ref[i]
    e_prev = texp_ref[jnp.maximum(i - 1, 0)]

    @pl.when((i == 0) | (e != e_prev))
    def _():
        wg_b[...] = wg_ref[0].astype(BF16)
        wu_b[...] = wu_ref[0].astype(BF16)
        wd_b[...] = wd_ref[0].astype(BF16)

    def step(par):
        ahead = (par + 2) % depth

        @pl.when(i == 0)
        def _():
            idx_copy(0, 0).start()
            idx_copy(0, 0).wait()
            issue_rows(0)

            @pl.when(n_steps > 1)
            def _():
                idx_copy(1, 1).start()
                idx_copy(1, 1).wait()

            @pl.when(n_act > 1)
            def _():
                issue_rows(1)

            @pl.when(n_steps > 2)
            def _():
                idx_copy(2, 2).start()

        @pl.when(i + 2 < n_steps)
        def _():
            idx_copy(i + 2, ahead).wait()

        @pl.when(i + 3 < n_steps)
        def _():
            idx_copy(i + 3, par).start()

        def compute_stages():
            hrow = xbuf[par]
            x = (hrow * lax.rsqrt(jnp.mean(hrow * hrow, axis=-1, keepdims=True) + NORM_EPS)
                 * nffn_ref[...]).astype(BF16)
            yield
            half = D_FF_EXPERT // 2
            gate, up = [], []
            for cb in range(2):
                gate.append(_dot(x, wg_b[:, cb * half:(cb + 1) * half]))
                yield
            for cb in range(2):
                up.append(_dot(x, wu_b[:, cb * half:(cb + 1) * half]))
                yield
            hdn = jnp.concatenate(
                [(gate[cb] * _sigmoid(gate[cb]) * up[cb]).astype(BF16) for cb in range(2)], axis=1)
            yield
            quarter = D_MODEL // 4
            for cb in range(4):
                ys_ref[:, cb * quarter:(cb + 1) * quarter] = _dot(
                    hdn, wd_b[:, cb * quarter:(cb + 1) * quarter])
                yield

        def issue_stages(per_stage=32):
            for r0 in range(0, MOE_TILE, per_stage):
                for r in range(r0, r0 + per_stage):
                    row_copy(ahead, r, tok_bufs[ahead][r]).start(priority=r % 2)
                yield

        @pl.when(i + 2 < n_act)
        def _():
            drain_rows(par)
            _interleave(compute_stages(), issue_stages())

        @pl.when((i < n_act) & (i + 2 >= n_act))
        def _():
            drain_rows(par)
            _interleave(compute_stages())

        @pl.when(i >= n_act)
        def _():
            ys_ref[...] = jnp.zeros_like(ys_ref)

    for par in range(depth):
        pl.when(i % depth == par)(functools.partial(step, par))


def _experts(texp, nact, tok, h, nffn, wg, wu, wd):
    n_tiles = tok.shape[0] // MOE_TILE
    w_blk = lambda i, texp, nact: (texp[i], 0, 0)
    grid_spec = pltpu.PrefetchScalarGridSpec(
        num_scalar_prefetch=2,
        grid=(n_tiles,),
        in_specs=[
            pl.BlockSpec(memory_space=pl.ANY),
            pl.BlockSpec(memory_space=pl.ANY),
            pl.BlockSpec((1, D_MODEL), lambda i, texp, nact: (0, 0)),
            pl.BlockSpec((1, D_MODEL, D_FF_EXPERT), w_blk),
            pl.BlockSpec((1, D_MODEL, D_FF_EXPERT), w_blk),
            pl.BlockSpec((1, D_FF_EXPERT, D_MODEL), w_blk),
        ],
        out_specs=pl.BlockSpec((MOE_TILE, D_MODEL), lambda i, texp, nact: (i, 0)),
        scratch_shapes=[pltpu.SMEM((MOE_TILE,), jnp.int32), pltpu.SMEM((MOE_TILE,), jnp.int32),
                        pltpu.SMEM((MOE_TILE,), jnp.int32),
                        pltpu.VMEM((3, MOE_TILE, D_MODEL), F32),
                        pltpu.VMEM((D_MODEL, D_FF_EXPERT), BF16),
                        pltpu.VMEM((D_MODEL, D_FF_EXPERT), BF16),
                        pltpu.VMEM((D_FF_EXPERT, D_MODEL), BF16),
                        pltpu.SemaphoreType.DMA((3,)), pltpu.SemaphoreType.DMA((3,))],
    )
    return pl.pallas_call(
        _experts_kernel,
        grid_spec=grid_spec,
        out_shape=jax.ShapeDtypeStruct((tok.shape[0], D_MODEL), F32),
        compiler_params=pltpu.CompilerParams(
            dimension_semantics=("arbitrary",), vmem_limit_bytes=VMEM_LIMIT),
        name="experts",
    )(texp, nact, tok, h, nffn, wg, wu, wd)


def _combine_kernel(pos_hbm, ys_hbm, h_ref, rf_ref, nfin_ref, out_ref,
                    idx_a, idx_b, idx_c, ybuf, idx_sems, row_sems):
    i = pl.program_id(0)
    n_steps = pl.num_programs(0)
    tb = h_ref.shape[0]
    idx_bufs = (idx_a, idx_b, idx_c)
    depth = len(idx_bufs)

    def idx_copy(step, par):
        return pltpu.make_async_copy(pos_hbm.at[step], idx_bufs[par], idx_sems.at[par])

    def issue_rows(par):
        def body(g, carry):
            for sub in range(SUBLANES):
                r = g * SUBLANES + sub
                for k in range(2):
                    pltpu.make_async_copy(
                        ys_hbm.at[pl.ds(idx_bufs[par][k * tb + r], 1)],
                        ybuf.at[par, k, pl.ds(r, 1)],
                        row_sems.at[par]).start(priority=k)
            return carry
        lax.fori_loop(0, tb // SUBLANES, body, 0)

    def drain_rows(par):
        for k in range(2):
            pltpu.make_async_copy(ys_hbm.at[pl.ds(0, tb)], ybuf.at[par, k], row_sems.at[par]).wait()

    def step(par):
        ahead = (par + 2) % depth

        @pl.when(i == 0)
        def _():
            idx_copy(0, 0).start()
            idx_copy(0, 0).wait()
            issue_rows(0)

            @pl.when(n_steps > 1)
            def _():
                idx_copy(1, 1).start()
                idx_copy(1, 1).wait()
                issue_rows(1)

            @pl.when(n_steps > 2)
            def _():
                idx_copy(2, 2).start()

        @pl.when(i + 2 < n_steps)
        def _():
            idx_copy(i + 2, ahead).wait()

        @pl.when(i + 3 < n_steps)
        def _():
            idx_copy(i + 3, par).start()

        n_stage = 8
        rows = tb // n_stage

        def compute_stages():
            rf = rf_ref[...]
            rf_cols = jnp.transpose(jnp.concatenate(
                [rf, jnp.zeros((LANES - SUBLANES, tb), F32)], axis=0))
            for st in range(n_stage):
                rs = slice(st * rows, (st + 1) * rows)
                hh = (h_ref[rs, :] + rf_cols[rs, 0:1] * ybuf[par, 0, rs, :]
                      + rf_cols[rs, 1:2] * ybuf[par, 1, rs, :])
                out_ref[rs, :] = (hh * lax.rsqrt(jnp.mean(hh * hh, axis=-1, keepdims=True) + NORM_EPS)
                                  * nfin_ref[...])
                yield

        def issue_stages():
            for st in range(n_stage):
                for r in range(st * rows, (st + 1) * rows):
                    for k in range(2):
                        pltpu.make_async_copy(
                            ys_hbm.at[pl.ds(idx_bufs[ahead][k * tb + r], 1)],
                            ybuf.at[ahead, k, pl.ds(r, 1)],
                            row_sems.at[ahead]).start(priority=k)
                yield

        drain_rows(par)

        @pl.when(i + 2 < n_steps)
        def _():
            _interleave(issue_stages(), compute_stages())

        @pl.when(i + 2 >= n_steps)
        def _():
            _interleave(compute_stages())

    for par in range(depth):
        pl.when(i % depth == par)(functools.partial(step, par))


def _combine(pos_blocks, ys, h, rf, nfin):
    n = h.shape[0]
    tb = pos_blocks.shape[1] // 2
    return pl.pallas_call(
        _combine_kernel,
        grid=(n // tb,),
        in_specs=[
            pl.BlockSpec(memory_space=pl.ANY),
            pl.BlockSpec(memory_space=pl.ANY),
            pl.BlockSpec((tb, D_MODEL), lambda i: (i, 0)),
            pl.BlockSpec((SUBLANES, tb), lambda i: (0, i)),
            pl.BlockSpec((1, D_MODEL), lambda i: (0, 0)),
        ],
        out_specs=pl.BlockSpec((tb, D_MODEL), lambda i: (i, 0)),
        out_shape=jax.ShapeDtypeStruct((n, D_MODEL), F32),
        scratch_shapes=[pltpu.SMEM((2 * tb,), jnp.int32), pltpu.SMEM((2 * tb,), jnp.int32),
                        pltpu.SMEM((2 * tb,), jnp.int32),
                        pltpu.VMEM((3, 2, tb, D_MODEL), F32),
                        pltpu.SemaphoreType.DMA((3,)), pltpu.SemaphoreType.DMA((3,))],
        compiler_params=pltpu.CompilerParams(dimension_semantics=("arbitrary",)),
        name="combine",
    )(pos_blocks, ys, h, rf, nfin)


def _moe(h, rf, ri, nffn, wg, wu, wd, nfin):
    n = h.shape[0]
    max_tiles = _max_tiles(n)
    n_tile_lanes = -(-max_tiles // LANES) * LANES
    rb = min(ROUTE_BLOCK, n)
    tri = jnp.triu(jnp.ones((rb, rb), BF16))
    pos, tinfo = _route(ri, tri, n_tile_lanes=n_tile_lanes)
    texp = tinfo[0, :max_tiles]
    nact = tinfo[1, :1]
    tb = min(ROW_BLOCK, n)
    pos_blocks = pos[0:2].reshape(2, n // tb, tb).transpose(1, 0, 2).reshape(n // tb, 2 * tb)
    tok = _invert(pos_blocks, max_tiles * MOE_TILE)
    ys = _experts(texp, nact, tok, h, nffn, wg, wu, wd)
    return _combine(pos_blocks, ys, h, rf, nfin)


def _pad_lanes_row(v):
    return jnp.pad(v.astype(F32), (0, LANES - v.shape[0]))[None, :]


def _block_forward(x, meta_tokens, norm_mix_w, w_in, conv_w, pool_w, pool_scale, a_log, dt_bias,
                   dn_norm_w, w_out, norm_ffn_w, router_group_w, router_group_b, router_expert_w,
                   router_expert_b, expert_w_gate, expert_w_up, expert_w_down, norm_final_w,
                   *, mixer_rows):
    bsz, seq, _ = x.shape
    n = bsz * seq
    x2d = x.reshape(n, D_MODEL)

    ab0 = 5 * D_MODEL
    w_main = jnp.concatenate([w_in[:, :ab0], w_in[:, ab0 + 2 * DN_HEADS:]], axis=1).astype(BF16)
    wab = jnp.pad(w_in[:, ab0:ab0 + 2 * DN_HEADS], ((0, 0), (0, LANES - 2 * DN_HEADS))).astype(BF16)
    wabt = wab.T
    nmix = norm_mix_w[None, :]
    alog = _pad_lanes_row(a_log)
    dtb = _pad_lanes_row(dt_bias)
    alogt = jnp.broadcast_to(alog.T, (LANES, LANES))
    dtbt = jnp.broadcast_to(dtb.T, (LANES, LANES))
    wr = jnp.zeros((D_MODEL, LANES), F32)
    wr = wr.at[:, 0:N_EXPERT_GROUPS].set(router_group_w)
    wr = wr.at[:, SUBLANES:SUBLANES + N_EXPERTS].set(router_expert_w)
    br = jnp.zeros((LANES,), F32)
    br = br.at[0:N_EXPERT_GROUPS].set(router_group_b)
    br = br.at[SUBLANES:SUBLANES + N_EXPERTS].set(router_expert_b)
    params = (nmix, w_main, wab, wabt,
              conv_w, pool_w.astype(BF16), pool_scale[None, :], alog, dtb, alogt, dtbt,
              dn_norm_w[None, :], w_out.astype(BF16), norm_ffn_w[None, :],
              wr.T, jnp.broadcast_to(br[:, None], (LANES, LANES)))

    pad_rows = CHUNK - N_META
    xm = jnp.concatenate([jnp.zeros((pad_rows, D_MODEL), F32), meta_tokens], axis=0)
    zeros_s = jnp.zeros((DN_HEADS, DN_HEAD_DIM, DN_HEAD_DIM), F32)
    zeros_ph = jnp.zeros((POOL_HALO, D_MODEL), F32)
    zeros_ch = jnp.zeros((CONV_HALO, 3 * D_MODEL), F32)
    meta_out = _mixer(xm, zeros_s, zeros_ph, zeros_ch, params,
                      batch=1, t_rows=CHUNK, pad_rows=pad_rows)
    s_meta, ph_meta, ch_meta = meta_out[3], meta_out[4], meta_out[5]

    h, rf, ri, _, _, _ = _mixer(x2d, s_meta, ph_meta, ch_meta, params,
                                batch=bsz, t_rows=mixer_rows, pad_rows=0)
    wg = expert_w_gate.reshape(N_EXPERTS, D_MODEL, D_FF_EXPERT)
    wu = expert_w_up.reshape(N_EXPERTS, D_MODEL, D_FF_EXPERT)
    wd = expert_w_down.reshape(N_EXPERTS, D_FF_EXPERT, D_MODEL)
    out = _moe(h, rf, ri, norm_ffn_w[None, :], wg, wu, wd, norm_final_w[None, :])
    return out.reshape(bsz, seq, D_MODEL)


def kernel(x, meta_tokens, norm_mix_w, w_in, conv_w, pool_w, pool_scale, a_log, dt_bias, dn_norm_w, w_out, norm_ffn_w, router_group_w, router_group_b, router_expert_w, router_expert_b, expert_w_gate, expert_w_up, expert_w_down, norm_final_w):
    assert norm_mix_w.shape[0] == 1, "single-layer block"
    seq = x.shape[1]
    return _block_forward(
        x, meta_tokens, norm_mix_w[0], w_in[0], conv_w[0], pool_w[0], pool_scale[0], a_log[0],
        dt_bias[0], dn_norm_w[0], w_out[0], norm_ffn_w[0], router_group_w[0], router_group_b[0],
        router_expert_w[0], router_expert_b[0], expert_w_gate[0], expert_w_up[0], expert_w_down[0],
        norm_final_w,
        mixer_rows=min(256, seq))
```

```python
import functools
import math

import jax
import jax.numpy as jnp
from jax import lax
from jax.experimental import pallas as pl
from jax.experimental.pallas import tpu as pltpu

F32 = jnp.float32
BF16 = jnp.bfloat16

D_MODEL = 1024
N_META = 16
POOL_WINDOWS = (2, 4, 8, 16)
POOL_GROUP_DIM = 256
DN_HEADS = 8
DN_HEAD_DIM = 128
CONV_WIDTH = 4
CHUNK = 64
N_EXPERT_GROUPS = 4
EXPERTS_PER_GROUP = 8
N_EXPERTS = 32
D_FF_EXPERT = 512
NORM_EPS = 1e-6

LANES = 128
SUBLANES = 8
P_MAIN_COLS = 7 * D_MODEL
POOL_HALO = 16
CONV_HALO = 8
VMEM_LIMIT = 56 * 1024 * 1024
MOE_TILE = 512
ROUTE_BLOCK = 512
ROW_BLOCK = 256


def _dot(a, b):
    return jnp.dot(a, b, preferred_element_type=F32)


def _dot_nt(a, b):
    return lax.dot_general(a, b, (((1,), (1,)), ((), ())), preferred_element_type=F32)


def _dot_tn(a, b):
    return lax.dot_general(a, b, (((0,), (0,)), ((), ())), preferred_element_type=F32)


def _sigmoid(x):
    return 1.0 / (1.0 + jnp.exp(-x))


def _interleave(*stage_generators):
    live = {i: g for i, g in enumerate(stage_generators)}
    results = [None] * len(stage_generators)
    while live:
        for i in list(live):
            try:
                next(live[i])
            except StopIteration as done:
                results[i] = done.value
                del live[i]
    return results


def _intra_chunk(qs, ks, vs, bcols, gcols, grows):
    c = qs[0].shape[0]
    hs = range(len(qs))
    ii = lax.broadcasted_iota(jnp.int32, (c, c), 0)
    jj = lax.broadcasted_iota(jnp.int32, (c, c), 1)
    dec =[jnp.exp(jnp.where(ii >= jj, gcols[h] - grows[h], -jnp.inf)) for h in hs]
    kb = [ks[h].astype(BF16) for h in hs]
    qkb = [jnp.concatenate([qs[h].astype(BF16), kb[h]], axis=0) for h in hs]
    qkk = [_dot_nt(qkb[h], kb[h]) for h in hs]
    yield
    egc = [jnp.exp(gcols[h]) for h in hs]
    pw = [jnp.where(ii > jj, -(bcols[h] * qkk[h][c:] * dec[h]), 0.0) for h in hs]
    sol = [jnp.concatenate([vs[h] * bcols[h], ks[h] * (bcols[h] * egc[h])], axis=1) for h in hs]
    width = 2 * DN_HEAD_DIM
    levels = int(math.log2(c))
    for lvl in range(levels):
        pb = [pw[h].astype(BF16) for h in hs]
        if lvl < levels - 1:
            r = [_dot(pb[h], jnp.concatenate([sol[h].astype(BF16), pb[h]], axis=1)) for h in hs]
            sol = [sol[h] + r[h][:, :width] for h in hs]
            pw = [r[h][:, width:] for h in hs]
        else:
            sol = [sol[h] + _dot(pb[h], sol[h].astype(BF16)) for h in hs]
        yield
    qd = [qs[h] * egc[h] for h in hs]
    glast = [gcols[h][c - 1:c, :] for h in hs]
    kd = [(ks[h] * jnp.exp(glast[h] - gcols[h])).astype(BF16) for h in hs]
    u = [sol[h][:, :DN_HEAD_DIM] for h in hs]
    wq = [jnp.concatenate([sol[h][:, DN_HEAD_DIM:], qd[h]], axis=0).astype(BF16) for h in hs]
    qkd = [(qkk[h][:c] * dec[h]).astype(BF16) for h in hs]
    cd = [jnp.broadcast_to(jnp.exp(glast[h]), (1, DN_HEAD_DIM)) for h in hs]
    return u, wq, qkd, kd, cd


def _scan_chunk(u, wq, qkd, kd, cd, s):
    c = u[0].shape[0]
    hs = range(len(u))
    sb = [s[h].astype(BF16) for h in hs]
    ws = [_dot(wq[h], sb[h]) for h in hs]
    yield
    vb = [(u[h] - ws[h][:c]).astype(BF16) for h in hs]
    o = [ws[h][c:] + _dot(qkd[h], vb[h]) for h in hs]
    s_new = [s[h] * cd[h] + _dot_tn(kd[h], vb[h]) for h in hs]
    yield
    return o, s_new


def _mixer_kernel(pad_rows, t_rows,
                  x_ref, s0_ref, ph0_ref, ch0_ref, nmix_ref, win_ref, wab_ref, wabt_ref,
                  convw_ref, poolw_ref, pscale_ref, alog_ref, dtb_ref, alogt_ref, dtbt_ref,
                  dnw_ref, wout_ref, nffn_ref, wrt_ref, brt_ref,
                  h_ref, rf_ref, ri_ref, s_out_ref, ph_out_ref, ch_out_ref,
                  s_scr, pool_buf, conv_buf, un_scr, ypool_scr, qkv_scr, o_scr, beta_scr, gcol_scr, grow_scr,
                  u_scr, wq_scr, qkd_scr, kd_scr, cd_scr):
    t = t_rows
    j = pl.program_id(1)
    n_chunks = t // CHUNK

    @pl.when(j == 0)
    def _():
        s_scr[...] = s0_ref[...]
        pool_buf[0:POOL_HALO, :] = ph0_ref[...]
        conv_buf[0:CONV_HALO, :] = ch0_ref[...]

    x = x_ref[...]
    un = (x * lax.rsqrt(jnp.mean(x * x, axis=-1, keepdims=True) + NORM_EPS) * nmix_ref[...]).astype(BF16)

    un_scr[...] = un
    sub = POOL_GROUP_DIM
    n_sub = D_MODEL // sub

    def project(col0):
        return _dot(un, win_ref[:, col0:col0 + sub])

    def pool_group(gi):
        win = POOL_WINDOWS[gi]
        cs = slice(gi * sub, (gi + 1) * sub)
        acc = pool_buf[:, cs]
        shift = 1
        while shift < win:
            acc = acc + pltpu.roll(acc, shift, axis=0)
            shift *= 2
        pooled = acc[POOL_HALO:, :] * (1.0 / win) - pool_buf[POOL_HALO:POOL_HALO + t, cs]
        ypool_scr[:, cs] = _dot(pooled.astype(BF16), poolw_ref[gi]) * pscale_ref[:, cs]

    def conv_sub(col0):
        cs = slice(col0, col0 + sub)
        acc = convw_ref[CONV_WIDTH - 1:CONV_WIDTH, cs] * conv_buf[CONV_HALO:CONV_HALO + t, cs]
        for kk in range(CONV_WIDTH - 1):
            off = CONV_HALO - (CONV_WIDTH - 1) + kk
            acc = acc + convw_ref[kk:kk + 1, cs] * conv_buf[off:off + t, cs]
        act = acc * _sigmoid(acc)
        if col0 >= 2 * D_MODEL:
            qkv_scr[:, cs] = act
            return
        for hh in range(sub // DN_HEAD_DIM):
            part = act[:, hh * DN_HEAD_DIM:(hh + 1) * DN_HEAD_DIM]
            nrm = lax.rsqrt(jnp.sum(part * part, axis=-1, keepdims=True) + NORM_EPS)
            if col0 < D_MODEL:
                nrm = nrm * (DN_HEAD_DIM ** -0.5)
            qkv_scr[:, col0 + hh * DN_HEAD_DIM:col0 + (hh + 1) * DN_HEAD_DIM] = part * nrm

    for si in range(n_sub):
        pool_buf[POOL_HALO:POOL_HALO + t, si * sub:(si + 1) * sub] = project(si * sub)
    for si in range(n_sub):
        conv_buf[CONV_HALO:CONV_HALO + t, si * sub:(si + 1) * sub] = project(D_MODEL + si * sub)
        pool_group(si)
    for blk in range(1, 3):
        for si in range(n_sub):
            c0 = blk * D_MODEL + si * sub
            conv_buf[CONV_HALO:CONV_HALO + t, c0:c0 + sub] = project(D_MODEL + c0)
            conv_sub(c0 - D_MODEL)
    pab = _dot(un, wab_ref[...])
    pabt = _dot_nt(wabt_ref[...], un)

    gcol_all = -jnp.exp(alog_ref[...]) * _softplus(pab + dtb_ref[...])
    beta_all = _sigmoid(pab)
    ab_rows = 2 * DN_HEADS
    grow_all = (-jnp.exp(alogt_ref[0:ab_rows, 0:1])
                * _softplus(pabt[0:ab_rows, :] + dtbt_ref[0:ab_rows, 0:1]))
    if pad_rows:
        rid = lax.broadcasted_iota(jnp.int32, (t, LANES), 0)
        gcol_all = jnp.where(rid >= pad_rows, gcol_all, 0.0)
        beta_all = jnp.where(rid >= pad_rows, beta_all, 0.0)
        cid = lax.broadcasted_iota(jnp.int32, (ab_rows, t), 1)
        grow_all = jnp.where(cid >= pad_rows, grow_all, 0.0)
    beta_scr[...] = beta_all
    in_chunk_r = lax.broadcasted_iota(jnp.int32, (t, LANES), 0) % CHUNK
    in_chunk_c = lax.broadcasted_iota(jnp.int32, (ab_rows, t), 1) % CHUNK
    lane_scan = t % LANES == 0
    shift = 1
    while shift < CHUNK:
        gcol_all = gcol_all + jnp.where(in_chunk_r >= shift, pltpu.roll(gcol_all, shift, axis=0), 0.0)
        if lane_scan:
            grow_all = grow_all + jnp.where(in_chunk_c >= shift, pltpu.roll(grow_all, shift, axis=1), 0.0)
        shift *= 2
    gcol_scr[...] = gcol_all
    if not lane_scan:
        ci = lax.broadcasted_iota(jnp.int32, (t, t), 0)
        cj = lax.broadcasted_iota(jnp.int32, (t, t), 1)
        tri_u = jnp.where((ci <= cj) & (ci // CHUNK == cj // CHUNK), 1.0, 0.0).astype(F32)
        grow_all = jnp.dot(grow_all, tri_u, preferred_element_type=F32,
                           precision=lax.Precision.HIGHEST)
    for c in range(n_chunks):
        grow_scr[c] = grow_all[:, c * CHUNK:(c + 1) * CHUNK]
    for si in range(n_sub):
        conv_sub(2 * D_MODEL + si * sub)

    heads = range(DN_HEADS)

    def head_cols(base, hh):
        return slice(base + hh * DN_HEAD_DIM, base + (hh + 1) * DN_HEAD_DIM)

    def intra_stage(chunks):
        qs, ks, vs, bcols, gcols, grows = [], [], [], [], [], []
        for c in chunks:
            rs = slice(c * CHUNK, (c + 1) * CHUNK)
            gcol_c = gcol_scr[rs, :]
            beta_c = beta_scr[rs, :]
            grow_c = grow_scr[c]
            for hh in heads:
                qs.append(qkv_scr[rs, head_cols(0, hh)])
                ks.append(qkv_scr[rs, head_cols(D_MODEL, hh)])
                vs.append(qkv_scr[rs, head_cols(2 * D_MODEL, hh)])
                bcols.append(beta_c[:, DN_HEADS + hh:DN_HEADS + hh + 1])
                gcols.append(gcol_c[:, hh:hh + 1])
                grows.append(grow_c[hh:hh + 1, :])
        u, wq, qkd, kd, cd = yield from _intra_chunk(qs, ks, vs, bcols, gcols, grows)
        for ci, c in enumerate(chunks):
            for hh in heads:
                idx = ci * DN_HEADS + hh
                u_scr[c, hh] = u[idx]
                wq_scr[c, hh] = wq[idx]
                qkd_scr[c, hh] = qkd[idx]
                kd_scr[c, hh] = kd[idx]
            cd_scr[c] = jnp.concatenate(cd[ci * DN_HEADS:(ci + 1) * DN_HEADS], axis=0)

    def scan_stage(chunks):
        for c in chunks:
            cds = cd_scr[c]
            o, s_new = yield from _scan_chunk(
                [u_scr[c, hh] for hh in heads], [wq_scr[c, hh] for hh in heads],
                [qkd_scr[c, hh] for hh in heads], [kd_scr[c, hh] for hh in heads],
                [cds[hh:hh + 1, :] for hh in heads], [s_scr[hh] for hh in heads])
            for hh in heads:
                s_scr[hh] = s_new[hh]
                o_scr[c * CHUNK:(c + 1) * CHUNK, head_cols(0, hh)] = o[hh]

    def output_stage(r0, nr):
        rs = slice(r0, r0 + nr)
        unh = un_scr[rs, :]

        def gate_proj(col0):
            return [_dot(unh, win_ref[:, col0 + si * sub:col0 + (si + 1) * sub]) for si in range(n_sub)]

        z = gate_proj(4 * D_MODEL)
        yield
        y_dn = []
        for hh in heads:
            o = o_scr[rs, head_cols(0, hh)]
            o = o * lax.rsqrt(jnp.mean(o * o, axis=-1, keepdims=True) + NORM_EPS) * dnw_ref[...]
            per = sub // DN_HEAD_DIM
            zz = z[hh // per][:, (hh % per) * DN_HEAD_DIM:(hh % per + 1) * DN_HEAD_DIM]
            y_dn.append(o * (zz * _sigmoid(zz)))
        g_pool = gate_proj(5 * D_MODEL)
        yield
        g_dn = gate_proj(6 * D_MODEL)
        yield
        merged = []
        per = sub // DN_HEAD_DIM
        for si in range(n_sub):
            y_dn_s = jnp.concatenate(y_dn[si * per:(si + 1) * per], axis=1)
            merged.append((_sigmoid(g_pool[si]) * ypool_scr[rs, si * sub:(si + 1) * sub]
                           + _sigmoid(g_dn[si]) * y_dn_s).astype(BF16))
        h = x_ref[rs, :] + _dot(jnp.concatenate(merged, axis=1), wout_ref[...])
        h_ref[rs, :] = h
        yield
        xt = h * lax.rsqrt(jnp.mean(h * h, axis=-1, keepdims=True) + NORM_EPS) * nffn_ref[...]
        n_logit_rows = SUBLANES + N_EXPERTS
        logits = lax.dot_general(wrt_ref[0:n_logit_rows, :], xt, (((1,), (1,)), ((), ())),
                                 preferred_element_type=F32, precision=lax.Precision.HIGHEST)
        yield
        logits = logits + brt_ref[0:n_logit_rows, 0:1]
        rid8 = lax.broadcasted_iota(jnp.int32, (SUBLANES, nr), 0)
        lg = jnp.where(rid8 < N_EXPERT_GROUPS, logits[0:SUBLANES, :], -jnp.inf)
        gmax = jnp.max(lg, axis=0, keepdims=True)
        g_idx = jnp.min(jnp.where(lg == gmax, rid8, SUBLANES), axis=0, keepdims=True)
        p_grp = 1.0 / jnp.sum(jnp.exp(lg - gmax), axis=0, keepdims=True)
        sel = jnp.zeros((EXPERTS_PER_GROUP, nr), F32)
        for gi in range(N_EXPERT_GROUPS):
            e0 = SUBLANES + gi * EXPERTS_PER_GROUP
            sel = jnp.where(g_idx == gi, logits[e0:e0 + EXPERTS_PER_GROUP, :], sel)
        m1 = jnp.max(sel, axis=0, keepdims=True)
        i1 = jnp.min(jnp.where(sel == m1, rid8, SUBLANES), axis=0, keepdims=True)
        sel2 = jnp.where(rid8 == i1, -jnp.inf, sel)
        m2 = jnp.max(sel2, axis=0, keepdims=True)
        i2 = jnp.min(jnp.where(sel2 == m2, rid8, SUBLANES), axis=0, keepdims=True)
        e21 = jnp.exp(m2 - m1)
        w1 = 1.0 / (1.0 + e21)
        c1 = p_grp * w1
        c2 = p_grp * (e21 * w1)
        id1 = g_idx * EXPERTS_PER_GROUP + i1
        id2 = g_idx * EXPERTS_PER_GROUP + i2
        rf_ref[:, rs] = jnp.where(rid8 == 0, c1, jnp.where(rid8 == 1, c2, 0.0))
        ri_ref[:, rs] = jnp.where(rid8 == 0, id1, jnp.where(rid8 == 1, id2, 0))

    group = 2 if n_chunks % 2 == 0 else 1
    groups = [list(range(g0, g0 + group)) for g0 in range(0, n_chunks, group)]
    _interleave(intra_stage(groups[0]))
    for gi in range(1, len(groups)):
        _interleave(intra_stage(groups[gi]), scan_stage(groups[gi - 1]))
    rows_before_last = groups[-1][0] * CHUNK
    if rows_before_last >= LANES:
        _interleave(scan_stage(groups[-1]), output_stage(0, rows_before_last))
        _interleave(output_stage(rows_before_last, t - rows_before_last))
    else:
        _interleave(scan_stage(groups[-1]))
        _interleave(output_stage(0, t))

    pool_buf[0:POOL_HALO, :] = pool_buf[t:t + POOL_HALO, :]
    conv_buf[0:CONV_HALO, :] = conv_buf[t:t + CONV_HALO, :]
    s_out_ref[...] = s_scr[...]
    ph_out_ref[...] = pool_buf[0:POOL_HALO, :]
    ch_out_ref[...] = conv_buf[0:CONV_HALO, :]


def _softplus(x):
    return jnp.maximum(x, 0.0) + jnp.log1p(jnp.exp(-jnp.abs(x)))


def _mixer(x2d, s0, ph0, ch0, params, *, batch, t_rows, pad_rows):
    n = x2d.shape[0]
    n_t = n // batch // t_rows
    t = t_rows
    row_blk = lambda b, j: (b * n_t + j, 0)
    col_blk = lambda b, j: (0, b * n_t + j)
    const2 = lambda b, j: (0, 0)
    const3 = lambda b, j: (0, 0, 0)
    (nmix, w_main, wab, wabt,
     convw, poolw, pscale, alog, dtb, alogt, dtbt, dnw, wout, nffn, wrt, brt) = params
    in_specs = [
        pl.BlockSpec((t, D_MODEL), row_blk),
        pl.BlockSpec((DN_HEADS, DN_HEAD_DIM, DN_HEAD_DIM), const3),
        pl.BlockSpec((POOL_HALO, D_MODEL), const2),
        pl.BlockSpec((CONV_HALO, 3 * D_MODEL), const2),
        pl.BlockSpec(nmix.shape, const2),
        pl.BlockSpec(w_main.shape, const2, pipeline_mode=pl.Buffered(1)),
        pl.BlockSpec(wab.shape, const2),
        pl.BlockSpec(wabt.shape, const2),
        pl.BlockSpec(convw.shape, const2),
        pl.BlockSpec(poolw.shape, const3),
        pl.BlockSpec(pscale.shape, const2),
        pl.BlockSpec(alog.shape, const2),
        pl.BlockSpec(dtb.shape, const2),
        pl.BlockSpec(alogt.shape, const2),
        pl.BlockSpec(dtbt.shape, const2),
        pl.BlockSpec(dnw.shape, const2),
        pl.BlockSpec(wout.shape, const2),
        pl.BlockSpec(nffn.shape, const2),
        pl.BlockSpec(wrt.shape, const2),
        pl.BlockSpec(brt.shape, const2),
    ]
    out_specs = [
        pl.BlockSpec((t, D_MODEL), row_blk),
        pl.BlockSpec((SUBLANES, t), col_blk),
        pl.BlockSpec((SUBLANES, t), col_blk),
        pl.BlockSpec((DN_HEADS, DN_HEAD_DIM, DN_HEAD_DIM), const3),
        pl.BlockSpec((POOL_HALO, D_MODEL), const2),
        pl.BlockSpec((CONV_HALO, 3 * D_MODEL), const2),
    ]
    out_shape = [
        jax.ShapeDtypeStruct((n, D_MODEL), F32),
        jax.ShapeDtypeStruct((SUBLANES, n), F32),
        jax.ShapeDtypeStruct((SUBLANES, n), jnp.int32),
        jax.ShapeDtypeStruct((DN_HEADS, DN_HEAD_DIM, DN_HEAD_DIM), F32),
        jax.ShapeDtypeStruct((POOL_HALO, D_MODEL), F32),
        jax.ShapeDtypeStruct((CONV_HALO, 3 * D_MODEL), F32),
    ]
    scratch = [
        pltpu.VMEM((DN_HEADS, DN_HEAD_DIM, DN_HEAD_DIM), F32),
        pltpu.VMEM((t + POOL_HALO, D_MODEL), F32),
        pltpu.VMEM((t + CONV_HALO, 3 * D_MODEL), F32),
        pltpu.VMEM((t, D_MODEL), BF16),
        pltpu.VMEM((t, D_MODEL), F32),
        pltpu.VMEM((t, 3 * D_MODEL), F32),
        pltpu.VMEM((t, D_MODEL), F32),
        pltpu.VMEM((t, LANES), F32),
        pltpu.VMEM((t, LANES), F32),
        pltpu.VMEM((t // CHUNK, 2 * DN_HEADS, CHUNK), F32),
        pltpu.VMEM((t // CHUNK, DN_HEADS, CHUNK, DN_HEAD_DIM), F32),
        pltpu.VMEM((t // CHUNK, DN_HEADS, 2 * CHUNK, DN_HEAD_DIM), BF16),
        pltpu.VMEM((t // CHUNK, DN_HEADS, CHUNK, CHUNK), BF16),
        pltpu.VMEM((t // CHUNK, DN_HEADS, CHUNK, DN_HEAD_DIM), BF16),
        pltpu.VMEM((t // CHUNK, DN_HEADS, DN_HEAD_DIM), F32),
    ]
    return pl.pallas_call(
        functools.partial(_mixer_kernel, pad_rows, t_rows),
        grid=(batch, n_t),
        in_specs=in_specs,
        out_specs=out_specs,
        out_shape=out_shape,
        scratch_shapes=scratch,
        compiler_params=pltpu.CompilerParams(
            dimension_semantics=("arbitrary", "arbitrary"), vmem_limit_bytes=VMEM_LIMIT),
        name="mixer_meta" if pad_rows else "mixer",
    )(x2d, s0, ph0, ch0, *params)


def _max_tiles(n_tokens):
    return 2 * n_tokens // MOE_TILE + N_EXPERTS


def _route_kernel(ri_ref, tri_ref, pos_ref, tinfo_ref, cnt_scr, base_scr):
    ph = pl.program_id(0)
    i = pl.program_id(1)
    rb = ri_ref.shape[1]
    eid = lax.broadcasted_iota(jnp.int32, (N_EXPERTS, rb), 0)
    oh1 = jnp.where(ri_ref[0:1, :] == eid, 1.0, 0.0).astype(F32)
    oh2 = jnp.where(ri_ref[1:2, :] == eid, 1.0, 0.0).astype(F32)
    ohs = oh1 + oh2

    @pl.when((ph == 0) & (i == 0))
    def _():
        cnt_scr[...] = jnp.zeros_like(cnt_scr)

    @pl.when(ph == 0)
    def _():
        cnt_scr[...] += ohs

    @pl.when((ph == 1) & (i == 0))
    def _():
        counts = jnp.sum(cnt_scr[...], axis=1, keepdims=True)
        padded = jnp.floor((counts + (MOE_TILE - 1)) * (1.0 / MOE_TILE)) * MOE_TILE
        padded_b = jnp.broadcast_to(padded, (N_EXPERTS, LANES))
        r = lax.broadcasted_iota(jnp.int32, (N_EXPERTS, N_EXPERTS), 0)
        c = lax.broadcasted_iota(jnp.int32, (N_EXPERTS, N_EXPERTS), 1)
        strict = jnp.where(r > c, 1.0, 0.0).astype(F32)
        offs = jnp.dot(strict, padded_b, preferred_element_type=F32,
                       precision=lax.Precision.HIGHEST)
        base_scr[...] = offs
        ends = offs[:, 0:1] + padded
        total = jnp.sum(padded, axis=0, keepdims=True)
        n_lanes = tinfo_ref.shape[1]
        tile_start = (lax.broadcasted_iota(jnp.int32, (N_EXPERTS, n_lanes), 1) * MOE_TILE).astype(F32)
        texp = jnp.sum(jnp.where(ends <= tile_start, 1.0, 0.0), axis=0, keepdims=True)
        last_active = jnp.sum(jnp.where(ends <= total - MOE_TILE, 1.0, 0.0), axis=0, keepdims=True)
        texp = jnp.minimum(texp, last_active)
        nact = jnp.broadcast_to(total * (1.0 / MOE_TILE), (1, n_lanes))
        rid = lax.broadcasted_iota(jnp.int32, tinfo_ref.shape, 0)
        tinfo_ref[...] = jnp.where(rid == 0, texp, jnp.where(rid == 1, nact, 0.0)).astype(jnp.int32)

    @pl.when(ph == 1)
    def _():
        incl = _dot(ohs.astype(BF16), tri_ref[...])
        slot = base_scr[:, 0:1] + incl - ohs
        pos1 = jnp.sum(oh1 * slot, axis=0, keepdims=True)
        pos2 = jnp.sum(oh2 * slot, axis=0, keepdims=True)
        rid = lax.broadcasted_iota(jnp.int32, pos_ref.shape, 0)
        pos_ref[...] = jnp.where(rid == 0, pos1, jnp.where(rid == 1, pos2, 0.0)).astype(jnp.int32)
        base_scr[...] += jnp.sum(ohs, axis=1, keepdims=True)


def _route(ri, tri, *, n_tile_lanes):
    n = ri.shape[1]
    rb = min(ROUTE_BLOCK, n)
    return pl.pallas_call(
        _route_kernel,
        grid=(2, n // rb),
        in_specs=[
            pl.BlockSpec((SUBLANES, rb), lambda p, i: (0, i)),
            pl.BlockSpec((rb, rb), lambda p, i: (0, 0)),
        ],
        out_specs=[
            pl.BlockSpec((SUBLANES, rb), lambda p, i: (0, i * p)),
            pl.BlockSpec((SUBLANES, n_tile_lanes), lambda p, i: (0, 0)),
        ],
        out_shape=[
            jax.ShapeDtypeStruct((SUBLANES, n), jnp.int32),
            jax.ShapeDtypeStruct((SUBLANES, n_tile_lanes), jnp.int32),
        ],
        scratch_shapes=[pltpu.VMEM((N_EXPERTS, rb), F32), pltpu.VMEM((N_EXPERTS, LANES), F32)],
        compiler_params=pltpu.CompilerParams(dimension_semantics=("arbitrary", "arbitrary")),
        name="route",
    )(ri, tri)


def _invert_kernel(tb, pos_hbm, zero_hbm, tok_hbm, idx_a, idx_b, tok_smem, idx_sems, out_sem):
    i = pl.program_id(0)
    n_steps = pl.num_programs(0)
    idx_bufs = (idx_a, idx_b)

    def idx_copy(step, par):
        return pltpu.make_async_copy(pos_hbm.at[step], idx_bufs[par], idx_sems.at[par])

    @pl.when(i == 0)
    def _():
        idx_copy(0, 0).start()
        clear = pltpu.make_async_copy(zero_hbm, tok_smem, out_sem)
        clear.start()
        clear.wait()

    def step(par):
        @pl.when(i + 1 < n_steps)
        def _():
            idx_copy(i + 1, 1 - par).start()

        idx_copy(i, par).wait()
        base = i * tb

        def fill(r, carry):
            for k in range(2):
                tok_smem[idx_bufs[par][k * tb + r]] = base + r
            return carry
        lax.fori_loop(0, tb, fill, 0, unroll=8)

    @pl.when(i % 2 == 0)
    def _():
        step(0)

    @pl.when(i % 2 == 1)
    def _():
        step(1)

    @pl.when(i == n_steps - 1)
    def _():
        out_cp = pltpu.make_async_copy(tok_smem, tok_hbm, out_sem)
        out_cp.start()
        out_cp.wait()


def _invert(pos_blocks, n_slots):
    tb = pos_blocks.shape[1] // 2
    return pl.pallas_call(
        functools.partial(_invert_kernel, tb),
        grid=(pos_blocks.shape[0],),
        in_specs=[pl.BlockSpec(memory_space=pl.ANY), pl.BlockSpec(memory_space=pl.ANY)],
        out_specs=pl.BlockSpec(memory_space=pl.ANY),
        out_shape=jax.ShapeDtypeStruct((n_slots,), jnp.int32),
        scratch_shapes=[pltpu.SMEM((2 * tb,), jnp.int32), pltpu.SMEM((2 * tb,), jnp.int32),
                        pltpu.SMEM((n_slots,), jnp.int32),
                        pltpu.SemaphoreType.DMA((2,)), pltpu.SemaphoreType.DMA(())],
        compiler_params=pltpu.CompilerParams(dimension_semantics=("arbitrary",)),
        name="invert_slots",
    )(pos_blocks, jnp.zeros((n_slots,), jnp.int32))


def _experts_kernel(texp_ref, nact_ref, tok_hbm, h_hbm, nffn_ref, wg_ref, wu_ref, wd_ref, ys_ref,
                    tok_a, tok_b, tok_c, xbuf, wg_b, wu_b, wd_b, idx_sems, row_sems):
    i = pl.program_id(0)
    n_steps = pl.num_programs(0)
    n_act = nact_ref[0]
    tok_bufs = (tok_a, tok_b, tok_c)
    depth = len(tok_bufs)

    def idx_copy(step, par):
        return pltpu.make_async_copy(tok_hbm.at[pl.ds(step * MOE_TILE, MOE_TILE)], tok_bufs[par],
                                     idx_sems.at[par])

    def row_copy(par, r, token):
        return pltpu.make_async_copy(h_hbm.at[pl.ds(token, 1)], xbuf.at[par, pl.ds(r, 1)],
                                     row_sems.at[par])

    def issue_rows(par):
        def body(g, carry):
            for k in range(2):
                r = 2 * g + k
                row_copy(par, r, tok_bufs[par][r]).start(priority=k)
            return carry
        lax.fori_loop(0, MOE_TILE // 2, body, 0, unroll=4)

    def drain_rows(par):
        pltpu.make_async_copy(h_hbm.at[pl.ds(0, MOE_TILE)], xbuf.at[par], row_sems.at[par]).wait()

    e = texp_ref[i]
    e_prev = texp_ref[jnp.maximum(i - 1, 0)]

    @pl.when((i == 0) | (e != e_prev))
    def _():
        wg_b[...] = wg_ref[0].astype(BF16)
        wu_b[...] = wu_ref[0].astype(BF16)
        wd_b[...] = wd_ref[0].astype(BF16)

    def step(par):
        ahead = (par + 2) % depth

        @pl.when(i == 0)
        def _():
            idx_copy(0, 0).start()
            idx_copy(0, 0).wait()
            issue_rows(0)

            @pl.when(n_steps > 1)
            def _():
                idx_copy(1, 1).start()
                idx_copy(1, 1).wait()

            @pl.when(n_act > 1)
            def _():
                issue_rows(1)

            @pl.when(n_steps > 2)
            def _():
                idx_copy(2, 2).start()

        @pl.when(i + 2 < n_steps)
        def _():
            idx_copy(i + 2, ahead).wait()

        @pl.when(i + 3 < n_steps)
        def _():
            idx_copy(i + 3, par).start()

        def compute_stages():
            hrow = xbuf[par]
            x = (hrow * lax.rsqrt(jnp.mean(hrow * hrow, axis=-1, keepdims=True) + NORM_EPS)
                 * nffn_ref[...]).astype(BF16)
            yield
            half = D_FF_EXPERT // 2
            gate, up = [], []
            for cb in range(2):
                gate.append(_dot(x, wg_b[:, cb * half:(cb + 1) * half]))
                yield
            for cb in range(2):
                up.append(_dot(x, wu_b[:, cb * half:(cb + 1) * half]))
                yield
            hdn = jnp.concatenate(
                [(gate[cb] * _sigmoid(gate[cb]) * up[cb]).astype(BF16) for cb in range(2)], axis=1)
            yield
            quarter = D_MODEL // 4
            for cb in range(4):
                ys_ref[:, cb * quarter:(cb + 1) * quarter] = _dot(
                    hdn, wd_b[:, cb * quarter:(cb + 1) * quarter])
                yield

        def issue_stages(per_stage=32):
            for r0 in range(0, MOE_TILE, per_stage):
                for r in range(r0, r0 + per_stage):
                    row_copy(ahead, r, tok_bufs[ahead][r]).start(priority=r % 2)
                yield

        @pl.when(i + 2 < n_act)
        def _():
            drain_rows(par)
            _interleave(compute_stages(), issue_stages())

        @pl.when((i < n_act) & (i + 2 >= n_act))
        def _():
            drain_rows(par)
            _interleave(compute_stages())

        @pl.when(i >= n_act)
        def _():
            ys_ref[...] = jnp.zeros_like(ys_ref)

    for par in range(depth):
        pl.when(i % depth == par)(functools.partial(step, par))


def _experts(texp, nact, tok, h, nffn, wg, wu, wd):
    n_tiles = tok.shape[0] // MOE_TILE
    w_blk = lambda i, texp, nact: (texp[i], 0, 0)
    grid_spec = pltpu.PrefetchScalarGridSpec(
        num_scalar_prefetch=2,
        grid=(n_tiles,),
        in_specs=[
            pl.BlockSpec(memory_space=pl.ANY),
            pl.BlockSpec(memory_space=pl.ANY),
            pl.BlockSpec((1, D_MODEL), lambda i, texp, nact: (0, 0)),
            pl.BlockSpec((1, D_MODEL, D_FF_EXPERT), w_blk),
            pl.BlockSpec((1, D_MODEL, D_FF_EXPERT), w_blk),
            pl.BlockSpec((1, D_FF_EXPERT, D_MODEL), w_blk),
        ],
        out_specs=pl.BlockSpec((MOE_TILE, D_MODEL), lambda i, texp, nact: (i, 0)),
        scratch_shapes=[pltpu.SMEM((MOE_TILE,), jnp.int32), pltpu.SMEM((MOE_TILE,), jnp.int32),
                        pltpu.SMEM((MOE_TILE,), jnp.int32),
                        pltpu.VMEM((3, MOE_TILE, D_MODEL), F32),
                        pltpu.VMEM((D_MODEL, D_FF_EXPERT), BF16),
                        pltpu.VMEM((D_MODEL, D_FF_EXPERT), BF16),
                        pltpu.VMEM((D_FF_EXPERT, D_MODEL), BF16),
                        pltpu.SemaphoreType.DMA((3,)), pltpu.SemaphoreType.DMA((3,))],
    )
    return pl.pallas_call(
        _experts_kernel,
        grid_spec=grid_spec,
        out_shape=jax.ShapeDtypeStruct((tok.shape[0], D_MODEL), F32),
        compiler_params=pltpu.CompilerParams(
            dimension_semantics=("arbitrary",), vmem_limit_bytes=VMEM_LIMIT),
        name="experts",
    )(texp, nact, tok, h, nffn, wg, wu, wd)


def _combine_kernel(pos_hbm, ys_hbm, h_ref, rf_ref, nfin_ref, out_ref,
                    idx_a, idx_b, idx_c, ybuf, idx_sems, row_sems):
    i = pl.program_id(0)
    n_steps = pl.num_programs(0)
    tb = h_ref.shape[0]
    idx_bufs = (idx_a, idx_b, idx_c)
    depth = len(idx_bufs)

    def idx_copy(step, par):
        return pltpu.make_async_copy(pos_hbm.at[step], idx_bufs[par], idx_sems.at[par])

    def issue_rows(par):
        def body(g, carry):
            for sub in range(SUBLANES):
                r = g * SUBLANES + sub
                for k in range(2):
                    pltpu.make_async_copy(
                        ys_hbm.at[pl.ds(idx_bufs[par][k * tb + r], 1)],
                        ybuf.at[par, k, pl.ds(r, 1)],
                        row_sems.at[par]).start(priority=k)
            return carry
        lax.fori_loop(0, tb // SUBLANES, body, 0)

    def drain_rows(par):
        for k in range(2):
            pltpu.make_async_copy(ys_hbm.at[pl.ds(0, tb)], ybuf.at[par, k], row_sems.at[par]).wait()

    def step(par):
        ahead = (par + 2) % depth

        @pl.when(i == 0)
        def _():
            idx_copy(0, 0).start()
            idx_copy(0, 0).wait()
            issue_rows(0)

            @pl.when(n_steps > 1)
            def _():
                idx_copy(1, 1).start()
                idx_copy(1, 1).wait()
                issue_rows(1)

            @pl.when(n_steps > 2)
            def _():
                idx_copy(2, 2).start()

        @pl.when(i + 2 < n_steps)
        def _():
            idx_copy(i + 2, ahead).wait()

        @pl.when(i + 3 < n_steps)
        def _():
            idx_copy(i + 3, par).start()

        n_stage = 8
        rows = tb // n_stage

        def compute_stages():
            rf = rf_ref[...]
            rf_cols = jnp.transpose(jnp.concatenate(
                [rf, jnp.zeros((LANES - SUBLANES, tb), F32)], axis=0))
            for st in range(n_stage):
                rs = slice(st * rows, (st + 1) * rows)
                hh = (h_ref[rs, :] + rf_cols[rs, 0:1] * ybuf[par, 0, rs, :]
                      + rf_cols[rs, 1:2] * ybuf[par, 1, rs, :])
                out_ref[rs, :] = (hh * lax.rsqrt(jnp.mean(hh * hh, axis=-1, keepdims=True) + NORM_EPS)
                                  * nfin_ref[...])
                yield

        def issue_stages():
            for st in range(n_stage):
                for r in range(st * rows, (st + 1) * rows):
                    for k in range(2):
                        pltpu.make_async_copy(
                            ys_hbm.at[pl.ds(idx_bufs[ahead][k * tb + r], 1)],
                            ybuf.at[ahead, k, pl.ds(r, 1)],
                            row_sems.at[ahead]).start(priority=k)
                yield

        drain_rows(par)

        @pl.when(i + 2 < n_steps)
        def _():
            _interleave(issue_stages(), compute_stages())

        @pl.when(i + 2 >= n_steps)
        def _():
            _interleave(compute_stages())

    for par in range(depth):
        pl.when(i % depth == par)(functools.partial(step, par))


def _combine(pos_blocks, ys, h, rf, nfin):
    n = h.shape[0]
    tb = pos_blocks.shape[1] // 2
    return pl.pallas_call(
        _combine_kernel,
        grid=(n // tb,),
        in_specs=[
            pl.BlockSpec(memory_space=pl.ANY),
            pl.BlockSpec(memory_space=pl.ANY),
            pl.BlockSpec((tb, D_MODEL), lambda i: (i, 0)),
            pl.BlockSpec((SUBLANES, tb), lambda i: (0, i)),
            pl.BlockSpec((1, D_MODEL), lambda i: (0, 0)),
        ],
        out_specs=pl.BlockSpec((tb, D_MODEL), lambda i: (i, 0)),
        out_shape=jax.ShapeDtypeStruct((n, D_MODEL), F32),
        scratch_shapes=[pltpu.SMEM((2 * tb,), jnp.int32), pltpu.SMEM((2 * tb,), jnp.int32),
                        pltpu.SMEM((2 * tb,), jnp.int32),
                        pltpu.VMEM((3, 2, tb, D_MODEL), F32),
                        pltpu.SemaphoreType.DMA((3,)), pltpu.SemaphoreType.DMA((3,))],
        compiler_params=pltpu.CompilerParams(dimension_semantics=("arbitrary",)),
        name="combine",
    )(pos_blocks, ys, h, rf, nfin)


def _moe(h, rf, ri, nffn, wg, wu, wd, nfin):
    n = h.shape[0]
    max_tiles = _max_tiles(n)
    n_tile_lanes = -(-max_tiles // LANES) * LANES
    rb = min(ROUTE_BLOCK, n)
    tri = jnp.triu(jnp.ones((rb, rb), BF16))
    pos, tinfo = _route(ri, tri, n_tile_lanes=n_tile_lanes)
    texp = tinfo[0, :max_tiles]
    nact = tinfo[1, :1]
    tb = min(ROW_BLOCK, n)
    pos_blocks = pos[0:2].reshape(2, n // tb, tb).transpose(1, 0, 2).reshape(n // tb, 2 * tb)
    tok = _invert(pos_blocks, max_tiles * MOE_TILE)
    ys = _experts(texp, nact, tok, h, nffn, wg, wu, wd)
    return _combine(pos_blocks, ys, h, rf, nfin)


def _pad_lanes_row(v):
    return jnp.pad(v.astype(F32), (0, LANES - v.shape[0]))[None, :]


def _block_forward(x, meta_tokens, norm_mix_w, w_in, conv_w, pool_w, pool_scale, a_log, dt_bias,
                   dn_norm_w, w_out, norm_ffn_w, router_group_w, router_group_b, router_expert_w,
                   router_expert_b, expert_w_gate, expert_w_up, expert_w_down, norm_final_w,
                   *, mixer_rows):
    bsz, seq, _ = x.shape
    n = bsz * seq
    x2d = x.reshape(n, D_MODEL)

    ab0 = 5 * D_MODEL
    w_main = jnp.concatenate([w_in[:, :ab0], w_in[:, ab0 + 2 * DN_HEADS:]], axis=1).astype(BF16)
    wab = jnp.pad(w_in[:, ab0:ab0 + 2 * DN_HEADS], ((0, 0), (0, LANES - 2 * DN_HEADS))).astype(BF16)
    wabt = wab.T
    nmix = norm_mix_w[None, :]
    alog = _pad_lanes_row(a_log)
    dtb = _pad_lanes_row(dt_bias)
    alogt = jnp.broadcast_to(alog.T, (LANES, LANES))
    dtbt = jnp.broadcast_to(dtb.T, (LANES, LANES))
    wr = jnp.zeros((D_MODEL, LANES), F32)
    wr = wr.at[:, 0:N_EXPERT_GROUPS].set(router_group_w)
    wr = wr.at[:, SUBLANES:SUBLANES + N_EXPERTS].set(router_expert_w)
    br = jnp.zeros((LANES,), F32)
    br = br.at[0:N_EXPERT_GROUPS].set(router_group_b)
    br = br.at[SUBLANES:SUBLANES + N_EXPERTS].set(router_expert_b)
    params = (nmix, w_main, wab, wabt,
              conv_w, pool_w.astype(BF16), pool_scale[None, :], alog, dtb, alogt, dtbt,
              dn_norm_w[None, :], w_out.astype(BF16), norm_ffn_w[None, :],
              wr.T, jnp.broadcast_to(br[:, None], (LANES, LANES)))

    pad_rows = CHUNK - N_META
    xm = jnp.concatenate([jnp.zeros((pad_rows, D_MODEL), F32), meta_tokens], axis=0)
    zeros_s = jnp.zeros((DN_HEADS, DN_HEAD_DIM, DN_HEAD_DIM), F32)
    zeros_ph = jnp.zeros((POOL_HALO, D_MODEL), F32)
    zeros_ch = jnp.zeros((CONV_HALO, 3 * D_MODEL), F32)
    meta_out = _mixer(xm, zeros_s, zeros_ph, zeros_ch, params,
                      batch=1, t_rows=CHUNK, pad_rows=pad_rows)
    s_meta, ph_meta, ch_meta = meta_out[3], meta_out[4], meta_out[5]

    h, rf, ri, _, _, _ = _mixer(x2d, s_meta, ph_meta, ch_meta, params,
                                batch=bsz, t_rows=mixer_rows, pad_rows=0)
    wg = expert_w_gate.reshape(N_EXPERTS, D_MODEL, D_FF_EXPERT)
    wu = expert_w_up.reshape(N_EXPERTS, D_MODEL, D_FF_EXPERT)
    wd = expert_w_down.reshape(N_EXPERTS, D_FF_EXPERT, D_MODEL)
    out = _moe(h, rf, ri, norm_ffn_w[None, :], wg, wu, wd, norm_final_w[None, :])
    return out.reshape(bsz, seq, D_MODEL)


def kernel(x, meta_tokens, norm_mix_w, w_in, conv_w, pool_w, pool_scale, a_log, dt_bias, dn_norm_w, w_out, norm_ffn_w, router_group_w, router_group_b, router_expert_w, router_expert_b, expert_w_gate, expert_w_up, expert_w_down, norm_final_w):
    assert norm_mix_w.shape[0] == 1, "single-layer block"
    seq = x.shape[1]
    return _block_forward(
        x, meta_tokens, norm_mix_w[0], w_in[0], conv_w[0], pool_w[0], pool_scale[0], a_log[0],
        dt_bias[0], dn_norm_w[0], w_out[0], norm_ffn_w[0], router_group_w[0], router_group_b[0],
        router_expert_w[0], router_expert_b[0], expert_w_gate[0], expert_w_up[0], expert_w_down[0],
        norm_final_w,
        mixer_rows=min(256, seq))
```

```python
import functools
import math

import jax
import jax.numpy as jnp
from jax import lax
from jax.experimental import pallas as pl
from jax.experimental.pallas import tpu as pltpu

F32 = jnp.float32
BF16 = jnp.bfloat16

D_MODEL = 1024
N_META = 16
POOL_WINDOWS = (2, 4, 8, 16)
POOL_GROUP_DIM = 256
DN_HEADS = 8
DN_HEAD_DIM = 128
CONV_WIDTH = 4
CHUNK = 64
N_EXPERT_GROUPS = 4
EXPERTS_PER_GROUP = 8
N_EXPERTS = 32
D_FF_EXPERT = 512
NORM_EPS = 1e-6

LANES = 128
SUBLANES = 8
P_MAIN_COLS = 7 * D_MODEL
POOL_HALO = 16
CONV_HALO = 8
VMEM_LIMIT = 56 * 1024 * 1024
MOE_TILE = 256
ROUTE_BLOCK = 512
ROW_BLOCK = 256


def _dot(a, b):
    return jnp.dot(a, b, preferred_element_type=F32)


def _dot_nt(a, b):
    return lax.dot_general(a, b, (((1,), (1,)), ((), ())), preferred_element_type=F32)


def _dot_tn(a, b):
    return lax.dot_general(a, b, (((0,), (0,)), ((), ())), preferred_element_type=F32)


def _sigmoid(x):
    return 1.0 / (1.0 + jnp.exp(-x))


def _interleave(*stage_generators):
    live = {i: g for i, g in enumerate(stage_generators)}
    results = [None] * len(stage_generators)
    while live:
        for i in list(live):
            try:
                next(live[i])
            except StopIteration as done:
                results[i] = done.value
                del live[i]
    return results


def _intra_chunk(qs, ks, vs, bcols, gcols, grows):
    c = qs[0].shape[0]
    hs = range(len(qs))
    ii = lax.broadcasted_iota(jnp.int32, (c, c), 0)
    jj = lax.broadcasted_iota(jnp.int32, (c, c), 1)
    dec =[jnp.exp(jnp.where(ii >= jj, gcols[h] - grows[h], -jnp.inf)) for h in hs]
    kb = [ks[h].astype(BF16) for h in hs]
    qkb = [jnp.concatenate([qs[h].astype(BF16), kb[h]], axis=0) for h in hs]
    qkk = [_dot_nt(qkb[h], kb[h]) for h in hs]
    yield
    egc = [jnp.exp(gcols[h]) for h in hs]
    pw = [jnp.where(ii > jj, -(bcols[h] * qkk[h][c:] * dec[h]), 0.0) for h in hs]
    sol = [jnp.concatenate([vs[h] * bcols[h], ks[h] * (bcols[h] * egc[h])], axis=1) for h in hs]
    width = 2 * DN_HEAD_DIM
    levels = int(math.log2(c))
    for lvl in range(levels):
        pb = [pw[h].astype(BF16) for h in hs]
        if lvl < levels - 1:
            r = [_dot(pb[h], jnp.concatenate([sol[h].astype(BF16), pb[h]], axis=1)) for h in hs]
            sol = [sol[h] + r[h][:, :width] for h in hs]
            pw = [r[h][:, width:] for h in hs]
        else:
            sol = [sol[h] + _dot(pb[h], sol[h].astype(BF16)) for h in hs]
        yield
    qd = [qs[h] * egc[h] for h in hs]
    glast = [gcols[h][c - 1:c, :] for h in hs]
    kd = [(ks[h] * jnp.exp(glast[h] - gcols[h])).astype(BF16) for h in hs]
    u = [sol[h][:, :DN_HEAD_DIM] for h in hs]
    wq = [jnp.concatenate([sol[h][:, DN_HEAD_DIM:], qd[h]], axis=0).astype(BF16) for h in hs]
    qkd = [(qkk[h][:c] * dec[h]).astype(BF16) for h in hs]
    cd = [jnp.broadcast_to(jnp.exp(glast[h]), (1, DN_HEAD_DIM)) for h in hs]
    return u, wq, qkd, kd, cd


def _scan_chunk(u, wq, qkd, kd, cd, s):
    c = u[0].shape[0]
    hs = range(len(u))
    sb = [s[h].astype(BF16) for h in hs]
    ws = [_dot(wq[h], sb[h]) for h in hs]
    yield
    vb = [(u[h] - ws[h][:c]).astype(BF16) for h in hs]
    o = [ws[h][c:] + _dot(qkd[h], vb[h]) for h in hs]
    s_new = [s[h] * cd[h] + _dot_tn(kd[h], vb[h]) for h in hs]
    yield
    return o, s_new


def _mixer_kernel(pad_rows, t_rows,
                  x_ref, s0_ref, ph0_ref, ch0_ref, nmix_ref, win_ref, wab_ref, wabt_ref,
                  convw_ref, poolw_ref, pscale_ref, alog_ref, dtb_ref, alogt_ref, dtbt_ref,
                  dnw_ref, wout_ref, nffn_ref, wrt_ref, brt_ref,
                  h_ref, rf_ref, ri_ref, s_out_ref, ph_out_ref, ch_out_ref,
                  s_scr, pool_buf, conv_buf, un_scr, ypool_scr, qkv_scr, o_scr, beta_scr, gcol_scr, grow_scr,
                  u_scr, wq_scr, qkd_scr, kd_scr, cd_scr):
    t = t_rows
    j = pl.program_id(1)
    n_chunks = t // CHUNK

    @pl.when(j == 0)
    def _():
        s_scr[...] = s0_ref[...]
        pool_buf[0:POOL_HALO, :] = ph0_ref[...]
        conv_buf[0:CONV_HALO, :] = ch0_ref[...]

    x = x_ref[...]
    un = (x * lax.rsqrt(jnp.mean(x * x, axis=-1, keepdims=True) + NORM_EPS) * nmix_ref[...]).astype(BF16)

    un_scr[...] = un
    sub = POOL_GROUP_DIM
    n_sub = D_MODEL // sub

    def project(col0):
        return _dot(un, win_ref[:, col0:col0 + sub])

    def pool_group(gi):
        win = POOL_WINDOWS[gi]
        cs = slice(gi * sub, (gi + 1) * sub)
        acc = pool_buf[:, cs]
        shift = 1
        while shift < win:
            acc = acc + pltpu.roll(acc, shift, axis=0)
            shift *= 2
        pooled = acc[POOL_HALO:, :] * (1.0 / win) - pool_buf[POOL_HALO:POOL_HALO + t, cs]
        ypool_scr[:, cs] = _dot(pooled.astype(BF16), poolw_ref[gi]) * pscale_ref[:, cs]

    def conv_sub(col0):
        cs = slice(col0, col0 + sub)
        acc = convw_ref[CONV_WIDTH - 1:CONV_WIDTH, cs] * conv_buf[CONV_HALO:CONV_HALO + t, cs]
        for kk in range(CONV_WIDTH - 1):
            off = CONV_HALO - (CONV_WIDTH - 1) + kk
            acc = acc + convw_ref[kk:kk + 1, cs] * conv_buf[off:off + t, cs]
        act = acc * _sigmoid(acc)
        if col0 >= 2 * D_MODEL:
            qkv_scr[:, cs] = act
            return
        for hh in range(sub // DN_HEAD_DIM):
            part = act[:, hh * DN_HEAD_DIM:(hh + 1) * DN_HEAD_DIM]
            nrm = lax.rsqrt(jnp.sum(part * part, axis=-1, keepdims=True) + NORM_EPS)
            if col0 < D_MODEL:
                nrm = nrm * (DN_HEAD_DIM ** -0.5)
            qkv_scr[:, col0 + hh * DN_HEAD_DIM:col0 + (hh + 1) * DN_HEAD_DIM] = part * nrm

    for si in range(n_sub):
        pool_buf[POOL_HALO:POOL_HALO + t, si * sub:(si + 1) * sub] = project(si * sub)
    for si in range(n_sub):
        conv_buf[CONV_HALO:CONV_HALO + t, si * sub:(si + 1) * sub] = project(D_MODEL + si * sub)
        pool_group(si)
    for blk in range(1, 3):
        for si in range(n_sub):
            c0 = blk * D_MODEL + si * sub
            conv_buf[CONV_HALO:CONV_HALO + t, c0:c0 + sub] = project(D_MODEL + c0)
            conv_sub(c0 - D_MODEL)
    pab = _dot(un, wab_ref[...])
    pabt = _dot_nt(wabt_ref[...], un)

    gcol_all = -jnp.exp(alog_ref[...]) * _softplus(pab + dtb_ref[...])
    beta_all = _sigmoid(pab)
    ab_rows = 2 * DN_HEADS
    grow_all = (-jnp.exp(alogt_ref[0:ab_rows, 0:1])
                * _softplus(pabt[0:ab_rows, :] + dtbt_ref[0:ab_rows, 0:1]))
    if pad_rows:
        rid = lax.broadcasted_iota(jnp.int32, (t, LANES), 0)
        gcol_all = jnp.where(rid >= pad_rows, gcol_all, 0.0)
        beta_all = jnp.where(rid >= pad_rows, beta_all, 0.0)
        cid = lax.broadcasted_iota(jnp.int32, (ab_rows, t), 1)
        grow_all = jnp.where(cid >= pad_rows, grow_all, 0.0)
    beta_scr[...] = beta_all
    in_chunk_r = lax.broadcasted_iota(jnp.int32, (t, LANES), 0) % CHUNK
    in_chunk_c = lax.broadcasted_iota(jnp.int32, (ab_rows, t), 1) % CHUNK
    lane_scan = t % LANES == 0
    shift = 1
    while shift < CHUNK:
        gcol_all = gcol_all + jnp.where(in_chunk_r >= shift, pltpu.roll(gcol_all, shift, axis=0), 0.0)
        if lane_scan:
            grow_all = grow_all + jnp.where(in_chunk_c >= shift, pltpu.roll(grow_all, shift, axis=1), 0.0)
        shift *= 2
    gcol_scr[...] = gcol_all
    if not lane_scan:
        ci = lax.broadcasted_iota(jnp.int32, (t, t), 0)
        cj = lax.broadcasted_iota(jnp.int32, (t, t), 1)
        tri_u = jnp.where((ci <= cj) & (ci // CHUNK == cj // CHUNK), 1.0, 0.0).astype(F32)
        grow_all = jnp.dot(grow_all, tri_u, preferred_element_type=F32,
                           precision=lax.Precision.HIGHEST)
    for c in range(n_chunks):
        grow_scr[c] = grow_all[:, c * CHUNK:(c + 1) * CHUNK]
    for si in range(n_sub):
        conv_sub(2 * D_MODEL + si * sub)

    heads = range(DN_HEADS)

    def head_cols(base, hh):
        return slice(base + hh * DN_HEAD_DIM, base + (hh + 1) * DN_HEAD_DIM)

    def intra_stage(chunks):
        qs, ks, vs, bcols, gcols, grows = [], [], [], [], [], []
        for c in chunks:
            rs = slice(c * CHUNK, (c + 1) * CHUNK)
            gcol_c = gcol_scr[rs, :]
            beta_c = beta_scr[rs, :]
            grow_c = grow_scr[c]
            for hh in heads:
                qs.append(qkv_scr[rs, head_cols(0, hh)])
                ks.append(qkv_scr[rs, head_cols(D_MODEL, hh)])
                vs.append(qkv_scr[rs, head_cols(2 * D_MODEL, hh)])
                bcols.append(beta_c[:, DN_HEADS + hh:DN_HEADS + hh + 1])
                gcols.append(gcol_c[:, hh:hh + 1])
                grows.append(grow_c[hh:hh + 1, :])
        u, wq, qkd, kd, cd = yield from _intra_chunk(qs, ks, vs, bcols, gcols, grows)
        for ci, c in enumerate(chunks):
            for hh in heads:
                idx = ci * DN_HEADS + hh
                u_scr[c, hh] = u[idx]
                wq_scr[c, hh] = wq[idx]
                qkd_scr[c, hh] = qkd[idx]
                kd_scr[c, hh] = kd[idx]
            cd_scr[c] = jnp.concatenate(cd[ci * DN_HEADS:(ci + 1) * DN_HEADS], axis=0)

    def scan_stage(chunks):
        for c in chunks:
            cds = cd_scr[c]
            o, s_new = yield from _scan_chunk(
                [u_scr[c, hh] for hh in heads], [wq_scr[c, hh] for hh in heads],
                [qkd_scr[c, hh] for hh in heads], [kd_scr[c, hh] for hh in heads],
                [cds[hh:hh + 1, :] for hh in heads], [s_scr[hh] for hh in heads])
            for hh in heads:
                s_scr[hh] = s_new[hh]
                o_scr[c * CHUNK:(c + 1) * CHUNK, head_cols(0, hh)] = o[hh]

    def output_stage(r0, nr):
        rs = slice(r0, r0 + nr)
        unh = un_scr[rs, :]

        def gate_proj(col0):
            return [_dot(unh, win_ref[:, col0 + si * sub:col0 + (si + 1) * sub]) for si in range(n_sub)]

        z = gate_proj(4 * D_MODEL)
        yield
        y_dn = []
        for hh in heads:
            o = o_scr[rs, head_cols(0, hh)]
            o = o * lax.rsqrt(jnp.mean(o * o, axis=-1, keepdims=True) + NORM_EPS) * dnw_ref[...]
            per = sub // DN_HEAD_DIM
            zz = z[hh // per][:, (hh % per) * DN_HEAD_DIM:(hh % per + 1) * DN_HEAD_DIM]
            y_dn.append(o * (zz * _sigmoid(zz)))
        g_pool = gate_proj(5 * D_MODEL)
        yield
        g_dn = gate_proj(6 * D_MODEL)
        yield
        merged = []
        per = sub // DN_HEAD_DIM
        for si in range(n_sub):
            y_dn_s = jnp.concatenate(y_dn[si * per:(si + 1) * per], axis=1)
            merged.append((_sigmoid(g_pool[si]) * ypool_scr[rs, si * sub:(si + 1) * sub]
                           + _sigmoid(g_dn[si]) * y_dn_s).astype(BF16))
        h = x_ref[rs, :] + _dot(jnp.concatenate(merged, axis=1), wout_ref[...])
        h_ref[rs, :] = h
        yield
        xt = h * lax.rsqrt(jnp.mean(h * h, axis=-1, keepdims=True) + NORM_EPS) * nffn_ref[...]
        n_logit_rows = SUBLANES + N_EXPERTS
        logits = lax.dot_general(wrt_ref[0:n_logit_rows, :], xt, (((1,), (1,)), ((), ())),
                                 preferred_element_type=F32, precision=lax.Precision.HIGHEST)
        yield
        logits = logits + brt_ref[0:n_logit_rows, 0:1]
        rid8 = lax.broadcasted_iota(jnp.int32, (SUBLANES, nr), 0)
        lg = jnp.where(rid8 < N_EXPERT_GROUPS, logits[0:SUBLANES, :], -jnp.inf)
        gmax = jnp.max(lg, axis=0, keepdims=True)
        g_idx = jnp.min(jnp.where(lg == gmax, rid8, SUBLANES), axis=0, keepdims=True)
        p_grp = 1.0 / jnp.sum(jnp.exp(lg - gmax), axis=0, keepdims=True)
        sel = jnp.zeros((EXPERTS_PER_GROUP, nr), F32)
        for gi in range(N_EXPERT_GROUPS):
            e0 = SUBLANES + gi * EXPERTS_PER_GROUP
            sel = jnp.where(g_idx == gi, logits[e0:e0 + EXPERTS_PER_GROUP, :], sel)
        m1 = jnp.max(sel, axis=0, keepdims=True)
        i1 = jnp.min(jnp.where(sel == m1, rid8, SUBLANES), axis=0, keepdims=True)
        sel2 = jnp.where(rid8 == i1, -jnp.inf, sel)
        m2 = jnp.max(sel2, axis=0, keepdims=True)
        i2 = jnp.min(jnp.where(sel2 == m2, rid8, SUBLANES), axis=0, keepdims=True)
        e21 = jnp.exp(m2 - m1)
        w1 = 1.0 / (1.0 + e21)
        c1 = p_grp * w1
        c2 = p_grp * (e21 * w1)
        id1 = g_idx * EXPERTS_PER_GROUP + i1
        id2 = g_idx * EXPERTS_PER_GROUP + i2
        rf_ref[:, rs] = jnp.where(rid8 == 0, c1, jnp.where(rid8 == 1, c2, 0.0))
        ri_ref[:, rs] = jnp.where(rid8 == 0, id1, jnp.where(rid8 == 1, id2, 0))

    group = 2 if n_chunks % 2 == 0 else 1
    groups = [list(range(g0, g0 + group)) for g0 in range(0, n_chunks, group)]
    _interleave(intra_stage(groups[0]))
    for gi in range(1, len(groups)):
        _interleave(intra_stage(groups[gi]), scan_stage(groups[gi - 1]))
    rows_before_last = groups[-1][0] * CHUNK
    if rows_before_last >= LANES:
        _interleave(scan_stage(groups[-1]), output_stage(0, rows_before_last))
        _interleave(output_stage(rows_before_last, t - rows_before_last))
    else:
        _interleave(scan_stage(groups[-1]))
        _interleave(output_stage(0, t))

    pool_buf[0:POOL_HALO, :] = pool_buf[t:t + POOL_HALO, :]
    conv_buf[0:CONV_HALO, :] = conv_buf[t:t + CONV_HALO, :]
    s_out_ref[...] = s_scr[...]
    ph_out_ref[...] = pool_buf[0:POOL_HALO, :]
    ch_out_ref[...] = conv_buf[0:CONV_HALO, :]


def _softplus(x):
    return jnp.maximum(x, 0.0) + jnp.log1p(jnp.exp(-jnp.abs(x)))


def _mixer(x2d, s0, ph0, ch0, params, *, batch, t_rows, pad_rows):
    n = x2d.shape[0]
    n_t = n // batch // t_rows
    t = t_rows
    row_blk = lambda b, j: (b * n_t + j, 0)
    col_blk = lambda b, j: (0, b * n_t + j)
    const2 = lambda b, j: (0, 0)
    const3 = lambda b, j: (0, 0, 0)
    (nmix, w_main, wab, wabt,
     convw, poolw, pscale, alog, dtb, alogt, dtbt, dnw, wout, nffn, wrt, brt) = params
    in_specs = [
        pl.BlockSpec((t, D_MODEL), row_blk),
        pl.BlockSpec((DN_HEADS, DN_HEAD_DIM, DN_HEAD_DIM), const3),
        pl.BlockSpec((POOL_HALO, D_MODEL), const2),
        pl.BlockSpec((CONV_HALO, 3 * D_MODEL), const2),
        pl.BlockSpec(nmix.shape, const2),
        pl.BlockSpec(w_main.shape, const2, pipeline_mode=pl.Buffered(1)),
        pl.BlockSpec(wab.shape, const2),
        pl.BlockSpec(wabt.shape, const2),
        pl.BlockSpec(convw.shape, const2),
        pl.BlockSpec(poolw.shape, const3),
        pl.BlockSpec(pscale.shape, const2),
        pl.BlockSpec(alog.shape, const2),
        pl.BlockSpec(dtb.shape, const2),
        pl.BlockSpec(alogt.shape, const2),
        pl.BlockSpec(dtbt.shape, const2),
        pl.BlockSpec(dnw.shape, const2),
        pl.BlockSpec(wout.shape, const2),
        pl.BlockSpec(nffn.shape, const2),
        pl.BlockSpec(wrt.shape, const2),
        pl.BlockSpec(brt.shape, const2),
    ]
    out_specs = [
        pl.BlockSpec((t, D_MODEL), row_blk),
        pl.BlockSpec((SUBLANES, t), col_blk),
        pl.BlockSpec((SUBLANES, t), col_blk),
        pl.BlockSpec((DN_HEADS, DN_HEAD_DIM, DN_HEAD_DIM), const3),
        pl.BlockSpec((POOL_HALO, D_MODEL), const2),
        pl.BlockSpec((CONV_HALO, 3 * D_MODEL), const2),
    ]
    out_shape = [
        jax.ShapeDtypeStruct((n, D_MODEL), F32),
        jax.ShapeDtypeStruct((SUBLANES, n), F32),
        jax.ShapeDtypeStruct((SUBLANES, n), jnp.int32),
        jax.ShapeDtypeStruct((DN_HEADS, DN_HEAD_DIM, DN_HEAD_DIM), F32),
        jax.ShapeDtypeStruct((POOL_HALO, D_MODEL), F32),
        jax.ShapeDtypeStruct((CONV_HALO, 3 * D_MODEL), F32),
    ]
    scratch = [
        pltpu.VMEM((DN_HEADS, DN_HEAD_DIM, DN_HEAD_DIM), F32),
        pltpu.VMEM((t + POOL_HALO, D_MODEL), F32),
        pltpu.VMEM((t + CONV_HALO, 3 * D_MODEL), F32),
        pltpu.VMEM((t, D_MODEL), BF16),
        pltpu.VMEM((t, D_MODEL), F32),
        pltpu.VMEM((t, 3 * D_MODEL), F32),
        pltpu.VMEM((t, D_MODEL), F32),
        pltpu.VMEM((t, LANES), F32),
        pltpu.VMEM((t, LANES), F32),
        pltpu.VMEM((t // CHUNK, 2 * DN_HEADS, CHUNK), F32),
        pltpu.VMEM((t // CHUNK, DN_HEADS, CHUNK, DN_HEAD_DIM), F32),
        pltpu.VMEM((t // CHUNK, DN_HEADS, 2 * CHUNK, DN_HEAD_DIM), BF16),
        pltpu.VMEM((t // CHUNK, DN_HEADS, CHUNK, CHUNK), BF16),
        pltpu.VMEM((t // CHUNK, DN_HEADS, CHUNK, DN_HEAD_DIM), BF16),
        pltpu.VMEM((t // CHUNK, DN_HEADS, DN_HEAD_DIM), F32),
    ]
    return pl.pallas_call(
        functools.partial(_mixer_kernel, pad_rows, t_rows),
        grid=(batch, n_t),
        in_specs=in_specs,
        out_specs=out_specs,
        out_shape=out_shape,
        scratch_shapes=scratch,
        compiler_params=pltpu.CompilerParams(
            dimension_semantics=("arbitrary", "arbitrary"), vmem_limit_bytes=VMEM_LIMIT),
        name="mixer_meta" if pad_rows else "mixer",
    )(x2d, s0, ph0, ch0, *params)


def _max_tiles(n_tokens):
    return 2 * n_tokens // MOE_TILE + N_EXPERTS


def _route_kernel(ri_ref, tri_ref, pos_ref, tinfo_ref, cnt_scr, base_scr):
    ph = pl.program_id(0)
    i = pl.program_id(1)
    rb = ri_ref.shape[1]
    eid = lax.broadcasted_iota(jnp.int32, (N_EXPERTS, rb), 0)
    oh1 = jnp.where(ri_ref[0:1, :] == eid, 1.0, 0.0).astype(F32)
    oh2 = jnp.where(ri_ref[1:2, :] == eid, 1.0, 0.0).astype(F32)
    ohs = oh1 + oh2

    @pl.when((ph == 0) & (i == 0))
    def _():
        cnt_scr[...] = jnp.zeros_like(cnt_scr)

    @pl.when(ph == 0)
    def _():
        cnt_scr[...] += ohs

    @pl.when((ph == 1) & (i == 0))
    def _():
        counts = jnp.sum(cnt_scr[...], axis=1, keepdims=True)
        padded = jnp.floor((counts + (MOE_TILE - 1)) * (1.0 / MOE_TILE)) * MOE_TILE
        padded_b = jnp.broadcast_to(padded, (N_EXPERTS, LANES))
        r = lax.broadcasted_iota(jnp.int32, (N_EXPERTS, N_EXPERTS), 0)
        c = lax.broadcasted_iota(jnp.int32, (N_EXPERTS, N_EXPERTS), 1)
        strict = jnp.where(r > c, 1.0, 0.0).astype(F32)
        offs = jnp.dot(strict, padded_b, preferred_element_type=F32,
                       precision=lax.Precision.HIGHEST)
        base_scr[...] = offs
        ends = offs[:, 0:1] + padded
        total = jnp.sum(padded, axis=0, keepdims=True)
        n_lanes = tinfo_ref.shape[1]
        tile_start = (lax.broadcasted_iota(jnp.int32, (N_EXPERTS, n_lanes), 1) * MOE_TILE).astype(F32)
        texp = jnp.sum(jnp.where(ends <= tile_start, 1.0, 0.0), axis=0, keepdims=True)
        last_active = jnp.sum(jnp.where(ends <= total - MOE_TILE, 1.0, 0.0), axis=0, keepdims=True)
        texp = jnp.minimum(texp, last_active)
        nact = jnp.broadcast_to(total * (1.0 / MOE_TILE), (1, n_lanes))
        rid = lax.broadcasted_iota(jnp.int32, tinfo_ref.shape, 0)
        tinfo_ref[...] = jnp.where(rid == 0, texp, jnp.where(rid == 1, nact, 0.0)).astype(jnp.int32)

    @pl.when(ph == 1)
    def _():
        incl = _dot(ohs.astype(BF16), tri_ref[...])
        slot = base_scr[:, 0:1] + incl - ohs
        pos1 = jnp.sum(oh1 * slot, axis=0, keepdims=True)
        pos2 = jnp.sum(oh2 * slot, axis=0, keepdims=True)
        rid = lax.broadcasted_iota(jnp.int32, pos_ref.shape, 0)
        pos_ref[...] = jnp.where(rid == 0, pos1, jnp.where(rid == 1, pos2, 0.0)).astype(jnp.int32)
        base_scr[...] += jnp.sum(ohs, axis=1, keepdims=True)


def _route(ri, tri, *, n_tile_lanes):
    n = ri.shape[1]
    rb = min(ROUTE_BLOCK, n)
    return pl.pallas_call(
        _route_kernel,
        grid=(2, n // rb),
        in_specs=[
            pl.BlockSpec((SUBLANES, rb), lambda p, i: (0, i)),
            pl.BlockSpec((rb, rb), lambda p, i: (0, 0)),
        ],
        out_specs=[
            pl.BlockSpec((SUBLANES, rb), lambda p, i: (0, i * p)),
            pl.BlockSpec((SUBLANES, n_tile_lanes), lambda p, i: (0, 0)),
        ],
        out_shape=[
            jax.ShapeDtypeStruct((SUBLANES, n), jnp.int32),
            jax.ShapeDtypeStruct((SUBLANES, n_tile_lanes), jnp.int32),
        ],
        scratch_shapes=[pltpu.VMEM((N_EXPERTS, rb), F32), pltpu.VMEM((N_EXPERTS, LANES), F32)],
        compiler_params=pltpu.CompilerParams(dimension_semantics=("arbitrary", "arbitrary")),
        name="route",
    )(ri, tri)


def _invert_kernel(tb, pos_hbm, zero_hbm, tok_hbm, idx_a, idx_b, tok_smem, idx_sems, out_sem):
    i = pl.program_id(0)
    n_steps = pl.num_programs(0)
    idx_bufs = (idx_a, idx_b)

    def idx_copy(step, par):
        return pltpu.make_async_copy(pos_hbm.at[step], idx_bufs[par], idx_sems.at[par])

    @pl.when(i == 0)
    def _():
        idx_copy(0, 0).start()
        clear = pltpu.make_async_copy(zero_hbm, tok_smem, out_sem)
        clear.start()
        clear.wait()

    def step(par):
        @pl.when(i + 1 < n_steps)
        def _():
            idx_copy(i + 1, 1 - par).start()

        idx_copy(i, par).wait()
        base = i * tb

        def fill(r, carry):
            for k in range(2):
                tok_smem[idx_bufs[par][k * tb + r]] = base + r
            return carry
        lax.fori_loop(0, tb, fill, 0, unroll=8)

    @pl.when(i % 2 == 0)
    def _():
        step(0)

    @pl.when(i % 2 == 1)
    def _():
        step(1)

    @pl.when(i == n_steps - 1)
    def _():
        out_cp = pltpu.make_async_copy(tok_smem, tok_hbm, out_sem)
        out_cp.start()
        out_cp.wait()


def _invert(pos_blocks, n_slots):
    tb = pos_blocks.shape[1] // 2
    return pl.pallas_call(
        functools.partial(_invert_kernel, tb),
        grid=(pos_blocks.shape[0],),
        in_specs=[pl.BlockSpec(memory_space=pl.ANY), pl.BlockSpec(memory_space=pl.ANY)],
        out_specs=pl.BlockSpec(memory_space=pl.ANY),
        out_shape=jax.ShapeDtypeStruct((n_slots,), jnp.int32),
        scratch_shapes=[pltpu.SMEM((2 * tb,), jnp.int32), pltpu.SMEM((2 * tb,), jnp.int32),
                        pltpu.SMEM((n_slots,), jnp.int32),
                        pltpu.SemaphoreType.DMA((2,)), pltpu.SemaphoreType.DMA(())],
        compiler_params=pltpu.CompilerParams(dimension_semantics=("arbitrary",)),
        name="invert_slots",
    )(pos_blocks, jnp.zeros((n_slots,), jnp.int32))


def _experts_kernel(texp_ref, nact_ref, tok_hbm, h_hbm, nffn_ref, wg_ref, wu_ref, wd_ref, ys_ref,
                    tok_a, tok_b, tok_c, xbuf, wg_b, wu_b, wd_b, idx_sems, row_sems):
    i = pl.program_id(0)
    n_steps = pl.num_programs(0)
    n_act = nact_ref[0]
    tok_bufs = (tok_a, tok_b, tok_c)
    depth = len(tok_bufs)

    def idx_copy(step, par):
        return pltpu.make_async_copy(tok_hbm.at[pl.ds(step * MOE_TILE, MOE_TILE)], tok_bufs[par],
                                     idx_sems.at[par])

    def row_copy(par, r, token):
        return pltpu.make_async_copy(h_hbm.at[pl.ds(token, 1)], xbuf.at[par, pl.ds(r, 1)],
                                     row_sems.at[par])

    def issue_rows(par):
        def body(g, carry):
            for k in range(2):
                r = 2 * g + k
                row_copy(par, r, tok_bufs[par][r]).start(priority=k)
            return carry
        lax.fori_loop(0, MOE_TILE // 2, body, 0, unroll=4)

    def drain_rows(par):
        pltpu.make_async_copy(h_hbm.at[pl.ds(0, MOE_TILE)], xbuf.at[par], row_sems.at[par]).wait()

    e = texp_ref[i]
    e_prev = texp_ref[jnp.maximum(i - 1, 0)]

    @pl.when((i == 0) | (e != e_prev))
    def _():
        wg_b[...] = wg_ref[0].astype(BF16)
        wu_b[...] = wu_ref[0].astype(BF16)
        wd_b[...] = wd_ref[0].astype(BF16)

    def step(par):
        ahead = (par + 2) % depth

        @pl.when(i == 0)
        def _():
            idx_copy(0, 0).start()
            idx_copy(0, 0).wait()
            issue_rows(0)

            @pl.when(n_steps > 1)
            def _():
                idx_copy(1, 1).start()
                idx_copy(1, 1).wait()

            @pl.when(n_act > 1)
            def _():
                issue_rows(1)

            @pl.when(n_steps > 2)
            def _():
                idx_copy(2, 2).start()

        @pl.when(i + 2 < n_steps)
        def _():
            idx_copy(i + 2, ahead).wait()

        @pl.when(i + 3 < n_steps)
        def _():
            idx_copy(i + 3, par).start()

        def compute_stages():
            n_q = 4
            rq = MOE_TILE // n_q
            xq = []
            for qi in range(n_q):
                hrow = xbuf[par, qi * rq:(qi + 1) * rq, :]
                xq.append((hrow * lax.rsqrt(jnp.mean(hrow * hrow, axis=-1, keepdims=True) + NORM_EPS)
                           * nffn_ref[...]).astype(BF16))
                yield
            x = jnp.concatenate(xq, axis=0)
            half = D_FF_EXPERT // 2
            gate, up = [], []
            for cb in range(2):
                gate.append(_dot(x, wg_b[:, cb * half:(cb + 1) * half]))
                yield
            for cb in range(2):
                up.append(_dot(x, wu_b[:, cb * half:(cb + 1) * half]))
                yield
            hq = []
            for qi in range(n_q):
                rs = slice(qi * rq, (qi + 1) * rq)
                hq.append(jnp.concatenate(
                    [(gate[cb][rs] * _sigmoid(gate[cb][rs]) * up[cb][rs]).astype(BF16)
                     for cb in range(2)], axis=1))
                yield
            hdn = jnp.concatenate(hq, axis=0)
            quarter = D_MODEL // 4
            for cb in range(4):
                ys_ref[:, cb * quarter:(cb + 1) * quarter] = _dot(
                    hdn, wd_b[:, cb * quarter:(cb + 1) * quarter])
                yield

        def issue_stages(per_stage=16):
            for r0 in range(0, MOE_TILE, per_stage):
                for r in range(r0, r0 + per_stage):
                    row_copy(ahead, r, tok_bufs[ahead][r]).start(priority=r % 2)
                yield

        @pl.when(i + 2 < n_act)
        def _():
            drain_rows(par)
            _interleave(compute_stages(), issue_stages())

        @pl.when((i < n_act) & (i + 2 >= n_act))
        def _():
            drain_rows(par)
            _interleave(compute_stages())

        @pl.when(i >= n_act)
        def _():
            ys_ref[...] = jnp.zeros_like(ys_ref)

    for par in range(depth):
        pl.when(i % depth == par)(functools.partial(step, par))


def _experts(texp, nact, tok, h, nffn, wg, wu, wd):
    n_tiles = tok.shape[0] // MOE_TILE
    w_blk = lambda i, texp, nact: (texp[i], 0, 0)
    grid_spec = pltpu.PrefetchScalarGridSpec(
        num_scalar_prefetch=2,
        grid=(n_tiles,),
        in_specs=[
            pl.BlockSpec(memory_space=pl.ANY),
            pl.BlockSpec(memory_space=pl.ANY),
            pl.BlockSpec((1, D_MODEL), lambda i, texp, nact: (0, 0)),
            pl.BlockSpec((1, D_MODEL, D_FF_EXPERT), w_blk),
            pl.BlockSpec((1, D_MODEL, D_FF_EXPERT), w_blk),
            pl.BlockSpec((1, D_FF_EXPERT, D_MODEL), w_blk),
        ],
        out_specs=pl.BlockSpec((MOE_TILE, D_MODEL), lambda i, texp, nact: (i, 0)),
        scratch_shapes=[pltpu.SMEM((MOE_TILE,), jnp.int32), pltpu.SMEM((MOE_TILE,), jnp.int32),
                        pltpu.SMEM((MOE_TILE,), jnp.int32),
                        pltpu.VMEM((3, MOE_TILE, D_MODEL), F32),
                        pltpu.VMEM((D_MODEL, D_FF_EXPERT), BF16),
                        pltpu.VMEM((D_MODEL, D_FF_EXPERT), BF16),
                        pltpu.VMEM((D_FF_EXPERT, D_MODEL), BF16),
                        pltpu.SemaphoreType.DMA((3,)), pltpu.SemaphoreType.DMA((3,))],
    )
    return pl.pallas_call(
        _experts_kernel,
        grid_spec=grid_spec,
        out_shape=jax.ShapeDtypeStruct((tok.shape[0], D_MODEL), F32),
        compiler_params=pltpu.CompilerParams(
            dimension_semantics=("arbitrary",), vmem_limit_bytes=VMEM_LIMIT),
        name="experts",
    )(texp, nact, tok, h, nffn, wg, wu, wd)


def _combine_kernel(pos_hbm, ys_hbm, h_ref, rf_ref, nfin_ref, out_ref,
                    idx_a, idx_b, idx_c, ybuf, idx_sems, row_sems):
    i = pl.program_id(0)
    n_steps = pl.num_programs(0)
    tb = h_ref.shape[0]
    idx_bufs = (idx_a, idx_b, idx_c)
    depth = len(idx_bufs)

    def idx_copy(step, par):
        return pltpu.make_async_copy(pos_hbm.at[step], idx_bufs[par], idx_sems.at[par])

    def issue_rows(par):
        def body(g, carry):
            for sub in range(SUBLANES):
                r = g * SUBLANES + sub
                for k in range(2):
                    pltpu.make_async_copy(
                        ys_hbm.at[pl.ds(idx_bufs[par][k * tb + r], 1)],
                        ybuf.at[par, k, pl.ds(r, 1)],
                        row_sems.at[par]).start(priority=k)
            return carry
        lax.fori_loop(0, tb // SUBLANES, body, 0)

    def drain_rows(par):
        for k in range(2):
            pltpu.make_async_copy(ys_hbm.at[pl.ds(0, tb)], ybuf.at[par, k], row_sems.at[par]).wait()

    def step(par):
        ahead = (par + 2) % depth

        @pl.when(i == 0)
        def _():
            idx_copy(0, 0).start()
            idx_copy(0, 0).wait()
            issue_rows(0)

            @pl.when(n_steps > 1)
            def _():
                idx_copy(1, 1).start()
                idx_copy(1, 1).wait()
                issue_rows(1)

            @pl.when(n_steps > 2)
            def _():
                idx_copy(2, 2).start()

        @pl.when(i + 2 < n_steps)
        def _():
            idx_copy(i + 2, ahead).wait()

        @pl.when(i + 3 < n_steps)
        def _():
            idx_copy(i + 3, par).start()

        n_stage = 8
        rows = tb // n_stage

        def compute_stages():
            rf = rf_ref[...]
            rf_cols = jnp.transpose(jnp.concatenate(
                [rf, jnp.zeros((LANES - SUBLANES, tb), F32)], axis=0))
            for st in range(n_stage):
                rs = slice(st * rows, (st + 1) * rows)
                hh = (h_ref[rs, :] + rf_cols[rs, 0:1] * ybuf[par, 0, rs, :]
                      + rf_cols[rs, 1:2] * ybuf[par, 1, rs, :])
                out_ref[rs, :] = (hh * lax.rsqrt(jnp.mean(hh * hh, axis=-1, keepdims=True) + NORM_EPS)
                                  * nfin_ref[...])
                yield

        def issue_stages():
            for st in range(n_stage):
                for r in range(st * rows, (st + 1) * rows):
                    for k in range(2):
                        pltpu.make_async_copy(
                            ys_hbm.at[pl.ds(idx_bufs[ahead][k * tb + r], 1)],
                            ybuf.at[ahead, k, pl.ds(r, 1)],
                            row_sems.at[ahead]).start(priority=k)
                yield

        drain_rows(par)

        @pl.when(i + 2 < n_steps)
        def _():
            _interleave(issue_stages(), compute_stages())

        @pl.when(i + 2 >= n_steps)
        def _():
            _interleave(compute_stages())

    for par in range(depth):
        pl.when(i % depth == par)(functools.partial(step, par))


def _combine(pos_blocks, ys, h, rf, nfin):
    n = h.shape[0]
    tb = pos_blocks.shape[1] // 2
    return pl.pallas_call(
        _combine_kernel,
        grid=(n // tb,),
        in_specs=[
            pl.BlockSpec(memory_space=pl.ANY),
            pl.BlockSpec(memory_space=pl.ANY),
            pl.BlockSpec((tb, D_MODEL), lambda i: (i, 0)),
            pl.BlockSpec((SUBLANES, tb), lambda i: (0, i)),
            pl.BlockSpec((1, D_MODEL), lambda i: (0, 0)),
        ],
        out_specs=pl.BlockSpec((tb, D_MODEL), lambda i: (i, 0)),
        out_shape=jax.ShapeDtypeStruct((n, D_MODEL), F32),
        scratch_shapes=[pltpu.SMEM((2 * tb,), jnp.int32), pltpu.SMEM((2 * tb,), jnp.int32),
                        pltpu.SMEM((2 * tb,), jnp.int32),
                        pltpu.VMEM((3, 2, tb, D_MODEL), F32),
                        pltpu.SemaphoreType.DMA((3,)), pltpu.SemaphoreType.DMA((3,))],
        compiler_params=pltpu.CompilerParams(dimension_semantics=("arbitrary",)),
        name="combine",
    )(pos_blocks, ys, h, rf, nfin)


def _moe(h, rf, ri, nffn, wg, wu, wd, nfin):
    n = h.shape[0]
    max_tiles = _max_tiles(n)
    n_tile_lanes = -(-max_tiles // LANES) * LANES
    rb = min(ROUTE_BLOCK, n)
    tri = jnp.triu(jnp.ones((rb, rb), BF16))
    pos, tinfo = _route(ri, tri, n_tile_lanes=n_tile_lanes)
    texp = tinfo[0, :max_tiles]
    nact = tinfo[1, :1]
    tb = min(ROW_BLOCK, n)
    pos_blocks = pos[0:2].reshape(2, n // tb, tb).transpose(1, 0, 2).reshape(n // tb, 2 * tb)
    tok = _invert(pos_blocks, max_tiles * MOE_TILE)
    ys = _experts(texp, nact, tok, h, nffn, wg, wu, wd)
    return _combine(pos_blocks, ys, h, rf, nfin)


def _pad_lanes_row(v):
    return jnp.pad(v.astype(F32), (0, LANES - v.shape[0]))[None, :]


def _block_forward(x, meta_tokens, norm_mix_w, w_in, conv_w, pool_w, pool_scale, a_log, dt_bias,
                   dn_norm_w, w_out, norm_ffn_w, router_group_w, router_group_b, router_expert_w,
                   router_expert_b, expert_w_gate, expert_w_up, expert_w_down, norm_final_w,
                   *, mixer_rows):
    bsz, seq, _ = x.shape
    n = bsz * seq
    x2d = x.reshape(n, D_MODEL)

    ab0 = 5 * D_MODEL
    w_main = jnp.concatenate([w_in[:, :ab0], w_in[:, ab0 + 2 * DN_HEADS:]], axis=1).astype(BF16)
    wab = jnp.pad(w_in[:, ab0:ab0 + 2 * DN_HEADS], ((0, 0), (0, LANES - 2 * DN_HEADS))).astype(BF16)
    wabt = wab.T
    nmix = norm_mix_w[None, :]
    alog = _pad_lanes_row(a_log)
    dtb = _pad_lanes_row(dt_bias)
    alogt = jnp.broadcast_to(alog.T, (LANES, LANES))
    dtbt = jnp.broadcast_to(dtb.T, (LANES, LANES))
    wr = jnp.zeros((D_MODEL, LANES), F32)
    wr = wr.at[:, 0:N_EXPERT_GROUPS].set(router_group_w)
    wr = wr.at[:, SUBLANES:SUBLANES + N_EXPERTS].set(router_expert_w)
    br = jnp.zeros((LANES,), F32)
    br = br.at[0:N_EXPERT_GROUPS].set(router_group_b)
    br = br.at[SUBLANES:SUBLANES + N_EXPERTS].set(router_expert_b)
    params = (nmix, w_main, wab, wabt,
              conv_w, pool_w.astype(BF16), pool_scale[None, :], alog, dtb, alogt, dtbt,
              dn_norm_w[None, :], w_out.astype(BF16), norm_ffn_w[None, :],
              wr.T, jnp.broadcast_to(br[:, None], (LANES, LANES)))

    pad_rows = CHUNK - N_META
    xm = jnp.concatenate([jnp.zeros((pad_rows, D_MODEL), F32), meta_tokens], axis=0)
    zeros_s = jnp.zeros((DN_HEADS, DN_HEAD_DIM, DN_HEAD_DIM), F32)
    zeros_ph = jnp.zeros((POOL_HALO, D_MODEL), F32)
    zeros_ch = jnp.zeros((CONV_HALO, 3 * D_MODEL), F32)
    meta_out = _mixer(xm, zeros_s, zeros_ph, zeros_ch, params,
                      batch=1, t_rows=CHUNK, pad_rows=pad_rows)
    s_meta, ph_meta, ch_meta = meta_out[3], meta_out[4], meta_out[5]

    h, rf, ri, _, _, _ = _mixer(x2d, s_meta, ph_meta, ch_meta, params,
                                batch=bsz, t_rows=mixer_rows, pad_rows=0)
    wg = expert_w_gate.reshape(N_EXPERTS, D_MODEL, D_FF_EXPERT)
    wu = expert_w_up.reshape(N_EXPERTS, D_MODEL, D_FF_EXPERT)
    wd = expert_w_down.reshape(N_EXPERTS, D_FF_EXPERT, D_MODEL)
    out = _moe(h, rf, ri, norm_ffn_w[None, :], wg, wu, wd, norm_final_w[None, :])
    return out.reshape(bsz, seq, D_MODEL)


def kernel(x, meta_tokens, norm_mix_w, w_in, conv_w, pool_w, pool_scale, a_log, dt_bias, dn_norm_w, w_out, norm_ffn_w, router_group_w, router_group_b, router_expert_w, router_expert_b, expert_w_gate, expert_w_up, expert_w_down, norm_final_w):
    assert norm_mix_w.shape[0] == 1, "single-layer block"
    seq = x.shape[1]
    return _block_forward(
        x, meta_tokens, norm_mix_w[0], w_in[0], conv_w[0], pool_w[0], pool_scale[0], a_log[0],
        dt_bias[0], dn_norm_w[0], w_out[0], norm_ffn_w[0], router_group_w[0], router_group_b[0],
        router_expert_w[0], router_expert_b[0], expert_w_gate[0], expert_w_up[0], expert_w_down[0],
        norm_final_w,
        mixer_rows=min(256, seq))
```

```python
import functools
import math

import jax
import jax.numpy as jnp
from jax import lax
from jax.experimental import pallas as pl
from jax.experimental.pallas import tpu as pltpu

F32 = jnp.float32
BF16 = jnp.bfloat16

D_MODEL = 1024
N_META = 16
POOL_WINDOWS = (2, 4, 8, 16)
POOL_GROUP_DIM = 256
DN_HEADS = 8
DN_HEAD_DIM = 128
CONV_WIDTH = 4
CHUNK = 64
N_EXPERT_GROUPS = 4
EXPERTS_PER_GROUP = 8
N_EXPERTS = 32
D_FF_EXPERT = 512
NORM_EPS = 1e-6

LANES = 128
SUBLANES = 8
POOL_HALO = 16
CONV_HALO = 8
VMEM_LIMIT = 56 * 1024 * 1024
MOE_TILE = 256
ROUTE_BLOCK = 2048
ROW_BLOCK = 256


def _dot(a, b):
    return jnp.dot(a, b, preferred_element_type=F32)


def _dot_nt(a, b):
    return lax.dot_general(a, b, (((1,), (1,)), ((), ())), preferred_element_type=F32)


def _dot_tn(a, b):
    return lax.dot_general(a, b, (((0,), (0,)), ((), ())), preferred_element_type=F32)


def _sigmoid(x):
    return 1.0 / (1.0 + jnp.exp(-x))


def _interleave(*stage_generators):
    live = {i: g for i, g in enumerate(stage_generators)}
    results = [None] * len(stage_generators)
    while live:
        for i in list(live):
            try:
                next(live[i])
            except StopIteration as done:
                results[i] = done.value
                del live[i]
    return results


def _intra_chunk(qs, ks, vs, bcols, gcols, grows):
    c = qs[0].shape[0]
    hs = range(len(qs))
    ii = lax.broadcasted_iota(jnp.int32, (c, c), 0)
    jj = lax.broadcasted_iota(jnp.int32, (c, c), 1)
    dec =[jnp.exp(jnp.where(ii >= jj, gcols[h] - grows[h], -jnp.inf)) for h in hs]
    kb = [ks[h].astype(BF16) for h in hs]
    qkb = [jnp.concatenate([qs[h].astype(BF16), kb[h]], axis=0) for h in hs]
    qkk = [_dot_nt(qkb[h], kb[h]) for h in hs]
    yield
    egc = [jnp.exp(gcols[h]) for h in hs]
    pw = [jnp.where(ii > jj, -(bcols[h] * qkk[h][c:] * dec[h]), 0.0) for h in hs]
    sol = [jnp.concatenate([vs[h] * bcols[h], ks[h] * (bcols[h] * egc[h])], axis=1) for h in hs]
    width = 2 * DN_HEAD_DIM
    levels = int(math.log2(c))
    for lvl in range(levels):
        pb = [pw[h].astype(BF16) for h in hs]
        if lvl < levels - 1:
            r = [_dot(pb[h], jnp.concatenate([sol[h].astype(BF16), pb[h]], axis=1)) for h in hs]
            sol = [sol[h] + r[h][:, :width] for h in hs]
            pw = [r[h][:, width:] for h in hs]
        else:
            sol = [sol[h] + _dot(pb[h], sol[h].astype(BF16)) for h in hs]
        yield
    qd = [qs[h] * egc[h] for h in hs]
    glast = [gcols[h][c - 1:c, :] for h in hs]
    kd = [(ks[h] * jnp.exp(glast[h] - gcols[h])).astype(BF16) for h in hs]
    u = [sol[h][:, :DN_HEAD_DIM] for h in hs]
    wq = [jnp.concatenate([sol[h][:, DN_HEAD_DIM:], qd[h]], axis=0).astype(BF16) for h in hs]
    qkd = [(qkk[h][:c] * dec[h]).astype(BF16) for h in hs]
    cd = [jnp.broadcast_to(jnp.exp(glast[h]), (1, DN_HEAD_DIM)) for h in hs]
    return u, wq, qkd, kd, cd


def _scan_chunk(u, wq, qkd, kd, cd, s):
    c = u[0].shape[0]
    hs = range(len(u))
    sb = [s[h].astype(BF16) for h in hs]
    ws = [_dot(wq[h], sb[h]) for h in hs]
    yield
    vb = [(u[h] - ws[h][:c]).astype(BF16) for h in hs]
    o = [ws[h][c:] + _dot(qkd[h], vb[h]) for h in hs]
    s_new = [s[h] * cd[h] + _dot_tn(kd[h], vb[h]) for h in hs]
    yield
    return o, s_new


def _mixer_kernel(pad_rows, t_rows,
                  x_ref, s0_ref, ph0_ref, ch0_ref, nmix_ref, win_ref, wgates_ref, wab_ref, wabt_ref,
                  convw_ref, poolw_ref, pscale_ref, alog_ref, dtb_ref, alogt_ref, dtbt_ref,
                  dnw_ref, wout_ref, nffn_ref, wrt_ref, brt_ref,
                  h_ref, rf_ref, ri_ref, s_out_ref, ph_out_ref, ch_out_ref,
                  s_scr, pool_buf, conv_buf, un_scr, ypool_scr, qkv_scr, o_scr, beta_scr, gcol_scr, grow_scr,
                  u_scr, wq_scr, qkd_scr, kd_scr, cd_scr):
    t = t_rows
    j = pl.program_id(1)
    n_chunks = t // CHUNK

    @pl.when(j == 0)
    def _():
        s_scr[...] = s0_ref[...]
        pool_buf[0:POOL_HALO, :] = ph0_ref[...]
        conv_buf[0:CONV_HALO, :] = ch0_ref[...]

    x = x_ref[...]
    un = (x * lax.rsqrt(jnp.mean(x * x, axis=-1, keepdims=True) + NORM_EPS) * nmix_ref[...]).astype(BF16)

    un_scr[...] = un
    sub = POOL_GROUP_DIM
    n_sub = D_MODEL // sub

    def project(col0):
        return _dot(un, win_ref[:, col0:col0 + sub])

    def pool_group(gi):
        win = POOL_WINDOWS[gi]
        cs = slice(gi * sub, (gi + 1) * sub)
        acc = pool_buf[:, cs]
        shift = 1
        while shift < win:
            acc = acc + pltpu.roll(acc, shift, axis=0)
            shift *= 2
        pooled = acc[POOL_HALO:, :] * (1.0 / win) - pool_buf[POOL_HALO:POOL_HALO + t, cs]
        ypool_scr[:, cs] = _dot(pooled.astype(BF16), poolw_ref[gi]) * pscale_ref[:, cs]

    def conv_sub(col0):
        cs = slice(col0, col0 + sub)
        acc = convw_ref[CONV_WIDTH - 1:CONV_WIDTH, cs] * conv_buf[CONV_HALO:CONV_HALO + t, cs]
        for kk in range(CONV_WIDTH - 1):
            off = CONV_HALO - (CONV_WIDTH - 1) + kk
            acc = acc + convw_ref[kk:kk + 1, cs] * conv_buf[off:off + t, cs]
        act = acc * _sigmoid(acc)
        if col0 >= 2 * D_MODEL:
            qkv_scr[:, cs] = act
            return
        for hh in range(sub // DN_HEAD_DIM):
            part = act[:, hh * DN_HEAD_DIM:(hh + 1) * DN_HEAD_DIM]
            nrm = lax.rsqrt(jnp.sum(part * part, axis=-1, keepdims=True) + NORM_EPS)
            if col0 < D_MODEL:
                nrm = nrm * (DN_HEAD_DIM ** -0.5)
            qkv_scr[:, col0 + hh * DN_HEAD_DIM:col0 + (hh + 1) * DN_HEAD_DIM] = part * nrm

    for si in range(n_sub):
        pool_buf[POOL_HALO:POOL_HALO + t, si * sub:(si + 1) * sub] = project(si * sub)
    for si in range(n_sub):
        conv_buf[CONV_HALO:CONV_HALO + t, si * sub:(si + 1) * sub] = project(D_MODEL + si * sub)
        pool_group(si)
    for blk in range(1, 3):
        for si in range(n_sub):
            c0 = blk * D_MODEL + si * sub
            conv_buf[CONV_HALO:CONV_HALO + t, c0:c0 + sub] = project(D_MODEL + c0)
            conv_sub(c0 - D_MODEL)
    pab = _dot(un, wab_ref[...])
    pabt = _dot_nt(wabt_ref[...], un)

    gcol_all = -jnp.exp(alog_ref[...]) * _softplus(pab + dtb_ref[...])
    beta_all = _sigmoid(pab)
    ab_rows = 2 * DN_HEADS
    grow_all = (-jnp.exp(alogt_ref[0:ab_rows, 0:1])
                * _softplus(pabt[0:ab_rows, :] + dtbt_ref[0:ab_rows, 0:1]))
    if pad_rows:
        rid = lax.broadcasted_iota(jnp.int32, (t, LANES), 0)
        gcol_all = jnp.where(rid >= pad_rows, gcol_all, 0.0)
        beta_all = jnp.where(rid >= pad_rows, beta_all, 0.0)
        cid = lax.broadcasted_iota(jnp.int32, (ab_rows, t), 1)
        grow_all = jnp.where(cid >= pad_rows, grow_all, 0.0)
    beta_scr[...] = beta_all
    in_chunk_r = lax.broadcasted_iota(jnp.int32, (t, LANES), 0) % CHUNK
    in_chunk_c = lax.broadcasted_iota(jnp.int32, (ab_rows, t), 1) % CHUNK
    lane_scan = t % LANES == 0
    shift = 1
    while shift < CHUNK:
        gcol_all = gcol_all + jnp.where(in_chunk_r >= shift, pltpu.roll(gcol_all, shift, axis=0), 0.0)
        if lane_scan:
            grow_all = grow_all + jnp.where(in_chunk_c >= shift, pltpu.roll(grow_all, shift, axis=1), 0.0)
        shift *= 2
    gcol_scr[...] = gcol_all
    if not lane_scan:
        ci = lax.broadcasted_iota(jnp.int32, (t, t), 0)
        cj = lax.broadcasted_iota(jnp.int32, (t, t), 1)
        tri_u = jnp.where((ci <= cj) & (ci // CHUNK == cj // CHUNK), 1.0, 0.0).astype(F32)
        grow_all = jnp.dot(grow_all, tri_u, preferred_element_type=F32,
                           precision=lax.Precision.HIGHEST)
    for c in range(n_chunks):
        grow_scr[c] = grow_all[:, c * CHUNK:(c + 1) * CHUNK]
    for si in range(n_sub):
        conv_sub(2 * D_MODEL + si * sub)

    heads = range(DN_HEADS)

    def head_cols(base, hh):
        return slice(base + hh * DN_HEAD_DIM, base + (hh + 1) * DN_HEAD_DIM)

    def intra_stage(chunks):
        qs, ks, vs, bcols, gcols, grows = [], [], [], [], [], []
        for c in chunks:
            rs = slice(c * CHUNK, (c + 1) * CHUNK)
            gcol_c = gcol_scr[rs, :]
            beta_c = beta_scr[rs, :]
            grow_c = grow_scr[c]
            for hh in heads:
                qs.append(qkv_scr[rs, head_cols(0, hh)])
                ks.append(qkv_scr[rs, head_cols(D_MODEL, hh)])
                vs.append(qkv_scr[rs, head_cols(2 * D_MODEL, hh)])
                bcols.append(beta_c[:, DN_HEADS + hh:DN_HEADS + hh + 1])
                gcols.append(gcol_c[:, hh:hh + 1])
                grows.append(grow_c[hh:hh + 1, :])
        u, wq, qkd, kd, cd = yield from _intra_chunk(qs, ks, vs, bcols, gcols, grows)
        for ci, c in enumerate(chunks):
            for hh in heads:
                idx = ci * DN_HEADS + hh
                u_scr[c, hh] = u[idx]
                wq_scr[c, hh] = wq[idx]
                qkd_scr[c, hh] = qkd[idx]
                kd_scr[c, hh] = kd[idx]
            cd_scr[c] = jnp.concatenate(cd[ci * DN_HEADS:(ci + 1) * DN_HEADS], axis=0)

    def scan_stage(chunks):
        for c in chunks:
            cds = cd_scr[c]
            o, s_new = yield from _scan_chunk(
                [u_scr[c, hh] for hh in heads], [wq_scr[c, hh] for hh in heads],
                [qkd_scr[c, hh] for hh in heads], [kd_scr[c, hh] for hh in heads],
                [cds[hh:hh + 1, :] for hh in heads], [s_scr[hh] for hh in heads])
            for hh in heads:
                s_scr[hh] = s_new[hh]
                o_scr[c * CHUNK:(c + 1) * CHUNK, head_cols(0, hh)] = o[hh]

    def output_stage(r0, nr):
        rs = slice(r0, r0 + nr)
        unh = un_scr[rs, :]

        def gate_proj(w_ref, col0):
            return [_dot(unh, w_ref[:, col0 + si * sub:col0 + (si + 1) * sub]) for si in range(n_sub)]

        z = gate_proj(win_ref, 4 * D_MODEL)
        yield
        y_dn = []
        for hh in heads:
            o = o_scr[rs, head_cols(0, hh)]
            o = o * lax.rsqrt(jnp.mean(o * o, axis=-1, keepdims=True) + NORM_EPS) * dnw_ref[...]
            per = sub // DN_HEAD_DIM
            zz = z[hh // per][:, (hh % per) * DN_HEAD_DIM:(hh % per + 1) * DN_HEAD_DIM]
            y_dn.append(o * (zz * _sigmoid(zz)))
        g_pool = gate_proj(wgates_ref, 0)
        yield
        g_dn = gate_proj(wgates_ref, D_MODEL)
        yield
        merged = []
        per = sub // DN_HEAD_DIM
        for si in range(n_sub):
            y_dn_s = jnp.concatenate(y_dn[si * per:(si + 1) * per], axis=1)
            merged.append((_sigmoid(g_pool[si]) * ypool_scr[rs, si * sub:(si + 1) * sub]
                           + _sigmoid(g_dn[si]) * y_dn_s).astype(BF16))
        h = x_ref[rs, :] + _dot(jnp.concatenate(merged, axis=1), wout_ref[...])
        h_ref[rs, :] = h
        yield
        xt = h * lax.rsqrt(jnp.mean(h * h, axis=-1, keepdims=True) + NORM_EPS) * nffn_ref[...]
        n_logit_rows = SUBLANES + N_EXPERTS
        logits = lax.dot_general(wrt_ref[0:n_logit_rows, :], xt, (((1,), (1,)), ((), ())),
                                 preferred_element_type=F32, precision=lax.Precision.HIGHEST)
        yield
        logits = logits + brt_ref[0:n_logit_rows, 0:1]
        rid8 = lax.broadcasted_iota(jnp.int32, (SUBLANES, nr), 0)
        lg = jnp.where(rid8 < N_EXPERT_GROUPS, logits[0:SUBLANES, :], -jnp.inf)
        gmax = jnp.max(lg, axis=0, keepdims=True)
        g_idx = jnp.min(jnp.where(lg == gmax, rid8, SUBLANES), axis=0, keepdims=True)
        p_grp = 1.0 / jnp.sum(jnp.exp(lg - gmax), axis=0, keepdims=True)
        sel = jnp.zeros((EXPERTS_PER_GROUP, nr), F32)
        for gi in range(N_EXPERT_GROUPS):
            e0 = SUBLANES + gi * EXPERTS_PER_GROUP
            sel = jnp.where(g_idx == gi, logits[e0:e0 + EXPERTS_PER_GROUP, :], sel)
        m1 = jnp.max(sel, axis=0, keepdims=True)
        i1 = jnp.min(jnp.where(sel == m1, rid8, SUBLANES), axis=0, keepdims=True)
        sel2 = jnp.where(rid8 == i1, -jnp.inf, sel)
        m2 = jnp.max(sel2, axis=0, keepdims=True)
        i2 = jnp.min(jnp.where(sel2 == m2, rid8, SUBLANES), axis=0, keepdims=True)
        e21 = jnp.exp(m2 - m1)
        w1 = 1.0 / (1.0 + e21)
        c1 = p_grp * w1
        c2 = p_grp * (e21 * w1)
        id1 = g_idx * EXPERTS_PER_GROUP + i1
        id2 = g_idx * EXPERTS_PER_GROUP + i2
        rf_ref[:, rs] = jnp.where(rid8 == 0, c1, jnp.where(rid8 == 1, c2, 0.0))
        ri_ref[:, rs] = jnp.where(rid8 == 0, id1, jnp.where(rid8 == 1, id2, 0))

    group = 2 if n_chunks % 2 == 0 else 1
    groups = [list(range(g0, g0 + group)) for g0 in range(0, n_chunks, group)]
    _interleave(intra_stage(groups[0]))
    for gi in range(1, len(groups)):
        _interleave(intra_stage(groups[gi]), scan_stage(groups[gi - 1]))
    rows_before_last = groups[-1][0] * CHUNK
    if rows_before_last >= LANES:
        _interleave(scan_stage(groups[-1]), output_stage(0, rows_before_last))
        _interleave(output_stage(rows_before_last, t - rows_before_last))
    else:
        _interleave(scan_stage(groups[-1]))
        _interleave(output_stage(0, t))

    pool_buf[0:POOL_HALO, :] = pool_buf[t:t + POOL_HALO, :]
    conv_buf[0:CONV_HALO, :] = conv_buf[t:t + CONV_HALO, :]
    s_out_ref[...] = s_scr[...]
    ph_out_ref[...] = pool_buf[0:POOL_HALO, :]
    ch_out_ref[...] = conv_buf[0:CONV_HALO, :]


def _softplus(x):
    return jnp.maximum(x, 0.0) + jnp.log1p(jnp.exp(-jnp.abs(x)))


def _mixer(x2d, s0, ph0, ch0, params, *, batch, t_rows, pad_rows):
    n = x2d.shape[0]
    n_t = n // batch // t_rows
    t = t_rows
    row_blk = lambda b, j: (b * n_t + j, 0)
    col_blk = lambda b, j: (0, b * n_t + j)
    const2 = lambda b, j: (0, 0)
    const3 = lambda b, j: (0, 0, 0)
    (nmix, w_main, w_gates, wab, wabt,
     convw, poolw, pscale, alog, dtb, alogt, dtbt, dnw, wout, nffn, wrt, brt) = params
    in_specs = [
        pl.BlockSpec((t, D_MODEL), row_blk),
        pl.BlockSpec((DN_HEADS, DN_HEAD_DIM, DN_HEAD_DIM), const3),
        pl.BlockSpec((POOL_HALO, D_MODEL), const2),
        pl.BlockSpec((CONV_HALO, 3 * D_MODEL), const2),
        pl.BlockSpec(nmix.shape, const2),
        pl.BlockSpec(w_main.shape, const2, pipeline_mode=pl.Buffered(1)),
        pl.BlockSpec(w_gates.shape, const2, pipeline_mode=pl.Buffered(1)),
        pl.BlockSpec(wab.shape, const2),
        pl.BlockSpec(wabt.shape, const2),
        pl.BlockSpec(convw.shape, const2),
        pl.BlockSpec(poolw.shape, const3),
        pl.BlockSpec(pscale.shape, const2),
        pl.BlockSpec(alog.shape, const2),
        pl.BlockSpec(dtb.shape, const2),
        pl.BlockSpec(alogt.shape, const2),
        pl.BlockSpec(dtbt.shape, const2),
        pl.BlockSpec(dnw.shape, const2),
        pl.BlockSpec(wout.shape, const2),
        pl.BlockSpec(nffn.shape, const2),
        pl.BlockSpec(wrt.shape, const2),
        pl.BlockSpec(brt.shape, const2),
    ]
    out_specs = [
        pl.BlockSpec((t, D_MODEL), row_blk),
        pl.BlockSpec((SUBLANES, t), col_blk),
        pl.BlockSpec((SUBLANES, t), col_blk),
        pl.BlockSpec((DN_HEADS, DN_HEAD_DIM, DN_HEAD_DIM), const3),
        pl.BlockSpec((POOL_HALO, D_MODEL), const2),
        pl.BlockSpec((CONV_HALO, 3 * D_MODEL), const2),
    ]
    out_shape = [
        jax.ShapeDtypeStruct((n, D_MODEL), F32),
        jax.ShapeDtypeStruct((SUBLANES, n), F32),
        jax.ShapeDtypeStruct((SUBLANES, n), jnp.int32),
        jax.ShapeDtypeStruct((DN_HEADS, DN_HEAD_DIM, DN_HEAD_DIM), F32),
        jax.ShapeDtypeStruct((POOL_HALO, D_MODEL), F32),
        jax.ShapeDtypeStruct((CONV_HALO, 3 * D_MODEL), F32),
    ]
    scratch = [
        pltpu.VMEM((DN_HEADS, DN_HEAD_DIM, DN_HEAD_DIM), F32),
        pltpu.VMEM((t + POOL_HALO, D_MODEL), F32),
        pltpu.VMEM((t + CONV_HALO, 3 * D_MODEL), F32),
        pltpu.VMEM((t, D_MODEL), BF16),
        pltpu.VMEM((t, D_MODEL), F32),
        pltpu.VMEM((t, 3 * D_MODEL), F32),
        pltpu.VMEM((t, D_MODEL), F32),
        pltpu.VMEM((t, LANES), F32),
        pltpu.VMEM((t, LANES), F32),
        pltpu.VMEM((t // CHUNK, 2 * DN_HEADS, CHUNK), F32),
        pltpu.VMEM((t // CHUNK, DN_HEADS, CHUNK, DN_HEAD_DIM), F32),
        pltpu.VMEM((t // CHUNK, DN_HEADS, 2 * CHUNK, DN_HEAD_DIM), BF16),
        pltpu.VMEM((t // CHUNK, DN_HEADS, CHUNK, CHUNK), BF16),
        pltpu.VMEM((t // CHUNK, DN_HEADS, CHUNK, DN_HEAD_DIM), BF16),
        pltpu.VMEM((t // CHUNK, DN_HEADS, DN_HEAD_DIM), F32),
    ]
    return pl.pallas_call(
        functools.partial(_mixer_kernel, pad_rows, t_rows),
        grid=(batch, n_t),
        in_specs=in_specs,
        out_specs=out_specs,
        out_shape=out_shape,
        scratch_shapes=scratch,
        compiler_params=pltpu.CompilerParams(
            dimension_semantics=("arbitrary", "arbitrary"), vmem_limit_bytes=VMEM_LIMIT),
        name="mixer_meta" if pad_rows else "mixer",
    )(x2d, s0, ph0, ch0, *params)


def _max_tiles(n_tokens):
    return 2 * n_tokens // MOE_TILE + N_EXPERTS


def _route_kernel(ri_ref, tri_ref, pos_ref, tinfo_ref, cnt_scr, base_scr):
    ph = pl.program_id(0)
    i = pl.program_id(1)
    rb = ri_ref.shape[1]
    eid = lax.broadcasted_iota(jnp.int32, (N_EXPERTS, rb), 0)
    oh1 = jnp.where(ri_ref[0:1, :] == eid, 1.0, 0.0).astype(F32)
    oh2 = jnp.where(ri_ref[1:2, :] == eid, 1.0, 0.0).astype(F32)
    ohs = oh1 + oh2

    @pl.when((ph == 0) & (i == 0))
    def _():
        cnt_scr[...] = jnp.zeros_like(cnt_scr)

    @pl.when(ph == 0)
    def _():
        cnt_scr[...] += ohs

    @pl.when((ph == 1) & (i == 0))
    def _():
        counts = jnp.sum(cnt_scr[...], axis=1, keepdims=True)
        padded = jnp.floor((counts + (MOE_TILE - 1)) * (1.0 / MOE_TILE)) * MOE_TILE
        padded_b = jnp.broadcast_to(padded, (N_EXPERTS, LANES))
        r = lax.broadcasted_iota(jnp.int32, (N_EXPERTS, N_EXPERTS), 0)
        c = lax.broadcasted_iota(jnp.int32, (N_EXPERTS, N_EXPERTS), 1)
        strict = jnp.where(r > c, 1.0, 0.0).astype(F32)
        offs = jnp.dot(strict, padded_b, preferred_element_type=F32,
                       precision=lax.Precision.HIGHEST)
        base_scr[...] = offs
        ends = offs[:, 0:1] + padded
        total = jnp.sum(padded, axis=0, keepdims=True)
        n_lanes = tinfo_ref.shape[1]
        tile_start = (lax.broadcasted_iota(jnp.int32, (N_EXPERTS, n_lanes), 1) * MOE_TILE).astype(F32)
        texp = jnp.sum(jnp.where(ends <= tile_start, 1.0, 0.0), axis=0, keepdims=True)
        last_active = jnp.sum(jnp.where(ends <= total - MOE_TILE, 1.0, 0.0), axis=0, keepdims=True)
        texp = jnp.minimum(texp, last_active)
        nact = jnp.broadcast_to(total * (1.0 / MOE_TILE), (1, n_lanes))
        rid = lax.broadcasted_iota(jnp.int32, tinfo_ref.shape, 0)
        tinfo_ref[...] = jnp.where(rid == 0, texp, jnp.where(rid == 1, nact, 0.0)).astype(jnp.int32)

    @pl.when(ph == 1)
    def _():
        incl = _dot(ohs.astype(BF16), tri_ref[...])
        slot = base_scr[:, 0:1] + incl - ohs
        pos1 = jnp.sum(oh1 * slot, axis=0, keepdims=True)
        pos2 = jnp.sum(oh2 * slot, axis=0, keepdims=True)
        rid = lax.broadcasted_iota(jnp.int32, pos_ref.shape, 0)
        pos_ref[...] = jnp.where(rid == 0, pos1, jnp.where(rid == 1, pos2, 0.0)).astype(jnp.int32)
        base_scr[...] += jnp.sum(ohs, axis=1, keepdims=True)


def _route(ri, tri, *, n_tile_lanes):
    n = ri.shape[1]
    rb = min(ROUTE_BLOCK, n)
    return pl.pallas_call(
        _route_kernel,
        grid=(2, n // rb),
        in_specs=[
            pl.BlockSpec((SUBLANES, rb), lambda p, i: (0, i)),
            pl.BlockSpec((rb, rb), lambda p, i: (0, 0)),
        ],
        out_specs=[
            pl.BlockSpec((SUBLANES, rb), lambda p, i: (0, i * p)),
            pl.BlockSpec((SUBLANES, n_tile_lanes), lambda p, i: (0, 0)),
        ],
        out_shape=[
            jax.ShapeDtypeStruct((SUBLANES, n), jnp.int32),
            jax.ShapeDtypeStruct((SUBLANES, n_tile_lanes), jnp.int32),
        ],
        scratch_shapes=[pltpu.VMEM((N_EXPERTS, rb), F32), pltpu.VMEM((N_EXPERTS, LANES), F32)],
        compiler_params=pltpu.CompilerParams(dimension_semantics=("arbitrary", "arbitrary")),
        name="route",
    )(ri, tri)


def _invert_kernel(tb, pos_hbm, zero_hbm, tok_hbm, idx_a, idx_b, tok_smem, idx_sems, out_sem):
    i = pl.program_id(0)
    n_steps = pl.num_programs(0)
    idx_bufs = (idx_a, idx_b)

    def idx_copy(step, par):
        return pltpu.make_async_copy(pos_hbm.at[step], idx_bufs[par], idx_sems.at[par])

    @pl.when(i == 0)
    def _():
        idx_copy(0, 0).start()
        clear = pltpu.make_async_copy(zero_hbm, tok_smem, out_sem)
        clear.start()
        clear.wait()

    def step(par):
        @pl.when(i + 1 < n_steps)
        def _():
            idx_copy(i + 1, 1 - par).start()

        idx_copy(i, par).wait()
        base = i * tb

        def fill(r, carry):
            for k in range(2):
                tok_smem[idx_bufs[par][k * tb + r]] = base + r
            return carry
        lax.fori_loop(0, tb, fill, 0, unroll=8)

    @pl.when(i % 2 == 0)
    def _():
        step(0)

    @pl.when(i % 2 == 1)
    def _():
        step(1)

    @pl.when(i == n_steps - 1)
    def _():
        out_cp = pltpu.make_async_copy(tok_smem, tok_hbm, out_sem)
        out_cp.start()
        out_cp.wait()


def _invert(pos_blocks, n_slots):
    tb = pos_blocks.shape[1] // 2
    return pl.pallas_call(
        functools.partial(_invert_kernel, tb),
        grid=(pos_blocks.shape[0],),
        in_specs=[pl.BlockSpec(memory_space=pl.ANY), pl.BlockSpec(memory_space=pl.ANY)],
        out_specs=pl.BlockSpec(memory_space=pl.ANY),
        out_shape=jax.ShapeDtypeStruct((n_slots,), jnp.int32),
        scratch_shapes=[pltpu.SMEM((2 * tb,), jnp.int32), pltpu.SMEM((2 * tb,), jnp.int32),
                        pltpu.SMEM((n_slots,), jnp.int32),
                        pltpu.SemaphoreType.DMA((2,)), pltpu.SemaphoreType.DMA(())],
        compiler_params=pltpu.CompilerParams(dimension_semantics=("arbitrary",)),
        name="invert_slots",
    )(pos_blocks, jnp.zeros((n_slots,), jnp.int32))


def _experts_kernel(texp_ref, nact_ref, tok_hbm, h_hbm, nffn_ref, wg_ref, wu_ref, wd_ref, ys_ref,
                    tok_a, tok_b, tok_c, xbuf, wg_b, wu_b, wd_b, idx_sems, row_sems):
    i = pl.program_id(0)
    n_steps = pl.num_programs(0)
    n_act = nact_ref[0]
    tok_bufs = (tok_a, tok_b, tok_c)
    depth = len(tok_bufs)

    def idx_copy(step, par):
        return pltpu.make_async_copy(tok_hbm.at[pl.ds(step * MOE_TILE, MOE_TILE)], tok_bufs[par],
                                     idx_sems.at[par])

    def row_copy(par, r, token):
        return pltpu.make_async_copy(h_hbm.at[pl.ds(token, 1)], xbuf.at[par, pl.ds(r, 1)],
                                     row_sems.at[par])

    def issue_rows(par):
        def body(g, carry):
            for k in range(2):
                r = 2 * g + k
                row_copy(par, r, tok_bufs[par][r]).start(priority=k)
            return carry
        lax.fori_loop(0, MOE_TILE // 2, body, 0, unroll=4)

    def drain_rows(par):
        pltpu.make_async_copy(h_hbm.at[pl.ds(0, MOE_TILE)], xbuf.at[par], row_sems.at[par]).wait()

    e = texp_ref[i]
    e_prev = texp_ref[jnp.maximum(i - 1, 0)]

    @pl.when((i == 0) | (e != e_prev))
    def _():
        wg_b[...] = wg_ref[0].astype(BF16)
        wu_b[...] = wu_ref[0].astype(BF16)
        wd_b[...] = wd_ref[0].astype(BF16)

    def step(par):
        ahead = (par + 2) % depth

        @pl.when(i == 0)
        def _():
            idx_copy(0, 0).start()
            idx_copy(0, 0).wait()
            issue_rows(0)

            @pl.when(n_steps > 1)
            def _():
                idx_copy(1, 1).start()
                idx_copy(1, 1).wait()

            @pl.when(n_act > 1)
            def _():
                issue_rows(1)

            @pl.when(n_steps > 2)
            def _():
                idx_copy(2, 2).start()

        @pl.when(i + 2 < n_steps)
        def _():
            idx_copy(i + 2, ahead).wait()

        @pl.when(i + 3 < n_steps)
        def _():
            idx_copy(i + 3, par).start()

        def compute_stages():
            n_q = 4
            rq = MOE_TILE // n_q
            xq = []
            for qi in range(n_q):
                hrow = xbuf[par, qi * rq:(qi + 1) * rq, :]
                xq.append((hrow * lax.rsqrt(jnp.mean(hrow * hrow, axis=-1, keepdims=True) + NORM_EPS)
                           * nffn_ref[...]).astype(BF16))
                yield
            x = jnp.concatenate(xq, axis=0)
            half = D_FF_EXPERT // 2
            gate, up = [], []
            for cb in range(2):
                gate.append(_dot(x, wg_b[:, cb * half:(cb + 1) * half]))
                yield
            for cb in range(2):
                up.append(_dot(x, wu_b[:, cb * half:(cb + 1) * half]))
                yield
            hq = []
            for qi in range(n_q):
                rs = slice(qi * rq, (qi + 1) * rq)
                hq.append(jnp.concatenate(
                    [(gate[cb][rs] * _sigmoid(gate[cb][rs]) * up[cb][rs]).astype(BF16)
                     for cb in range(2)], axis=1))
                yield
            hdn = jnp.concatenate(hq, axis=0)
            quarter = D_MODEL // 4
            for cb in range(4):
                ys_ref[:, cb * quarter:(cb + 1) * quarter] = _dot(
                    hdn, wd_b[:, cb * quarter:(cb + 1) * quarter])
                yield

        def issue_stages(per_stage=16):
            for r0 in range(0, MOE_TILE, per_stage):
                for r in range(r0, r0 + per_stage):
                    row_copy(ahead, r, tok_bufs[ahead][r]).start(priority=r % 2)
                yield

        @pl.when(i + 2 < n_act)
        def _():
            drain_rows(par)
            _interleave(compute_stages(), issue_stages())

        @pl.when((i < n_act) & (i + 2 >= n_act))
        def _():
            drain_rows(par)
            _interleave(compute_stages())

        @pl.when(i >= n_act)
        def _():
            ys_ref[...] = jnp.zeros_like(ys_ref)

    for par in range(depth):
        pl.when(i % depth == par)(functools.partial(step, par))


def _experts(texp, nact, tok, h, nffn, wg, wu, wd):
    n_tiles = tok.shape[0] // MOE_TILE
    w_blk = lambda i, texp, nact: (texp[i], 0, 0)
    grid_spec = pltpu.PrefetchScalarGridSpec(
        num_scalar_prefetch=2,
        grid=(n_tiles,),
        in_specs=[
            pl.BlockSpec(memory_space=pl.ANY),
            pl.BlockSpec(memory_space=pl.ANY),
            pl.BlockSpec((1, D_MODEL), lambda i, texp, nact: (0, 0)),
            pl.BlockSpec((1, D_MODEL, D_FF_EXPERT), w_blk),
            pl.BlockSpec((1, D_MODEL, D_FF_EXPERT), w_blk),
            pl.BlockSpec((1, D_FF_EXPERT, D_MODEL), w_blk),
        ],
        out_specs=pl.BlockSpec((MOE_TILE, D_MODEL), lambda i, texp, nact: (i, 0)),
        scratch_shapes=[pltpu.SMEM((MOE_TILE,), jnp.int32), pltpu.SMEM((MOE_TILE,), jnp.int32),
                        pltpu.SMEM((MOE_TILE,), jnp.int32),
                        pltpu.VMEM((3, MOE_TILE, D_MODEL), F32),
                        pltpu.VMEM((D_MODEL, D_FF_EXPERT), BF16),
                        pltpu.VMEM((D_MODEL, D_FF_EXPERT), BF16),
                        pltpu.VMEM((D_FF_EXPERT, D_MODEL), BF16),
                        pltpu.SemaphoreType.DMA((3,)), pltpu.SemaphoreType.DMA((3,))],
    )
    return pl.pallas_call(
        _experts_kernel,
        grid_spec=grid_spec,
        out_shape=jax.ShapeDtypeStruct((tok.shape[0], D_MODEL), F32),
        compiler_params=pltpu.CompilerParams(
            dimension_semantics=("arbitrary",), vmem_limit_bytes=VMEM_LIMIT),
        name="experts",
    )(texp, nact, tok, h, nffn, wg, wu, wd)


def _combine_kernel(pos_hbm, ys_hbm, h_ref, rf_ref, nfin_ref, out_ref,
                    idx_a, idx_b, idx_c, ybuf, idx_sems, row_sems):
    i = pl.program_id(0)
    n_steps = pl.num_programs(0)
    tb = h_ref.shape[0]
    idx_bufs = (idx_a, idx_b, idx_c)
    depth = len(idx_bufs)

    def idx_copy(step, par):
        return pltpu.make_async_copy(pos_hbm.at[step], idx_bufs[par], idx_sems.at[par])

    def issue_rows(par):
        def body(g, carry):
            for sub in range(SUBLANES):
                r = g * SUBLANES + sub
                for k in range(2):
                    pltpu.make_async_copy(
                        ys_hbm.at[pl.ds(idx_bufs[par][k * tb + r], 1)],
                        ybuf.at[par, k, pl.ds(r, 1)],
                        row_sems.at[par]).start(priority=k)
            return carry
        lax.fori_loop(0, tb // SUBLANES, body, 0)

    def drain_rows(par):
        for k in range(2):
            pltpu.make_async_copy(ys_hbm.at[pl.ds(0, tb)], ybuf.at[par, k], row_sems.at[par]).wait()

    def step(par):
        ahead = (par + 2) % depth

        @pl.when(i == 0)
        def _():
            idx_copy(0, 0).start()
            idx_copy(0, 0).wait()
            issue_rows(0)

            @pl.when(n_steps > 1)
            def _():
                idx_copy(1, 1).start()
                idx_copy(1, 1).wait()
                issue_rows(1)

            @pl.when(n_steps > 2)
            def _():
                idx_copy(2, 2).start()

        @pl.when(i + 2 < n_steps)
        def _():
            idx_copy(i + 2, ahead).wait()

        @pl.when(i + 3 < n_steps)
        def _():
            idx_copy(i + 3, par).start()

        n_stage = 8
        rows = tb // n_stage

        def compute_stages():
            rf = rf_ref[...]
            rf_cols = jnp.transpose(jnp.concatenate(
                [rf, jnp.zeros((LANES - SUBLANES, tb), F32)], axis=0))
            for st in range(n_stage):
                rs = slice(st * rows, (st + 1) * rows)
                hh = (h_ref[rs, :] + rf_cols[rs, 0:1] * ybuf[par, 0, rs, :]
                      + rf_cols[rs, 1:2] * ybuf[par, 1, rs, :])
                out_ref[rs, :] = (hh * lax.rsqrt(jnp.mean(hh * hh, axis=-1, keepdims=True) + NORM_EPS)
                                  * nfin_ref[...])
                yield

        def issue_stages():
            for st in range(n_stage):
                for r in range(st * rows, (st + 1) * rows):
                    for k in range(2):
                        pltpu.make_async_copy(
                            ys_hbm.at[pl.ds(idx_bufs[ahead][k * tb + r], 1)],
                            ybuf.at[ahead, k, pl.ds(r, 1)],
                            row_sems.at[ahead]).start(priority=k)
                yield

        drain_rows(par)

        @pl.when(i + 2 < n_steps)
        def _():
            _interleave(issue_stages(), compute_stages())

        @pl.when(i + 2 >= n_steps)
        def _():
            _interleave(compute_stages())

    for par in range(depth):
        pl.when(i % depth == par)(functools.partial(step, par))


def _combine(pos_blocks, ys, h, rf, nfin):
    n = h.shape[0]
    tb = pos_blocks.shape[1] // 2
    return pl.pallas_call(
        _combine_kernel,
        grid=(n // tb,),
        in_specs=[
            pl.BlockSpec(memory_space=pl.ANY),
            pl.BlockSpec(memory_space=pl.ANY),
            pl.BlockSpec((tb, D_MODEL), lambda i: (i, 0)),
            pl.BlockSpec((SUBLANES, tb), lambda i: (0, i)),
            pl.BlockSpec((1, D_MODEL), lambda i: (0, 0)),
        ],
        out_specs=pl.BlockSpec((tb, D_MODEL), lambda i: (i, 0)),
        out_shape=jax.ShapeDtypeStruct((n, D_MODEL), F32),
        scratch_shapes=[pltpu.SMEM((2 * tb,), jnp.int32), pltpu.SMEM((2 * tb,), jnp.int32),
                        pltpu.SMEM((2 * tb,), jnp.int32),
                        pltpu.VMEM((3, 2, tb, D_MODEL), F32),
                        pltpu.SemaphoreType.DMA((3,)), pltpu.SemaphoreType.DMA((3,))],
        compiler_params=pltpu.CompilerParams(dimension_semantics=("arbitrary",)),
        name="combine",
    )(pos_blocks, ys, h, rf, nfin)


def _moe(h, rf, ri, nffn, wg, wu, wd, nfin):
    n = h.shape[0]
    max_tiles = _max_tiles(n)
    n_tile_lanes = -(-max_tiles // LANES) * LANES
    rb = min(ROUTE_BLOCK, n)
    tri = jnp.triu(jnp.ones((rb, rb), BF16))
    pos, tinfo = _route(ri, tri, n_tile_lanes=n_tile_lanes)
    texp = tinfo[0, :max_tiles]
    nact = tinfo[1, :1]
    tb = min(ROW_BLOCK, n)
    pos_blocks = pos[0:2].reshape(2, n // tb, tb).transpose(1, 0, 2).reshape(n // tb, 2 * tb)
    tok = _invert(pos_blocks, max_tiles * MOE_TILE)
    ys = _experts(texp, nact, tok, h, nffn, wg, wu, wd)
    return _combine(pos_blocks, ys, h, rf, nfin)


def _pad_lanes_row(v):
    return jnp.pad(v.astype(F32), (0, LANES - v.shape[0]))[None, :]


def _block_forward(x, meta_tokens, norm_mix_w, w_in, conv_w, pool_w, pool_scale, a_log, dt_bias,
                   dn_norm_w, w_out, norm_ffn_w, router_group_w, router_group_b, router_expert_w,
                   router_expert_b, expert_w_gate, expert_w_up, expert_w_down, norm_final_w,
                   *, mixer_rows):
    bsz, seq, _ = x.shape
    n = bsz * seq
    x2d = x.reshape(n, D_MODEL)

    ab0 = 5 * D_MODEL
    w_main = w_in[:, :ab0].astype(BF16)
    w_gates = w_in[:, ab0 + 2 * DN_HEADS:].astype(BF16)
    wab = jnp.pad(w_in[:, ab0:ab0 + 2 * DN_HEADS], ((0, 0), (0, LANES - 2 * DN_HEADS))).astype(BF16)
    wabt = wab.T
    nmix = norm_mix_w[None, :]
    alog = _pad_lanes_row(a_log)
    dtb = _pad_lanes_row(dt_bias)
    alogt = jnp.broadcast_to(alog.T, (LANES, LANES))
    dtbt = jnp.broadcast_to(dtb.T, (LANES, LANES))
    wr = jnp.zeros((D_MODEL, LANES), F32)
    wr = wr.at[:, 0:N_EXPERT_GROUPS].set(router_group_w)
    wr = wr.at[:, SUBLANES:SUBLANES + N_EXPERTS].set(router_expert_w)
    br = jnp.zeros((LANES,), F32)
    br = br.at[0:N_EXPERT_GROUPS].set(router_group_b)
    br = br.at[SUBLANES:SUBLANES + N_EXPERTS].set(router_expert_b)
    params = (nmix, w_main, w_gates, wab, wabt,
              conv_w, pool_w.astype(BF16), pool_scale[None, :], alog, dtb, alogt, dtbt,
              dn_norm_w[None, :], w_out.astype(BF16), norm_ffn_w[None, :],
              wr.T, jnp.broadcast_to(br[:, None], (LANES, LANES)))

    pad_rows = CHUNK - N_META
    xm = jnp.concatenate([jnp.zeros((pad_rows, D_MODEL), F32), meta_tokens], axis=0)
    zeros_s = jnp.zeros((DN_HEADS, DN_HEAD_DIM, DN_HEAD_DIM), F32)
    zeros_ph = jnp.zeros((POOL_HALO, D_MODEL), F32)
    zeros_ch = jnp.zeros((CONV_HALO, 3 * D_MODEL), F32)
    meta_out = _mixer(xm, zeros_s, zeros_ph, zeros_ch, params,
                      batch=1, t_rows=CHUNK, pad_rows=pad_rows)
    s_meta, ph_meta, ch_meta = meta_out[3], meta_out[4], meta_out[5]

    h, rf, ri, _, _, _ = _mixer(x2d, s_meta, ph_meta, ch_meta, params,
                                batch=bsz, t_rows=mixer_rows, pad_rows=0)
    wg = expert_w_gate.reshape(N_EXPERTS, D_MODEL, D_FF_EXPERT)
    wu = expert_w_up.reshape(N_EXPERTS, D_MODEL, D_FF_EXPERT)
    wd = expert_w_down.reshape(N_EXPERTS, D_FF_EXPERT, D_MODEL)
    out = _moe(h, rf, ri, norm_ffn_w[None, :], wg, wu, wd, norm_final_w[None, :])
    return out.reshape(bsz, seq, D_MODEL)


def kernel(x, meta_tokens, norm_mix_w, w_in, conv_w, pool_w, pool_scale, a_log, dt_bias, dn_norm_w, w_out, norm_ffn_w, router_group_w, router_group_b, router_expert_w, router_expert_b, expert_w_gate, expert_w_up, expert_w_down, norm_final_w):
    assert norm_mix_w.shape[0] == 1, "single-layer block"
    seq = x.shape[1]
    return _block_forward(
        x, meta_tokens, norm_mix_w[0], w_in[0], conv_w[0], pool_w[0], pool_scale[0], a_log[0],
        dt_bias[0], dn_norm_w[0], w_out[0], norm_ffn_w[0], router_group_w[0], router_group_b[0],
        router_expert_w[0], router_expert_b[0], expert_w_gate[0], expert_w_up[0], expert_w_down[0],
        norm_final_w,
        mixer_rows=min(256, seq))
```
